```python
import math
import jax, jax.numpy as jnp
from jax import lax
import numpy as np

D_MODEL = 2048
BATCH = 4
SEQ = 4096
DEPTH = 2

CHUNK = 64
EPS = 1e-6
MIX_WIDTH = 2 * D_MODEL
N_EVEN = (DEPTH + 1) // 2
N_ODD = DEPTH // 2
SSD_HEADDIM = 64
SSD_INNER = 3 * MIX_WIDTH // 4
SSD_HEADS = SSD_INNER // SSD_HEADDIM
SSD_GROUPS = 8
SSD_HPG = SSD_HEADS // SSD_GROUPS
SSD_STATE = 128
SSD_CONV = 4
SSD_XBC = SSD_INNER + 2 * SSD_GROUPS * SSD_STATE
S5_WIDTH = MIX_WIDTH - SSD_INNER
S5_GROUP_CH = 16
S5_GROUPS = S5_WIDTH // S5_GROUP_CH
S5_STATE = 64
IN0_WIDTH = SSD_INNER + SSD_XBC + SSD_HEADS + S5_WIDTH
RET_HEADS = 8
RET_QK = D_MODEL // RET_HEADS
RET_V = MIX_WIDTH // RET_HEADS
IN1_WIDTH = 2 * D_MODEL + 2 * MIX_WIDTH
ROPE_BASE = 10000.0
FFN_HIDDEN = ((-(-8 * D_MODEL // 3) + 255) // 256) * 256

kernel_name = "hybrid_ssd_s5_retention_adaln_trunk"


def rmsnorm(x, g):
    xf = x.astype(jnp.float32)
    y = xf * lax.rsqrt(jnp.mean(xf * xf, axis=-1, keepdims=True) + EPS)
    return (y * g.astype(jnp.float32)).astype(x.dtype)


def ada_modulation(c, w, b):
    m = (jnp.dot(jax.nn.silu(c), w) + b)[:, None, :]
    shift, scale, gate = jnp.split(m, 3, axis=-1)
    return shift, scale, gate


def causal_depthwise_conv(x, w, b):
    k = w.shape[0]
    y = lax.conv_general_dilated(x, w[:, None, :], window_strides=(1,), padding=[(k - 1, 0)],
                                 dimension_numbers=('NWC', 'WIO', 'NWC'),
                                 feature_group_count=x.shape[-1])
    return y + b


def to_chunks(t):
    b, l = t.shape[:2]
    return jnp.moveaxis(t.reshape(b, l // CHUNK, CHUNK, *t.shape[2:]), 1, 0)


def from_chunks(t):
    t = jnp.moveaxis(t, 0, 1)
    return t.reshape(t.shape[0], t.shape[1] * t.shape[2], *t.shape[3:])


def ssd_chunk_scan(xh, dt, a, bm, cm):
    b = xh.shape[0]
    causal = jnp.tril(jnp.ones((CHUNK, CHUNK), dtype=bool))

    def step(state, inp):
        xc, dtc, bc, cc = inp
        acum = jnp.cumsum(dtc * a, axis=1)
        seg = acum[:, :, None] - acum[:, None, :]
        lmat = jnp.exp(jnp.where(causal[None, :, :, None, None], seg, -jnp.inf))
        xdt = xc * dtc[..., None]
        scores = jnp.einsum('blgn,bsgn->blsg', cc, bc)
        y_diag = jnp.einsum('blsg,blsgh,bsghp->blghp', scores, lmat, xdt)
        y_off = jnp.einsum('blgn,bghpn->blghp', cc, state) * jnp.exp(acum)[..., None]
        decay = jnp.exp(acum[:, -1:] - acum)
        new_state = (state * jnp.exp(acum[:, -1])[..., None, None]
                     + jnp.einsum('blgn,blgh,blghp->bghpn', bc, decay, xdt))
        return new_state, y_diag + y_off

    state0 = jnp.zeros((b, SSD_GROUPS, SSD_HPG, SSD_HEADDIM, SSD_STATE), jnp.float32)
    _, y = lax.scan(step, state0, (to_chunks(xh), to_chunks(dt), to_chunks(bm), to_chunks(cm)))
    return from_chunks(y)


def linear_recurrence_op(e1, e2):
    a1, b1 = e1
    a2, b2 = e2
    return a1 * a2, a2 * b1 + b2


def s5_branch(u, a_re, a_im, log_dt, b_re, b_im, c_re, c_im, d, glu_w, glu_b):
    f32 = jnp.float32
    b, l, _ = u.shape
    uf = u.astype(f32)
    ug = uf.reshape(b, l, S5_GROUPS, S5_GROUP_CH)
    lam = lax.complex(a_re.astype(f32), a_im.astype(f32))
    delta = jnp.exp(log_dt.astype(f32))[:, None]
    lam_bar = jnp.exp(lam * delta)
    b_bar = ((lam_bar - 1) / lam)[..., None] * lax.complex(b_re.astype(f32), b_im.astype(f32))
    bu = jnp.einsum('gpc,blgc->blgp', b_bar, ug.astype(jnp.complex64))
    a_seq = jnp.broadcast_to(lam_bar, (1, l) + lam_bar.shape)
    _, states = lax.associative_scan(linear_recurrence_op, (a_seq, bu), axis=1)
    c_mat = lax.complex(c_re.astype(f32), c_im.astype(f32))
    y = jnp.real(jnp.einsum('gcp,blgp->blgc', c_mat, states)).reshape(b, l, S5_WIDTH)
    y = jax.nn.gelu(y + d.astype(f32) * uf)
    y = y * jax.nn.sigmoid(y @ glu_w.astype(f32) + glu_b.astype(f32))
    return y.astype(u.dtype)


def ssd_s5_mixer(h, w_in, conv_w, conv_b, dt_bias, a_log, d_skip, norm_g,
                 a_re, a_im, log_dt, b_re, b_im, c_re, c_im, s5_d, glu_w, glu_b, w_out):
    f32 = jnp.float32
    dtype = h.dtype
    b, l, _ = h.shape
    proj = h @ w_in
    z, xbc, dt_raw, u = jnp.split(proj, [SSD_INNER, SSD_INNER + SSD_XBC,
                                         SSD_INNER + SSD_XBC + SSD_HEADS], axis=-1)
    xbc = jax.nn.silu(causal_depthwise_conv(xbc, conv_w, conv_b))
    xs, bm, cm = jnp.split(xbc, [SSD_INNER, SSD_INNER + SSD_GROUPS * SSD_STATE], axis=-1)
    xs = xs.astype(f32).reshape(b, l, SSD_GROUPS, SSD_HPG, SSD_HEADDIM)
    bm = bm.astype(f32).reshape(b, l, SSD_GROUPS, SSD_STATE)
    cm = cm.astype(f32).reshape(b, l, SSD_GROUPS, SSD_STATE)
    dt = jax.nn.softplus(dt_raw.astype(f32) + dt_bias.astype(f32)).reshape(b, l, SSD_GROUPS, SSD_HPG)
    a = -jnp.exp(a_log.astype(f32)).reshape(SSD_GROUPS, SSD_HPG)
    y = ssd_chunk_scan(xs, dt, a, bm, cm) + d_skip.astype(f32).reshape(SSD_GROUPS, SSD_HPG, 1) * xs
    y = y.reshape(b, l, SSD_GROUPS, SSD_HPG * SSD_HEADDIM) * \
        jax.nn.silu(z.astype(f32)).reshape(b, l, SSD_GROUPS, SSD_HPG * SSD_HEADDIM)
    y = y * lax.rsqrt(jnp.mean(y * y, axis=-1, keepdims=True) + EPS)
    y_a = (y.reshape(b, l, SSD_INNER) * norm_g.astype(f32)).astype(dtype)
    y_b = s5_branch(u, a_re, a_im, log_dt, b_re, b_im, c_re, c_im, s5_d, glu_w, glu_b)
    return jnp.concatenate([y_a, y_b], axis=-1) @ w_out


def rotary(t, cos, sin):
    t1, t2 = jnp.split(t, 2, axis=-1)
    return jnp.concatenate([t1 * cos - t2 * sin, t1 * sin + t2 * cos], axis=-1)


def retention_chunkwise(q, k, v):
    f32 = jnp.float32
    b = q.shape[0]
    log_g = jnp.log(1.0 - 2.0 ** (-5.0 - jnp.arange(RET_HEADS, dtype=f32)))
    idx = jnp.arange(CHUNK, dtype=f32)
    d_intra = jnp.exp(log_g[:, None, None] * jnp.abs(idx[:, None] - idx[None, :]))
    xi = jnp.exp(log_g[:, None] * (idx + 1.0)).T[None, :, :, None]
    zeta = jnp.exp(log_g[:, None] * (CHUNK - 1.0 - idx)).T[None, :, :, None]
    g_chunk = jnp.exp(log_g * CHUNK)[None, :, None, None]

    def step(r, inp):
        qc, kc, vc = inp
        s = jnp.einsum('bihd,bjhd->bhij', qc, kc) * d_intra
        o = jnp.einsum('bhij,bjhe->bihe', s, vc)
        o = o + jnp.einsum('bihd,bhde->bihe', qc * xi, r)
        r = r * g_chunk + jnp.einsum('bjhd,bjhe->bhde', kc * zeta, vc)
        return r, o

    r0 = jnp.zeros((b, RET_HEADS, RET_QK, RET_V), f32)
    _, o = lax.scan(step, r0, (to_chunks(q), to_chunks(k), to_chunks(v)))
    return from_chunks(o)


def retention_mixer(h, w_in, gn_g, w_out):
    f32 = jnp.float32
    dtype = h.dtype
    b, l, _ = h.shape
    proj = h @ w_in
    q, k, v, g = jnp.split(proj, [D_MODEL, 2 * D_MODEL, 2 * D_MODEL + MIX_WIDTH], axis=-1)
    q = q.astype(f32).reshape(b, l, RET_HEADS, RET_QK)
    k = k.astype(f32).reshape(b, l, RET_HEADS, RET_QK) * (RET_QK ** -0.5)
    v = v.astype(f32).reshape(b, l, RET_HEADS, RET_V)
    inv_freq = ROPE_BASE ** (-jnp.arange(0, RET_QK, 2, dtype=f32) / RET_QK)
    ang = jnp.arange(l, dtype=f32)[:, None] * inv_freq[None, :]
    cos, sin = jnp.cos(ang)[:, None, :], jnp.sin(ang)[:, None, :]
    q = rotary(q, cos, sin)
    k = rotary(k, cos, sin)
    y = retention_chunkwise(q, k, v)
    mu = jnp.mean(y, axis=-1, keepdims=True)
    yc = y - mu
    y = yc * lax.rsqrt(jnp.mean(yc * yc, axis=-1, keepdims=True) + EPS)
    y = y.reshape(b, l, MIX_WIDTH) * gn_g.astype(f32) * jax.nn.silu(g.astype(f32))
    return y.astype(dtype) @ w_out


def swiglu(h, w_in, w_out):
    gate, up = jnp.split(h @ w_in, 2, axis=-1)
    return (jax.nn.silu(gate) * up) @ w_out


def setup_inputs(seed: int = 0) -> dict:
    key = jax.random.key(seed)
    ks = iter(jax.random.split(key, 48))
    f32 = jnp.float32
    D = D_MODEL

    def nrm(shape, std):
        return std * jax.random.normal(next(ks), shape, f32)

    def gain(shape):
        return 1.0 + nrm(shape, 0.02)

    x = nrm((BATCH, SEQ, D), 1.0)
    c = nrm((BATCH, D), 1.0)
    norm_mix_g = gain((DEPTH, D))
    ada_mix_w = nrm((DEPTH, D, 3 * D), 0.5 * D ** -0.5)
    ada_mix_b = nrm((DEPTH, 3 * D), 0.01)
    norm_ffn_g = gain((DEPTH, D))
    ada_ffn_w = nrm((DEPTH, D, 3 * D), 0.5 * D ** -0.5)
    ada_ffn_b = nrm((DEPTH, 3 * D), 0.01)
    ffn_w_in = nrm((DEPTH, D, 2 * FFN_HIDDEN), D ** -0.5)
    ffn_w_out = nrm((DEPTH, FFN_HIDDEN, D), FFN_HIDDEN ** -0.5)
    ab_w_in = nrm((N_EVEN, D, IN0_WIDTH), D ** -0.5)
    ssd_conv_w = nrm((N_EVEN, SSD_CONV, SSD_XBC), SSD_CONV ** -0.5)
    ssd_conv_b = nrm((N_EVEN, SSD_XBC), 0.01)
    dt0 = jnp.exp(jax.random.uniform(next(ks), (N_EVEN, SSD_HEADS), f32, math.log(1e-3), math.log(1e-1)))
    ssd_dt_bias = dt0 + jnp.log(-jnp.expm1(-dt0))
    ssd_a_log = jnp.log(jax.random.uniform(next(ks), (N_EVEN, SSD_HEADS), f32, 1.0, 16.0))
    ssd_d = gain((N_EVEN, SSD_HEADS))
    ssd_norm_g = gain((N_EVEN, SSD_INNER))
    s5_a_re = -0.5 + nrm((N_EVEN, S5_GROUPS, S5_STATE), 1e-3)
    s5_a_im = jnp.broadcast_to(math.pi * jnp.arange(S5_STATE, dtype=f32), (N_EVEN, S5_GROUPS, S5_STATE))
    s5_log_dt = jax.random.uniform(next(ks), (N_EVEN, S5_GROUPS), f32, math.log(1e-3), math.log(1e-1))
    s5_b_re = nrm((N_EVEN, S5_GROUPS, S5_STATE, S5_GROUP_CH), (2 * S5_GROUP_CH) ** -0.5)
    s5_b_im = nrm((N_EVEN, S5_GROUPS, S5_STATE, S5_GROUP_CH), (2 * S5_GROUP_CH) ** -0.5)
    s5_c_re = nrm((N_EVEN, S5_GROUPS, S5_GROUP_CH, S5_STATE), (2 * S5_STATE) ** -0.5)
    s5_c_im = nrm((N_EVEN, S5_GROUPS, S5_GROUP_CH, S5_STATE), (2 * S5_STATE) ** -0.5)
    s5_d = nrm((N_EVEN, S5_WIDTH), 1.0)
    s5_glu_w = nrm((N_EVEN, S5_WIDTH, S5_WIDTH), S5_WIDTH ** -0.5)
    s5_glu_b = nrm((N_EVEN, S5_WIDTH), 0.01)
    ab_w_out = nrm((N_EVEN, MIX_WIDTH, D), MIX_WIDTH ** -0.5)
    ret_w_in = nrm((N_ODD, D, IN1_WIDTH), D ** -0.5)
    ret_gn_g = gain((N_ODD, MIX_WIDTH))
    ret_w_out = nrm((N_ODD, MIX_WIDTH, D), MIX_WIDTH ** -0.5)
    final_norm_g = gain((D,))
    return {
        "x": x, "c": c,
        "norm_mix_g": norm_mix_g, "ada_mix_w": ada_mix_w, "ada_mix_b": ada_mix_b,
        "norm_ffn_g": norm_ffn_g, "ada_ffn_w": ada_ffn_w, "ada_ffn_b": ada_ffn_b,
        "ffn_w_in": ffn_w_in, "ffn_w_out": ffn_w_out,
        "ab_w_in": ab_w_in, "ssd_conv_w": ssd_conv_w, "ssd_conv_b": ssd_conv_b,
        "ssd_dt_bias": ssd_dt_bias, "ssd_a_log": ssd_a_log, "ssd_d": ssd_d, "ssd_norm_g": ssd_norm_g,
        "s5_a_re": s5_a_re, "s5_a_im": s5_a_im, "s5_log_dt": s5_log_dt,
        "s5_b_re": s5_b_re, "s5_b_im": s5_b_im, "s5_c_re": s5_c_re, "s5_c_im": s5_c_im,
        "s5_d": s5_d, "s5_glu_w": s5_glu_w, "s5_glu_b": s5_glu_b, "ab_w_out": ab_w_out,
        "ret_w_in": ret_w_in, "ret_gn_g": ret_gn_g, "ret_w_out": ret_w_out,
        "final_norm_g": final_norm_g,
    }


def reference(x, c, norm_mix_g, ada_mix_w, ada_mix_b, norm_ffn_g, ada_ffn_w, ada_ffn_b,
              ffn_w_in, ffn_w_out, ab_w_in, ssd_conv_w, ssd_conv_b, ssd_dt_bias, ssd_a_log,
              ssd_d, ssd_norm_g, s5_a_re, s5_a_im, s5_log_dt, s5_b_re, s5_b_im, s5_c_re, s5_c_im,
              s5_d, s5_glu_w, s5_glu_b, ab_w_out, ret_w_in, ret_gn_g, ret_w_out, final_norm_g):
    for i in range(DEPTH):
        j = i // 2
        shift, scale, gate = ada_modulation(c, ada_mix_w[i], ada_mix_b[i])
        hn = rmsnorm(x, norm_mix_g[i]) * (1 + scale) + shift
        if i % 2 == 0:
            y = ssd_s5_mixer(hn, ab_w_in[j], ssd_conv_w[j], ssd_conv_b[j], ssd_dt_bias[j], ssd_a_log[j],
                             ssd_d[j], ssd_norm_g[j], s5_a_re[j], s5_a_im[j], s5_log_dt[j],
                             s5_b_re[j], s5_b_im[j], s5_c_re[j], s5_c_im[j], s5_d[j],
                             s5_glu_w[j], s5_glu_b[j], ab_w_out[j])
        else:
            y = retention_mixer(hn, ret_w_in[j], ret_gn_g[j], ret_w_out[j])
        x = x + gate * y
        shift, scale, gate = ada_modulation(c, ada_ffn_w[i], ada_ffn_b[i])
        hn = rmsnorm(x, norm_ffn_g[i]) * (1 + scale) + shift
        x = x + gate * swiglu(hn, ffn_w_in[i], ffn_w_out[i])
    return rmsnorm(x, final_norm_g)
```

```python
import functools
import math

import numpy as np
import jax
import jax.numpy as jnp
from jax import lax
from jax.experimental import pallas as pl
from jax.experimental.pallas import tpu as pltpu

F32 = jnp.float32
BF16 = jnp.bfloat16
EPS = 1e-6

D_MODEL = 2048
MIX_WIDTH = 2 * D_MODEL
SSD_HEADDIM = 64
SSD_INNER = 3 * MIX_WIDTH // 4
SSD_HEADS = SSD_INNER // SSD_HEADDIM
SSD_GROUPS = 8
SSD_HPG = SSD_HEADS // SSD_GROUPS
SSD_STATE = 128
SSD_CONV = 4
SSD_GW = SSD_HPG * SSD_HEADDIM
SSD_BC = SSD_GROUPS * SSD_STATE
SSD_XBC = SSD_INNER + 2 * SSD_BC
S5_WIDTH = MIX_WIDTH - SSD_INNER
S5_GROUP_CH = 16
S5_GROUPS = S5_WIDTH // S5_GROUP_CH
S5_STATE = 64
S5_CHUNK = 64
S5_ROW = S5_CHUNK * S5_GROUP_CH
RET_HEADS = 8
RET_QK = D_MODEL // RET_HEADS
RET_V = MIX_WIDTH // RET_HEADS
RET_CHUNK = 64
RET_BLOCK = 256
ROPE_BASE = 10000.0
FFN_HIDDEN = ((-(-8 * D_MODEL // 3) + 255) // 256) * 256

LANES = 128
SSD_LC = 128
VMEM_LIMIT = 56 * 1024 * 1024


def _cparams(sem):
    return pltpu.CompilerParams(dimension_semantics=sem, vmem_limit_bytes=VMEM_LIMIT)


def _dot(a, b):
    return jnp.dot(a, b, preferred_element_type=F32)


def _dot_nt(a, b):
    return lax.dot_general(a, b, (((1,), (1,)), ((), ())), preferred_element_type=F32)


def _dot_tn(a, b):
    return lax.dot_general(a, b, (((0,), (0,)), ((), ())), preferred_element_type=F32)


def _split3(v):
    hi = v.astype(BF16)
    r = v - hi.astype(F32)
    mid = r.astype(BF16)
    lo = (r - mid.astype(F32)).astype(BF16)
    return hi, mid, lo


def _sel_right(v, e):
    hi, mid, lo = _split3(v)
    return (_dot(lo, e) + _dot(mid, e)) + _dot(hi, e)


def _sel_left(e, v):
    hi, mid, lo = _split3(v)
    return (_dot(e, lo) + _dot(e, mid)) + _dot(e, hi)


def _silu(v):
    return v * jax.nn.sigmoid(v)


def _modulated_norm(x, g, scale, shift):
    ms = jnp.mean(x * x, axis=-1, keepdims=True)
    y = x * lax.rsqrt(ms + EPS) * g
    return y * (1.0 + scale) + shift


def _ada_kernel(c_ref, w_ref, b_ref, o_ref):
    sc = _silu(c_ref[...]).astype(BF16)
    o_ref[0] = _dot(sc, w_ref[0].astype(BF16)) + b_ref[0]


def _ada_modulation(c, w, b):
    depth, d, n = w.shape
    bsz = c.shape[0]
    tn = 1024
    return pl.pallas_call(
        _ada_kernel,
        grid=(depth, n // tn),
        in_specs=[pl.BlockSpec((bsz, d), lambda i, j: (0, 0)),
                  pl.BlockSpec((1, d, tn), lambda i, j: (i, 0, j)),
                  pl.BlockSpec((1, 1, tn), lambda i, j: (i, 0, j))],
        out_specs=pl.BlockSpec((1, bsz, tn), lambda i, j: (i, 0, j)),
        out_shape=jax.ShapeDtypeStruct((depth, bsz, n), F32),
        compiler_params=_cparams(("parallel", "parallel")),
        name="ada_modulation",
    )(c, w, b.reshape(depth, 1, n))


def _norm_mm_kernel(x_ref, g_ref, sc_ref, sh_ref, w_ref, o_ref, hn_ref):
    @pl.when(pl.program_id(2) == 0)
    def _():
        hn_ref[...] = _modulated_norm(x_ref[0], g_ref[...], sc_ref[0], sh_ref[0]).astype(BF16)

    o_ref[0] = _dot(hn_ref[...], w_ref[...]).astype(o_ref.dtype)


def _norm_mm_extra_kernel(x_ref, g_ref, sc_ref, sh_ref, w_ref, w2_ref, o_ref, o2_ref, hn_ref):
    @pl.when(pl.program_id(2) == 0)
    def _():
        hn = _modulated_norm(x_ref[0], g_ref[...], sc_ref[0], sh_ref[0]).astype(BF16)
        hn_ref[...] = hn
        o2_ref[0] = _dot(hn, w2_ref[...])

    o_ref[0] = _dot(hn_ref[...], w_ref[...]).astype(o_ref.dtype)


def _norm_matmul(x, g, scale, shift, w, w2=None, *, tm=1024, tn=512, name):
    bsz, l, d = x.shape
    n = w.shape[1]
    tm = min(tm, l)
    grid = (bsz, l // tm, n // tn)
    in_specs = [pl.BlockSpec((1, tm, d), lambda b, i, j: (b, i, 0)),
                pl.BlockSpec((1, d), lambda b, i, j: (0, 0)),
                pl.BlockSpec((1, 1, d), lambda b, i, j: (b, 0, 0)),
                pl.BlockSpec((1, 1, d), lambda b, i, j: (b, 0, 0)),
                pl.BlockSpec((d, tn), lambda b, i, j: (0, j))]
    out_specs = pl.BlockSpec((1, tm, tn), lambda b, i, j: (b, i, j))
    out_shape = jax.ShapeDtypeStruct((bsz, l, n), F32)
    args = [x, g.reshape(1, d), scale, shift, w]
    if w2 is None:
        kern = _norm_mm_kernel
    else:
        kern = _norm_mm_extra_kernel
        n2 = w2.shape[1]
        in_specs.append(pl.BlockSpec((d, n2), lambda b, i, j: (0, 0)))
        out_specs = [out_specs, pl.BlockSpec((1, tm, n2), lambda b, i, j: (b, i, 0))]
        out_shape = [out_shape, jax.ShapeDtypeStruct((bsz, l, n2), F32)]
        args.append(w2)
    return pl.pallas_call(
        kern, grid=grid, in_specs=in_specs, out_specs=out_specs, out_shape=out_shape,
        scratch_shapes=[pltpu.VMEM((tm, d), BF16)],
        compiler_params=_cparams(("parallel", "parallel", "arbitrary")),
        name=name,
    )(*args)


def _mm_res_kernel(*refs, bounds):
    na = len(bounds) - 1
    a_refs = refs[:na]
    w_ref, x_ref, gate_ref, o_ref, acc_ref = refs[na:]
    k = pl.program_id(2)

    @pl.when(k == 0)
    def _():
        acc_ref[...] = jnp.zeros_like(acc_ref)

    for i, a_ref in enumerate(a_refs):
        @pl.when((k >= bounds[i]) & (k < bounds[i + 1]))
        def _(a_ref=a_ref):
            acc_ref[...] += _dot(a_ref[0], w_ref[...])

    @pl.when(k == bounds[-1] - 1)
    def _():
        o_ref[0] = x_ref[0] + gate_ref[0] * acc_ref[...]


def _matmul_residual(a_list, w, x, gate, *, tm=512, tk=1024, name):
    bsz, l, d = x.shape
    kk, n = w.shape
    tm = min(tm, l)
    bounds = [0]
    for a in a_list:
        bounds.append(bounds[-1] + a.shape[2] // tk)
    assert bounds[-1] * tk == kk

    def a_spec(lo, hi):
        return pl.BlockSpec((1, tm, tk), lambda b, i, k: (b, i, jnp.clip(k - lo, 0, hi - lo - 1)))

    return pl.pallas_call(
        functools.partial(_mm_res_kernel, bounds=tuple(bounds)),
        grid=(bsz, l // tm, bounds[-1]),
        in_specs=[a_spec(bounds[i], bounds[i + 1]) for i in range(len(a_list))] + [
            pl.BlockSpec((tk, n), lambda b, i, k: (k, 0)),
            pl.BlockSpec((1, tm, d), lambda b, i, k: (b, i, 0)),
            pl.BlockSpec((1, 1, d), lambda b, i, k: (b, 0, 0))],
        out_specs=pl.BlockSpec((1, tm, n), lambda b, i, k: (b, i, 0)),
        out_shape=jax.ShapeDtypeStruct((bsz, l, n), F32),
        scratch_shapes=[pltpu.VMEM((tm, n), F32)],
        compiler_params=_cparams(("parallel", "parallel", "arbitrary")),
        name=name,
    )(*a_list, w, x, gate)


def _ffn_kernel(x_ref, g_ref, sc_ref, sh_ref, gate_ref, wg_ref, wu_ref, wo_ref, fg_ref,
                o_ref, hn_ref, acc_ref, *, nh, final_norm):
    h = pl.program_id(2)

    @pl.when(h == 0)
    def _():
        hn_ref[...] = _modulated_norm(x_ref[0], g_ref[...], sc_ref[0], sh_ref[0]).astype(BF16)
        acc_ref[...] = jnp.zeros_like(acc_ref)

    hn = hn_ref[...]
    gt = _dot(hn, wg_ref[...])
    up = _dot(hn, wu_ref[...])
    act = (_silu(gt) * up).astype(BF16)
    acc_ref[...] += _dot(act, wo_ref[...])

    @pl.when(h == nh - 1)
    def _():
        y = x_ref[0] + gate_ref[0] * acc_ref[...]
        if final_norm:
            ms = jnp.mean(y * y, axis=-1, keepdims=True)
            y = y * lax.rsqrt(ms + EPS) * fg_ref[...]
        o_ref[0] = y


def _ffn(x, g, scale, shift, gate, w_in, w_out, final_g, *, final_norm, tm=512, th=512, name):
    bsz, l, d = x.shape
    hid = w_out.shape[0]
    tm = min(tm, l)
    nh = hid // th
    vec = pl.BlockSpec((1, 1, d), lambda b, i, h: (b, 0, 0))
    return pl.pallas_call(
        functools.partial(_ffn_kernel, nh=nh, final_norm=final_norm),
        grid=(bsz, l // tm, nh),
        in_specs=[pl.BlockSpec((1, tm, d), lambda b, i, h: (b, i, 0)),
                  pl.BlockSpec((1, d), lambda b, i, h: (0, 0)),
                  vec, vec, vec,
                  pl.BlockSpec((d, th), lambda b, i, h: (0, h)),
                  pl.BlockSpec((d, th), lambda b, i, h: (0, h + nh)),
                  pl.BlockSpec((th, d), lambda b, i, h: (h, 0)),
                  pl.BlockSpec((1, d), lambda b, i, h: (0, 0))],
        out_specs=pl.BlockSpec((1, tm, d), lambda b, i, h: (b, i, 0)),
        out_shape=jax.ShapeDtypeStruct((bsz, l, d), F32),
        scratch_shapes=[pltpu.VMEM((tm, d), BF16), pltpu.VMEM((tm, d), F32)],
        compiler_params=_cparams(("parallel", "parallel", "arbitrary")),
        name=name,
    )(x, g.reshape(1, d), scale, shift, gate, w_in, w_in, w_out, final_g.reshape(1, d))


def _conv_silu(raw_ref, pad_ref, w_ref, b_ref, out_ref, first, lc, width, cb=512):
    @pl.when(first)
    def _():
        pad_ref[0:8, :] = jnp.zeros((8, width), F32)

    pad_ref[8:8 + lc, :] = raw_ref[0]
    for c0 in range(0, width, cb):
        cs = slice(c0, c0 + cb)
        acc = b_ref[:, cs] + w_ref[3:4, cs] * pad_ref[8:8 + lc, cs]
        acc = acc + w_ref[2:3, cs] * pad_ref[7:7 + lc, cs]
        acc = acc + w_ref[1:2, cs] * pad_ref[6:6 + lc, cs]
        acc = acc + w_ref[0:1, cs] * pad_ref[5:5 + lc, cs]
        out_ref[:, cs] = _silu(acc)
    pad_ref[0:8, :] = pad_ref[lc:lc + 8, :]


def _ssd_kernel(z_ref, xs_ref, bm_ref, cm_ref, dt_ref,
                cwx_ref, cwb_ref, cwc_ref, cbx_ref, cbb_ref, cbc_ref,
                dtb_ref, alog_ref, dsk_ref, ng_ref, e_ref, tri_ref,
                o_ref,
                padx_s, padb_s, padc_s, xs_s, bm_s, cm_s, rhs_s, st_s, *, lc):
    first = pl.program_id(1) == 0

    @pl.when(first)
    def _():
        st_s[...] = jnp.zeros_like(st_s)

    _conv_silu(xs_ref, padx_s, cwx_ref, cbx_ref, xs_s, first, lc, SSD_INNER)
    _conv_silu(bm_ref, padb_s, cwb_ref, cbb_ref, bm_s, first, lc, SSD_BC)
    _conv_silu(cm_ref, padc_s, cwc_ref, cbc_ref, cm_s, first, lc, SSD_BC)

    rhs_s[0:lc, :] = xs_s[...].astype(BF16)
    rhs_s[lc:lc + SSD_STATE, :] = st_s[...].astype(BF16)

    v = dt_ref[0] + dtb_ref[...]
    dt = jnp.maximum(v, 0.0) + jnp.log(1.0 + jnp.exp(-jnp.abs(v)))
    a = -jnp.exp(alog_ref[...])
    acum = _sel_left(tri_ref[...], dt * a)
    acum_last = acum[lc - 1:lc, :]
    wdec = jnp.exp(acum_last - acum) * dt
    eacum = jnp.exp(acum)
    acum_t = acum.T
    dt_t = dt.T
    wexp = _sel_right(wdec, e_ref[...])
    eal = _sel_right(jnp.broadcast_to(jnp.exp(acum_last), (8, LANES)), e_ref[...])[0:1, :]

    row = lax.broadcasted_iota(jnp.int32, (lc, lc), 0)
    col = lax.broadcasted_iota(jnp.int32, (lc, lc), 1)
    causal = row >= col
    lane = lax.broadcasted_iota(jnp.int32, (lc, LANES), 1)
    low_half = lane < SSD_HEADDIM

    for g in range(SSD_GROUPS):
        gs = slice(g * SSD_STATE, (g + 1) * SSD_STATE)
        bm_g = bm_s[:, gs]
        cm_g = cm_s[:, gs]
        bm_gb = bm_g.astype(BF16)
        scores = _dot_nt(cm_g.astype(BF16), bm_gb)
        pieces = []
        for j in range(SSD_HPG // 2):
            ps = slice(g * SSD_GW + j * LANES, g * SSD_GW + (j + 1) * LANES)
            rhs = rhs_s[:, ps]
            ys = []
            for hh in range(2):
                h = g * SSD_HPG + 2 * j + hh
                seg = acum[:, h:h + 1] - acum_t[h:h + 1, :]
                lmat = jnp.exp(jnp.where(causal, seg, -1e30))
                amat = scores * lmat * dt_t[h:h + 1, :]
                cs = cm_g * eacum[:, h:h + 1]
                lhs = jnp.concatenate([amat, cs], axis=1).astype(BF16)
                ys.append(_dot(lhs, rhs))
            pieces.append(jnp.where(low_half, ys[0], ys[1]))
        y = jnp.concatenate(pieces, axis=1)

        ws = slice(g * SSD_GW, (g + 1) * SSD_GW)
        xs_g = xs_s[:, ws]
        xw = (xs_g * wexp[:, ws]).astype(BF16)
        st_s[:, ws] = st_s[:, ws] * eal[:, ws] + _dot_tn(bm_gb, xw)

        y = y + dsk_ref[:, ws] * xs_g
        y = y * _silu(z_ref[0, :, ws])
        ms = jnp.mean(y * y, axis=-1, keepdims=True)
        o_ref[0, :, ws] = (y * lax.rsqrt(ms + EPS) * ng_ref[:, ws]).astype(o_ref.dtype)


def _ssd(proj, dtp, conv_w, conv_b, dt_bias, a_log, d_skip, norm_g):
    bsz, l, _ = proj.shape
    lc = min(SSD_LC, l)
    i0 = SSD_INNER
    cwx, cwb, cwc = conv_w[:, :i0], conv_w[:, i0:i0 + SSD_BC], conv_w[:, i0 + SSD_BC:]
    cb = conv_b.reshape(1, -1)
    cbx, cbb, cbc = cb[:, :i0], cb[:, i0:i0 + SSD_BC], cb[:, i0 + SSD_BC:]
    pad = LANES - SSD_HEADS
    dtb = jnp.pad(dt_bias, (0, pad)).reshape(1, LANES)
    alog = jnp.pad(a_log, (0, pad)).reshape(1, LANES)
    dsk = jnp.repeat(d_skip, SSD_HEADDIM).reshape(1, i0)
    head_of_lane = np.arange(i0) // SSD_HEADDIM
    expand = jnp.asarray(np.arange(LANES)[:, None] == head_of_lane[None, :], BF16)
    tri = jnp.asarray(np.tril(np.ones((lc, lc))), BF16)

    def full(shape):
        return pl.BlockSpec(shape, lambda b, k: (0,) * len(shape))

    nb3 = i0 // SSD_BC
    return pl.pallas_call(
        functools.partial(_ssd_kernel, lc=lc),
        grid=(bsz, l // lc),
        in_specs=[pl.BlockSpec((1, lc, i0), lambda b, k: (b, k, 0)),
                  pl.BlockSpec((1, lc, i0), lambda b, k: (b, k, 1)),
                  pl.BlockSpec((1, lc, SSD_BC), lambda b, k: (b, k, 2 * nb3)),
                  pl.BlockSpec((1, lc, SSD_BC), lambda b, k: (b, k, 2 * nb3 + 1)),
                  pl.BlockSpec((1, lc, LANES), lambda b, k: (b, k, 0)),
                  full((SSD_CONV, i0)), full((SSD_CONV, SSD_BC)), full((SSD_CONV, SSD_BC)),
                  full((1, i0)), full((1, SSD_BC)), full((1, SSD_BC)),
                  full((1, LANES)), full((1, LANES)), full((1, i0)), full((1, i0)),
                  full((LANES, i0)), full((lc, lc))],
        out_specs=pl.BlockSpec((1, lc, i0), lambda b, k: (b, k, 0)),
        out_shape=jax.ShapeDtypeStruct((bsz, l, i0), BF16),
        scratch_shapes=[pltpu.VMEM((lc + 8, i0), F32), pltpu.VMEM((lc + 8, SSD_BC), F32),
                        pltpu.VMEM((lc + 8, SSD_BC), F32),
                        pltpu.VMEM((lc, i0), F32), pltpu.VMEM((lc, SSD_BC), F32),
                        pltpu.VMEM((lc, SSD_BC), F32),
                        pltpu.VMEM((lc + SSD_STATE, i0), BF16), pltpu.VMEM((SSD_STATE, i0), F32)],
        compiler_params=_cparams(("parallel", "arbitrary")),
        name="ssd_scan",
    )(proj, proj, proj, proj, dtp, cwx, cwb, cwc, cbx, cbb, cbc, dtb, alog, dsk,
      norm_g.reshape(1, i0), expand, tri)


def _cplx_mul(ar, ai, br, bi):
    return ar * br - ai * bi, ar * bi + ai * br


def _zoh_coef(are, aim, delta):
    er = jnp.exp(are * delta)
    br = er * jnp.cos(aim * delta) - 1.0
    bi = er * jnp.sin(aim * delta)
    den = are * are + aim * aim
    return (br * are + bi * aim) / den, (bi * are - br * aim) / den


def _s5_kernel(u_ref, arc_ref, aic_ref, arr_ref, air_ref, ldt_ref,
               bT_re_ref, bT_im_ref, bt_re_ref, bt_im_ref, ct_re_ref, ct_im_ref, d_ref,
               rep_ref, rev_ref, o_ref,
               m_s, s_re_s, s_im_s, xp_re_s, xp_im_s, *, nb, nchunks):
    hi = lax.Precision.HIGHEST
    delta = jnp.exp(ldt_ref[0])
    arc, aic = arc_ref[0], aic_ref[0]
    arr, air = arr_ref[0], air_ref[0]

    tau = lax.broadcasted_iota(jnp.int32, (1, S5_CHUNK), 1).astype(F32)
    mag = jnp.exp((arc * delta) * tau)
    ang = (aic * delta) * tau
    p_re, p_im = mag * jnp.cos(ang), mag * jnp.sin(ang)
    lb_re = jnp.exp(arc * delta) * jnp.cos(aic * delta)
    lb_im = jnp.exp(arc * delta) * jnp.sin(aic * delta)
    cf_re_c, cf_im_c = _zoh_coef(arc, aic, delta)
    cf_re_r, cf_im_r = _zoh_coef(arr, air, delta)

    rep, rev = rep_ref[...], rev_ref[...]
    pe_re, pe_im = _sel_right(p_re, rep), _sel_right(p_im, rep)
    qe_re, qe_im = _sel_right(p_re, rev), _sel_right(p_im, rev)

    e_re, e_im = _cplx_mul(ct_re_ref[0], ct_im_ref[0], pe_re, pe_im)
    bb_re, bb_im = _cplx_mul(cf_re_r, cf_im_r, bT_re_ref[0], bT_im_ref[0])
    r0 = (jnp.dot(bb_re, e_re, precision=hi, preferred_element_type=F32)
          - jnp.dot(bb_im, e_im, precision=hi, preferred_element_type=F32))

    lane = lax.broadcasted_iota(jnp.int32, (S5_GROUP_CH, S5_ROW), 1)
    per = LANES // S5_GROUP_CH
    for j in range(per):
        if j == 0:
            rj = r0
        else:
            rj = jnp.where(lane >= S5_GROUP_CH * j, pltpu.roll(r0, S5_GROUP_CH * j, 1), 0.0)
        rjb = rj.astype(BF16)
        for a in range(S5_CHUNK // per):
            s = per * a + j
            rows = slice(S5_GROUP_CH * s, S5_GROUP_CH * (s + 1))
            if a > 0:
                m_s[rows, 0:LANES * a] = jnp.zeros((S5_GROUP_CH, LANES * a), BF16)
            m_s[rows, LANES * a:S5_ROW] = rjb[:, 0:S5_ROW - LANES * a]

    bbt_re, bbt_im = _cplx_mul(cf_re_c, cf_im_c, bt_re_ref[0], bt_im_ref[0])
    wt_re, wt_im = _cplx_mul(qe_re, qe_im, bbt_re, bbt_im)
    u = u_ref[0]
    ub = u.astype(BF16)
    s_re_s[...] = _dot_nt(ub, wt_re.astype(BF16))
    s_im_s[...] = _dot_nt(ub, wt_im.astype(BF16))

    cpow = float(S5_CHUNK)
    c_re = jnp.exp(arr * delta * cpow) * jnp.cos(air * delta * cpow)
    c_im = jnp.exp(arr * delta * cpow) * jnp.sin(air * delta * cpow)
    xr = jnp.zeros((nb, S5_STATE), F32)
    xi = jnp.zeros((nb, S5_STATE), F32)
    for k in range(nchunks):
        rows = slice(k * nb, (k + 1) * nb)
        xp_re_s[rows, :] = xr
        xp_im_s[rows, :] = xi
        xr, xi = (xr * c_re - xi * c_im + s_re_s[rows, :],
                  xr * c_im + xi * c_re + s_im_s[rows, :])

    p1_re, p1_im = _cplx_mul(pe_re, pe_im, lb_re, lb_im)
    e1_re, e1_im = _cplx_mul(ct_re_ref[0], ct_im_ref[0], p1_re, p1_im)
    y = _dot(ub, m_s[...])
    y = y + _dot(xp_re_s[...].astype(BF16), e1_re.astype(BF16))
    y = y - _dot(xp_im_s[...].astype(BF16), e1_im.astype(BF16))
    y = y + d_ref[0] * u
    o_ref[0] = 0.5 * y * (1.0 + jnp.tanh(math.sqrt(2.0 / math.pi) * (y + 0.044715 * (y * y * y))))


def _s5(u, a_re, a_im, log_dt, b_re, b_im, c_re, c_im, d):
    bsz, l, _ = u.shape
    g, p, ch = S5_GROUPS, S5_STATE, S5_GROUP_CH
    nchunks = l // S5_CHUNK
    nrows = nchunks * bsz
    ug = u.reshape(bsz, nchunks, S5_CHUNK, g, ch).transpose(3, 1, 0, 2, 4).reshape(g, nrows, S5_ROW)
    tile = lambda t: jnp.tile(t, (1, 1, S5_CHUNK))
    t_of_lane = np.arange(S5_ROW) // ch
    rep = jnp.asarray(np.arange(S5_CHUNK)[:, None] == t_of_lane[None, :], BF16)
    rev = jnp.asarray(np.arange(S5_CHUNK)[:, None] == (S5_CHUNK - 1 - t_of_lane)[None, :], BF16)

    def per_group(shape):
        return pl.BlockSpec((1,) + shape, lambda i: (i, 0, 0))

    yg = pl.pallas_call(
        functools.partial(_s5_kernel, nb=bsz, nchunks=nchunks),
        grid=(g,),
        in_specs=[per_group((nrows, S5_ROW)),
                  per_group((p, 1)), per_group((p, 1)), per_group((1, p)), per_group((1, p)),
                  per_group((1, 1)),
                  per_group((ch, p)), per_group((ch, p)),
                  per_group((p, S5_ROW)), per_group((p, S5_ROW)),
                  per_group((p, S5_ROW)), per_group((p, S5_ROW)),
                  per_group((1, S5_ROW)),
                  pl.BlockSpec((S5_CHUNK, S5_ROW), lambda i: (0, 0)),
                  pl.BlockSpec((S5_CHUNK, S5_ROW), lambda i: (0, 0))],
        out_specs=per_group((nrows, S5_ROW)),
        out_shape=jax.ShapeDtypeStruct((g, nrows, S5_ROW), F32),
        scratch_shapes=[pltpu.VMEM((S5_ROW, S5_ROW), BF16),
                        pltpu.VMEM((nrows, p), F32), pltpu.VMEM((nrows, p), F32),
                        pltpu.VMEM((nrows, p), F32), pltpu.VMEM((nrows, p), F32)],
        compiler_params=_cparams(("parallel",)),
        name="s5_ssm",
    )(ug,
      a_re.reshape(g, p, 1), a_im.reshape(g, p, 1), a_re.reshape(g, 1, p), a_im.reshape(g, 1, p),
      log_dt.reshape(g, 1, 1),
      b_re.transpose(0, 2, 1), b_im.transpose(0, 2, 1),
      tile(b_re), tile(b_im),
      tile(c_re.transpose(0, 2, 1)), tile(c_im.transpose(0, 2, 1)),
      jnp.tile(d.reshape(g, 1, ch), (1, 1, S5_CHUNK)),
      rep, rev)
    return yg.reshape(g, nchunks, bsz, S5_CHUNK, ch).transpose(2, 1, 3, 0, 4).reshape(bsz, l, g * ch)


def _glu_kernel(y_ref, w_ref, b_ref, o_ref):
    y = y_ref[0]
    gate = jax.nn.sigmoid(_dot(y.astype(BF16), w_ref[...]) + b_ref[...])
    o_ref[0] = (y * gate).astype(o_ref.dtype)


def _glu(y, w, b, *, tm=1024):
    bsz, l, n = y.shape
    tm = min(tm, l)
    return pl.pallas_call(
        _glu_kernel,
        grid=(bsz, l // tm),
        in_specs=[pl.BlockSpec((1, tm, n), lambda b_, i: (b_, i, 0)),
                  pl.BlockSpec((n, n), lambda b_, i: (0, 0)),
                  pl.BlockSpec((1, n), lambda b_, i: (0, 0))],
        out_specs=pl.BlockSpec((1, tm, n), lambda b_, i: (b_, i, 0)),
        out_shape=jax.ShapeDtypeStruct((bsz, l, n), BF16),
        compiler_params=_cparams(("parallel", "parallel")),
        name="s5_glu",
    )(y, w, b.reshape(1, n))


def _ret_kernel(q_ref, k_ref, v_ref, g_ref, lg_ref, invf_ref, gn_ref, o_ref,
                r_s, cos_s, sin_s, *, tb):
    j = pl.program_id(1)
    h = pl.program_id(2)

    @pl.when(h == 0)
    def _():
        pos = (j * tb + lax.broadcasted_iota(jnp.int32, (tb, 1), 0)).astype(F32)
        ang = pos * invf_ref[...]
        cos_s[...] = jnp.cos(ang)
        sin_s[...] = jnp.sin(ang)

    @pl.when(j == 0)
    def _():
        r_s[h] = jnp.zeros((RET_QK, RET_V), F32)

    cos, sin = cos_s[...], sin_s[...]
    half = RET_QK // 2

    def rot(t):
        t1, t2 = t[:, :half], t[:, half:]
        return jnp.concatenate([t1 * cos - t2 * sin, t1 * sin + t2 * cos], axis=1)

    lg = lg_ref[0]
    q = rot(q_ref[0])
    k = rot(k_ref[0] * (RET_QK ** -0.5))
    vb = v_ref[0].astype(BF16)

    ri = lax.broadcasted_iota(jnp.int32, (tb, tb), 0)
    ci = lax.broadcasted_iota(jnp.int32, (tb, tb), 1)
    rc, cc = ri // RET_CHUNK, ci // RET_CHUNK
    dist = jnp.where(rc == cc, jnp.abs(ri - ci), ri - ci).astype(F32)
    dmat = jnp.where(cc > rc, 0.0, jnp.exp(lg * jnp.maximum(dist, 0.0)))
    idx = lax.broadcasted_iota(jnp.int32, (tb, 1), 0).astype(F32)
    xi = jnp.exp(lg * (idx + 1.0))
    zeta = jnp.exp(lg * (tb - 1.0 - idx))
    gblk = jnp.exp(lg * float(tb))

    r = r_s[h]
    s = _dot_nt(q.astype(BF16), k.astype(BF16)) * dmat
    o = _dot(s.astype(BF16), vb) + _dot((q * xi).astype(BF16), r.astype(BF16))
    r_s[h] = r * gblk + _dot_tn((k * zeta).astype(BF16), vb)

    mu = jnp.mean(o, axis=-1, keepdims=True)
    oc = o - mu
    y = oc * lax.rsqrt(jnp.mean(oc * oc, axis=-1, keepdims=True) + EPS)
    o_ref[0] = (y * gn_ref[...] * _silu(g_ref[0])).astype(o_ref.dtype)


def _retention(proj, gn_g):
    bsz, l, _ = proj.shape
    tb = min(RET_BLOCK, l)
    hq = RET_HEADS
    log_g = jnp.log(1.0 - 2.0 ** (-5.0 - jnp.arange(hq, dtype=F32))).reshape(hq, 1, 1)
    inv_freq = (ROPE_BASE ** (-jnp.arange(0, RET_QK, 2, dtype=F32) / RET_QK)).reshape(1, RET_QK // 2)
    return pl.pallas_call(
        functools.partial(_ret_kernel, tb=tb),
        grid=(bsz, l // tb, hq),
        in_specs=[pl.BlockSpec((1, tb, RET_QK), lambda b, j, h: (b, j, h)),
                  pl.BlockSpec((1, tb, RET_QK), lambda b, j, h: (b, j, hq + h)),
                  pl.BlockSpec((1, tb, RET_V), lambda b, j, h: (b, j, hq + h)),
                  pl.BlockSpec((1, tb, RET_V), lambda b, j, h: (b, j, 2 * hq + h)),
                  pl.BlockSpec((1, 1, 1), lambda b, j, h: (h, 0, 0)),
                  pl.BlockSpec((1, RET_QK // 2), lambda b, j, h: (0, 0)),
                  pl.BlockSpec((1, RET_V), lambda b, j, h: (0, h))],
        out_specs=pl.BlockSpec((1, tb, RET_V), lambda b, j, h: (b, j, h)),
        out_shape=jax.ShapeDtypeStruct((bsz, l, hq * RET_V), BF16),
        scratch_shapes=[pltpu.VMEM((hq, RET_QK, RET_V), F32),
                        pltpu.VMEM((tb, RET_QK // 2), F32), pltpu.VMEM((tb, RET_QK // 2), F32)],
        compiler_params=_cparams(("parallel", "arbitrary", "arbitrary")),
        name="retention",
    )(proj, proj, proj, proj, log_g, inv_freq, gn_g.reshape(1, -1))


def _split_mod(m, bsz):
    d = m.shape[-1] // 3
    return tuple(m[:, k * d:(k + 1) * d].reshape(bsz, 1, d) for k in range(3))


def kernel(x, c, norm_mix_g, ada_mix_w, ada_mix_b, norm_ffn_g, ada_ffn_w, ada_ffn_b, ffn_w_in, ffn_w_out, ab_w_in, ssd_conv_w, ssd_conv_b, ssd_dt_bias, ssd_a_log, ssd_d, ssd_norm_g, s5_a_re, s5_a_im, s5_log_dt, s5_b_re, s5_b_im, s5_c_re, s5_c_im, s5_d, s5_glu_w, s5_glu_b, ab_w_out, ret_w_in, ret_gn_g, ret_w_out, final_norm_g):
    bsz = x.shape[0]
    mod_mix = _ada_modulation(c, ada_mix_w, ada_mix_b)
    mod_ffn = _ada_modulation(c, ada_ffn_w, ada_ffn_b)

    shift, scale, gate = _split_mod(mod_mix[0], bsz)
    w0 = ab_w_in[0]
    n_main = SSD_INNER + SSD_XBC
    w_main = jnp.concatenate([w0[:, :n_main], w0[:, n_main + SSD_HEADS:]], axis=1).astype(BF16)
    w_dt = jnp.pad(w0[:, n_main:n_main + SSD_HEADS], ((0, 0), (0, LANES - SSD_HEADS))).astype(BF16)
    proj0, dtp = _norm_matmul(x, norm_mix_g[0], scale, shift, w_main, w_dt, name="in_proj0")
    y_a = _ssd(proj0, dtp, ssd_conv_w[0], ssd_conv_b[0], ssd_dt_bias[0], ssd_a_log[0], ssd_d[0],
               ssd_norm_g[0])
    u = proj0[:, :, n_main:]
    y_s5 = _s5(u, s5_a_re[0], s5_a_im[0], s5_log_dt[0], s5_b_re[0], s5_b_im[0],
               s5_c_re[0], s5_c_im[0], s5_d[0])
    y_b = _glu(y_s5, s5_glu_w[0].astype(BF16), s5_glu_b[0])
    x = _matmul_residual([y_a, y_b], ab_w_out[0].astype(BF16), x, gate, name="out_proj0")

    shift, scale, gate = _split_mod(mod_ffn[0], bsz)
    x = _ffn(x, norm_ffn_g[0], scale, shift, gate, ffn_w_in[0].astype(BF16),
             ffn_w_out[0].astype(BF16), final_norm_g, final_norm=False, name="ffn0")

    shift, scale, gate = _split_mod(mod_mix[1], bsz)
    proj1 = _norm_matmul(x, norm_mix_g[1], scale, shift, ret_w_in[0].astype(BF16), name="in_proj1")
    y_r = _retention(proj1, ret_gn_g[0])
    x = _matmul_residual([y_r], ret_w_out[0].astype(BF16), x, gate, name="out_proj1")

    shift, scale, gate = _split_mod(mod_ffn[1], bsz)
    return _ffn(x, norm_ffn_g[1], scale, shift, gate, ffn_w_in[1].astype(BF16),
                ffn_w_out[1].astype(BF16), final_norm_g, final_norm=True, name="ffn1")
```

```python
import functools
import math

import numpy as np
import jax
import jax.numpy as jnp
from jax import lax
from jax.experimental import pallas as pl
from jax.experimental.pallas import tpu as pltpu

F32 = jnp.float32
BF16 = jnp.bfloat16
EPS = 1e-6

D_MODEL = 2048
MIX_WIDTH = 2 * D_MODEL
SSD_HEADDIM = 64
SSD_INNER = 3 * MIX_WIDTH // 4
SSD_HEADS = SSD_INNER // SSD_HEADDIM
SSD_GROUPS = 8
SSD_HPG = SSD_HEADS // SSD_GROUPS
SSD_STATE = 128
SSD_CONV = 4
SSD_GW = SSD_HPG * SSD_HEADDIM
SSD_BC = SSD_GROUPS * SSD_STATE
SSD_XBC = SSD_INNER + 2 * SSD_BC
S5_WIDTH = MIX_WIDTH - SSD_INNER
S5_GROUP_CH = 16
S5_GROUPS = S5_WIDTH // S5_GROUP_CH
S5_STATE = 64
S5_CHUNK = 128
RET_HEADS = 8
RET_QK = D_MODEL // RET_HEADS
RET_V = MIX_WIDTH // RET_HEADS
RET_CHUNK = 64
RET_BLOCK = 256
ROPE_BASE = 10000.0
FFN_HIDDEN = ((-(-8 * D_MODEL // 3) + 255) // 256) * 256

LANES = 128
SSD_LC = 128
VMEM_LIMIT = 56 * 1024 * 1024


def _cparams(sem):
    return pltpu.CompilerParams(dimension_semantics=sem, vmem_limit_bytes=VMEM_LIMIT)


def _dot(a, b):
    return jnp.dot(a, b, preferred_element_type=F32)


def _dot_nt(a, b):
    return lax.dot_general(a, b, (((1,), (1,)), ((), ())), preferred_element_type=F32)


def _dot_tn(a, b):
    return lax.dot_general(a, b, (((0,), (0,)), ((), ())), preferred_element_type=F32)


def _split3(v):
    hi = v.astype(BF16)
    r = v - hi.astype(F32)
    mid = r.astype(BF16)
    lo = (r - mid.astype(F32)).astype(BF16)
    return hi, mid, lo


def _sel_right(v, e):
    hi, mid, lo = _split3(v)
    return (_dot(lo, e) + _dot(mid, e)) + _dot(hi, e)


def _sel_left(e, v):
    hi, mid, lo = _split3(v)
    return (_dot(e, lo) + _dot(e, mid)) + _dot(e, hi)


def _silu(v):
    h = 0.5 * v
    return h + h * jnp.tanh(h)


def _modulated_norm(x, g, scale, shift):
    ms = jnp.mean(x * x, axis=-1, keepdims=True)
    y = x * lax.rsqrt(ms + EPS) * g
    return y * (1.0 + scale) + shift


def _ada_kernel(c_ref, w_ref, b_ref, o_ref):
    sc = _silu(c_ref[...]).astype(BF16)
    o_ref[0] = _dot(sc, w_ref[0].astype(BF16)) + b_ref[0]


def _ada_modulation(c, w, b):
    depth, d, n = w.shape
    bsz = c.shape[0]
    tn = 1024
    return pl.pallas_call(
        _ada_kernel,
        grid=(depth, n // tn),
        in_specs=[pl.BlockSpec((bsz, d), lambda i, j: (0, 0)),
                  pl.BlockSpec((1, d, tn), lambda i, j: (i, 0, j)),
                  pl.BlockSpec((1, 1, tn), lambda i, j: (i, 0, j))],
        out_specs=pl.BlockSpec((1, bsz, tn), lambda i, j: (i, 0, j)),
        out_shape=jax.ShapeDtypeStruct((depth, bsz, n), F32),
        compiler_params=_cparams(("parallel", "parallel")),
        name="ada_modulation",
    )(c, w, b.reshape(depth, 1, n))


def _norm_mm_kernel(x_ref, g_ref, sc_ref, sh_ref, w_ref, o_ref, hn_ref):
    @pl.when(pl.program_id(2) == 0)
    def _():
        hn_ref[...] = _modulated_norm(x_ref[0], g_ref[...], sc_ref[0], sh_ref[0]).astype(BF16)

    o_ref[0] = _dot(hn_ref[...], w_ref[...]).astype(o_ref.dtype)


def _in_proj0_kernel(x_ref, g_ref, sc_ref, sh_ref, w_ref, wdt_ref, wut_ref, o_ref, dt_ref, ut_ref, hn_ref,
                     *, n_main):
    n = pl.program_id(2)

    @pl.when(n == 0)
    def _():
        hn = _modulated_norm(x_ref[0], g_ref[...], sc_ref[0], sh_ref[0]).astype(BF16)
        hn_ref[...] = hn
        dt_ref[0] = _dot(hn, wdt_ref[...])

    @pl.when(n < n_main)
    def _():
        o_ref[0] = _dot(hn_ref[...], w_ref[...])

    @pl.when(n >= n_main)
    def _():
        ut_ref[...] = _dot_nt(wut_ref[...], hn_ref[...])


def _norm_matmul(x, g, scale, shift, w, *, tm=1024, tn=512, name):
    bsz, l, d = x.shape
    n = w.shape[1]
    tm = min(tm, l)
    return pl.pallas_call(
        _norm_mm_kernel,
        grid=(bsz, l // tm, n // tn),
        in_specs=[pl.BlockSpec((1, tm, d), lambda b, i, j: (b, i, 0)),
                  pl.BlockSpec((1, d), lambda b, i, j: (0, 0)),
                  pl.BlockSpec((1, 1, d), lambda b, i, j: (b, 0, 0)),
                  pl.BlockSpec((1, 1, d), lambda b, i, j: (b, 0, 0)),
                  pl.BlockSpec((d, tn), lambda b, i, j: (0, j))],
        out_specs=pl.BlockSpec((1, tm, tn), lambda b, i, j: (b, i, j)),
        out_shape=jax.ShapeDtypeStruct((bsz, l, n), F32),
        scratch_shapes=[pltpu.VMEM((tm, d), BF16)],
        compiler_params=_cparams(("parallel", "parallel", "arbitrary")),
        name=name,
    )(x, g.reshape(1, d), scale, shift, w)


def _in_proj0(x, g, scale, shift, w, w_dt, w_ut, *, tm=1024, tn=512, tc=512):
    bsz, l, d = x.shape
    n, ndt, nut = w.shape[1], w_dt.shape[1], w_ut.shape[0]
    tm = min(tm, l)
    n_main, n_ut, nt = n // tn, nut // tc, l // tm
    return pl.pallas_call(
        functools.partial(_in_proj0_kernel, n_main=n_main),
        grid=(bsz, nt, n_main + n_ut),
        in_specs=[pl.BlockSpec((1, tm, d), lambda b, i, j: (b, i, 0)),
                  pl.BlockSpec((1, d), lambda b, i, j: (0, 0)),
                  pl.BlockSpec((1, 1, d), lambda b, i, j: (b, 0, 0)),
                  pl.BlockSpec((1, 1, d), lambda b, i, j: (b, 0, 0)),
                  pl.BlockSpec((d, tn), lambda b, i, j: (0, jnp.minimum(j, n_main - 1))),
                  pl.BlockSpec((d, ndt), lambda b, i, j: (0, 0)),
                  pl.BlockSpec((tc, d), lambda b, i, j: (jnp.clip(j - n_main, 0, n_ut - 1), 0))],
        out_specs=[pl.BlockSpec((1, tm, tn), lambda b, i, j: (b, i, jnp.minimum(j, n_main - 1))),
                   pl.BlockSpec((1, tm, ndt), lambda b, i, j: (b, i, 0)),
                   pl.BlockSpec((tc, tm), lambda b, i, j: (jnp.clip(j - n_main, 0, n_ut - 1), b * nt + i))],
        out_shape=[jax.ShapeDtypeStruct((bsz, l, n), F32),
                   jax.ShapeDtypeStruct((bsz, l, ndt), F32),
                   jax.ShapeDtypeStruct((nut, bsz * l), F32)],
        scratch_shapes=[pltpu.VMEM((tm, d), BF16)],
        compiler_params=_cparams(("parallel", "parallel", "arbitrary")),
        name="in_proj0",
    )(x, g.reshape(1, d), scale, shift, w, w_dt, w_ut)


def _mm_res_kernel(*refs, bounds):
    na = len(bounds) - 1
    a_refs = refs[:na]
    w_ref, x_ref, gate_ref, o_ref, acc_ref = refs[na:]
    k = pl.program_id(2)

    @pl.when(k == 0)
    def _():
        acc_ref[...] = jnp.zeros_like(acc_ref)

    for i, a_ref in enumerate(a_refs):
        @pl.when((k >= bounds[i]) & (k < bounds[i + 1]))
        def _(a_ref=a_ref):
            acc_ref[...] += _dot(a_ref[0], w_ref[...])

    @pl.when(k == bounds[-1] - 1)
    def _():
        o_ref[0] = x_ref[0] + gate_ref[0] * acc_ref[...]


def _matmul_residual(a_list, w, x, gate, *, tm=512, tk=1024, name):
    bsz, l, d = x.shape
    kk, n = w.shape
    tm = min(tm, l)
    bounds = [0]
    for a in a_list:
        bounds.append(bounds[-1] + a.shape[2] // tk)
    assert bounds[-1] * tk == kk

    def a_spec(lo, hi):
        return pl.BlockSpec((1, tm, tk), lambda b, i, k: (b, i, jnp.clip(k - lo, 0, hi - lo - 1)))

    return pl.pallas_call(
        functools.partial(_mm_res_kernel, bounds=tuple(bounds)),
        grid=(bsz, l // tm, bounds[-1]),
        in_specs=[a_spec(bounds[i], bounds[i + 1]) for i in range(len(a_list))] + [
            pl.BlockSpec((tk, n), lambda b, i, k: (k, 0)),
            pl.BlockSpec((1, tm, d), lambda b, i, k: (b, i, 0)),
            pl.BlockSpec((1, 1, d), lambda b, i, k: (b, 0, 0))],
        out_specs=pl.BlockSpec((1, tm, n), lambda b, i, k: (b, i, 0)),
        out_shape=jax.ShapeDtypeStruct((bsz, l, n), F32),
        scratch_shapes=[pltpu.VMEM((tm, n), F32)],
        compiler_params=_cparams(("parallel", "parallel", "arbitrary")),
        name=name,
    )(*a_list, w, x, gate)


def _ffn_kernel(x_ref, g_ref, sc_ref, sh_ref, gate_ref, wg_ref, wu_ref, wo_ref, fg_ref,
                o_ref, hn_ref, acc_ref, *, nh, final_norm):
    h = pl.program_id(2)

    @pl.when(h == 0)
    def _():
        hn_ref[...] = _modulated_norm(x_ref[0], g_ref[...], sc_ref[0], sh_ref[0]).astype(BF16)
        acc_ref[...] = jnp.zeros_like(acc_ref)

    hn = hn_ref[...]
    gt = _dot(hn, wg_ref[...])
    up = _dot(hn, wu_ref[...])
    act = (_silu(gt) * up).astype(BF16)
    acc_ref[...] += _dot(act, wo_ref[...])

    @pl.when(h == nh - 1)
    def _():
        y = x_ref[0] + gate_ref[0] * acc_ref[...]
        if final_norm:
            ms = jnp.mean(y * y, axis=-1, keepdims=True)
            y = y * lax.rsqrt(ms + EPS) * fg_ref[...]
        o_ref[0] = y


def _ffn(x, g, scale, shift, gate, w_in, w_out, final_g, *, final_norm, tm=512, th=512, name):
    bsz, l, d = x.shape
    hid = w_out.shape[0]
    tm = min(tm, l)
    nh = hid // th
    vec = pl.BlockSpec((1, 1, d), lambda b, i, h: (b, 0, 0))
    return pl.pallas_call(
        functools.partial(_ffn_kernel, nh=nh, final_norm=final_norm),
        grid=(bsz, l // tm, nh),
        in_specs=[pl.BlockSpec((1, tm, d), lambda b, i, h: (b, i, 0)),
                  pl.BlockSpec((1, d), lambda b, i, h: (0, 0)),
                  vec, vec, vec,
                  pl.BlockSpec((d, th), lambda b, i, h: (0, h)),
                  pl.BlockSpec((d, th), lambda b, i, h: (0, h + nh)),
                  pl.BlockSpec((th, d), lambda b, i, h: (h, 0)),
                  pl.BlockSpec((1, d), lambda b, i, h: (0, 0))],
        out_specs=pl.BlockSpec((1, tm, d), lambda b, i, h: (b, i, 0)),
        out_shape=jax.ShapeDtypeStruct((bsz, l, d), F32),
        scratch_shapes=[pltpu.VMEM((tm, d), BF16), pltpu.VMEM((tm, d), F32)],
        compiler_params=_cparams(("parallel", "parallel", "arbitrary")),
        name=name,
    )(x, g.reshape(1, d), scale, shift, gate, w_in, w_in, w_out, final_g.reshape(1, d))


def _conv_silu(raw_ref, pad_ref, w_ref, b_ref, out_ref, first, lc, width, cb=512):
    @pl.when(first)
    def _():
        pad_ref[0:8, :] = jnp.zeros((8, width), F32)

    pad_ref[8:8 + lc, :] = raw_ref[0]
    for c0 in range(0, width, cb):
        cs = slice(c0, c0 + cb)
        acc = b_ref[:, cs] + w_ref[3:4, cs] * pad_ref[8:8 + lc, cs]
        acc = acc + w_ref[2:3, cs] * pad_ref[7:7 + lc, cs]
        acc = acc + w_ref[1:2, cs] * pad_ref[6:6 + lc, cs]
        acc = acc + w_ref[0:1, cs] * pad_ref[5:5 + lc, cs]
        out_ref[:, cs] = _silu(acc)
    pad_ref[0:8, :] = pad_ref[lc:lc + 8, :]


def _ssd_kernel(z_ref, xs_ref, bm_ref, cm_ref, dt_ref,
                cwx_ref, cwb_ref, cwc_ref, cbx_ref, cbb_ref, cbc_ref,
                dtb_ref, alog_ref, dsk_ref, ng_ref, e_ref, tri_ref,
                o_ref,
                padx_s, padb_s, padc_s, xs_s, bm_s, cm_s, rhs_s, st_s, *, lc):
    first = pl.program_id(1) == 0

    @pl.when(first)
    def _():
        st_s[...] = jnp.zeros_like(st_s)

    _conv_silu(xs_ref, padx_s, cwx_ref, cbx_ref, xs_s, first, lc, SSD_INNER)
    _conv_silu(bm_ref, padb_s, cwb_ref, cbb_ref, bm_s, first, lc, SSD_BC)
    _conv_silu(cm_ref, padc_s, cwc_ref, cbc_ref, cm_s, first, lc, SSD_BC)

    rhs_s[0:lc, :] = xs_s[...].astype(BF16)
    rhs_s[lc:lc + SSD_STATE, :] = st_s[...].astype(BF16)

    v = dt_ref[0] + dtb_ref[...]
    dt = jnp.maximum(v, 0.0) + jnp.log(1.0 + jnp.exp(-jnp.abs(v)))
    a = -jnp.exp(alog_ref[...])
    acum = _sel_left(tri_ref[...], dt * a)
    acum_last = acum[lc - 1:lc, :]
    wdec = jnp.exp(acum_last - acum) * dt
    eacum = jnp.exp(acum)
    acum_t = acum.T
    dt_t = dt.T
    wexp = _sel_right(wdec, e_ref[...])
    eal = _sel_right(jnp.broadcast_to(jnp.exp(acum_last), (8, LANES)), e_ref[...])[0:1, :]

    row = lax.broadcasted_iota(jnp.int32, (lc, lc), 0)
    col = lax.broadcasted_iota(jnp.int32, (lc, lc), 1)
    causal = row >= col
    lane = lax.broadcasted_iota(jnp.int32, (lc, LANES), 1)
    low_half = lane < SSD_HEADDIM

    for g in range(SSD_GROUPS):
        gs = slice(g * SSD_STATE, (g + 1) * SSD_STATE)
        bm_g = bm_s[:, gs]
        cm_g = cm_s[:, gs]
        bm_gb = bm_g.astype(BF16)
        scores = _dot_nt(cm_g.astype(BF16), bm_gb)
        pieces = []
        for j in range(SSD_HPG // 2):
            ps = slice(g * SSD_GW + j * LANES, g * SSD_GW + (j + 1) * LANES)
            rhs = rhs_s[:, ps]
            ys = []
            for hh in range(2):
                h = g * SSD_HPG + 2 * j + hh
                seg = acum[:, h:h + 1] - acum_t[h:h + 1, :]
                lmat = jnp.exp(jnp.where(causal, seg, -1e30))
                amat = scores * lmat * dt_t[h:h + 1, :]
                cs = cm_g * eacum[:, h:h + 1]
                lhs = jnp.concatenate([amat, cs], axis=1).astype(BF16)
                ys.append(_dot(lhs, rhs))
            pieces.append(jnp.where(low_half, ys[0], ys[1]))
        y = jnp.concatenate(pieces, axis=1)

        ws = slice(g * SSD_GW, (g + 1) * SSD_GW)
        xs_g = xs_s[:, ws]
        xw = (xs_g * wexp[:, ws]).astype(BF16)
        st_s[:, ws] = st_s[:, ws] * eal[:, ws] + _dot_tn(bm_gb, xw)

        y = y + dsk_ref[:, ws] * xs_g
        y = y * _silu(z_ref[0, :, ws])
        ms = jnp.mean(y * y, axis=-1, keepdims=True)
        o_ref[0, :, ws] = (y * lax.rsqrt(ms + EPS) * ng_ref[:, ws]).astype(o_ref.dtype)


def _ssd(proj, dtp, conv_w, conv_b, dt_bias, a_log, d_skip, norm_g):
    bsz, l, _ = proj.shape
    lc = min(SSD_LC, l)
    i0 = SSD_INNER
    cwx, cwb, cwc = conv_w[:, :i0], conv_w[:, i0:i0 + SSD_BC], conv_w[:, i0 + SSD_BC:]
    cb = conv_b.reshape(1, -1)
    cbx, cbb, cbc = cb[:, :i0], cb[:, i0:i0 + SSD_BC], cb[:, i0 + SSD_BC:]
    pad = LANES - SSD_HEADS
    dtb = jnp.pad(dt_bias, (0, pad)).reshape(1, LANES)
    alog = jnp.pad(a_log, (0, pad)).reshape(1, LANES)
    dsk = jnp.repeat(d_skip, SSD_HEADDIM).reshape(1, i0)
    head_of_lane = np.arange(i0) // SSD_HEADDIM
    expand = jnp.asarray(np.arange(LANES)[:, None] == head_of_lane[None, :], BF16)
    tri = jnp.asarray(np.tril(np.ones((lc, lc))), BF16)

    def full(shape):
        return pl.BlockSpec(shape, lambda b, k: (0,) * len(shape))

    nb3 = i0 // SSD_BC
    return pl.pallas_call(
        functools.partial(_ssd_kernel, lc=lc),
        grid=(bsz, l // lc),
        in_specs=[pl.BlockSpec((1, lc, i0), lambda b, k: (b, k, 0)),
                  pl.BlockSpec((1, lc, i0), lambda b, k: (b, k, 1)),
                  pl.BlockSpec((1, lc, SSD_BC), lambda b, k: (b, k, 2 * nb3)),
                  pl.BlockSpec((1, lc, SSD_BC), lambda b, k: (b, k, 2 * nb3 + 1)),
                  pl.BlockSpec((1, lc, LANES), lambda b, k: (b, k, 0)),
                  full((SSD_CONV, i0)), full((SSD_CONV, SSD_BC)), full((SSD_CONV, SSD_BC)),
                  full((1, i0)), full((1, SSD_BC)), full((1, SSD_BC)),
                  full((1, LANES)), full((1, LANES)), full((1, i0)), full((1, i0)),
                  full((LANES, i0)), full((lc, lc))],
        out_specs=pl.BlockSpec((1, lc, i0), lambda b, k: (b, k, 0)),
        out_shape=jax.ShapeDtypeStruct((bsz, l, i0), BF16),
        scratch_shapes=[pltpu.VMEM((lc + 8, i0), F32), pltpu.VMEM((lc + 8, SSD_BC), F32),
                        pltpu.VMEM((lc + 8, SSD_BC), F32),
                        pltpu.VMEM((lc, i0), F32), pltpu.VMEM((lc, SSD_BC), F32),
                        pltpu.VMEM((lc, SSD_BC), F32),
                        pltpu.VMEM((lc + SSD_STATE, i0), BF16), pltpu.VMEM((SSD_STATE, i0), F32)],
        compiler_params=_cparams(("parallel", "arbitrary")),
        name="ssd_scan",
    )(proj, proj, proj, proj, dtp, cwx, cwb, cwc, cbx, cbb, cbc, dtb, alog, dsk,
      norm_g.reshape(1, i0), expand, tri)


def _cplx_mul(ar, ai, br, bi):
    return ar * br - ai * bi, ar * bi + ai * br


def _zoh_coef(are, aim, delta):
    er = jnp.exp(are * delta)
    br = er * jnp.cos(aim * delta) - 1.0
    bi = er * jnp.sin(aim * delta)
    den = are * are + aim * aim
    return (br * are + bi * aim) / den, (bi * are - br * aim) / den


def _lam_pow(are, aim, delta, n):
    mag = jnp.exp((are * delta) * n)
    ang = (aim * delta) * n
    return mag * jnp.cos(ang), mag * jnp.sin(ang)


def _s5_kernel(u_ref, arc_ref, aic_ref, arr_ref, air_ref, ldt_ref,
               bT_re_ref, bT_im_ref, b_re_ref, b_im_ref, ct_re_ref, ct_im_ref, d_ref,
               o_ref,
               m_s, r0_s, s_re_s, s_im_s, xp_re_s, xp_im_s, *, nb, nchunks):
    hi = lax.Precision.HIGHEST
    nch, lc = S5_GROUP_CH, S5_CHUNK
    delta = jnp.exp(ldt_ref[0])
    arc, aic = arc_ref[0], aic_ref[0]
    arr, air = arr_ref[0], air_ref[0]
    tau = lax.broadcasted_iota(jnp.int32, (1, lc), 1).astype(F32)
    p_re, p_im = _lam_pow(arc, aic, delta, tau)
    q_re, q_im = _lam_pow(arc, aic, delta, (lc - 1.0) - tau)
    lb_re, lb_im = _lam_pow(arc, aic, delta, 1.0)
    cf_re_c, cf_im_c = _zoh_coef(arc, aic, delta)
    cf_re_r, cf_im_r = _zoh_coef(arr, air, delta)
    ct_re, ct_im = ct_re_ref[0], ct_im_ref[0]

    es = [_cplx_mul(ct_re[:, c:c + 1], ct_im[:, c:c + 1], p_re, p_im) for c in range(nch)]
    e_re = jnp.concatenate([e[0] for e in es], axis=1)
    e_im = jnp.concatenate([e[1] for e in es], axis=1)
    bbT_re, bbT_im = _cplx_mul(cf_re_r, cf_im_r, bT_re_ref[0], bT_im_ref[0])
    r0_s[...] = (jnp.dot(bbT_re, e_re, precision=hi, preferred_element_type=F32)
                 - jnp.dot(bbT_im, e_im, precision=hi, preferred_element_type=F32))

    srow = lax.broadcasted_iota(jnp.int32, (lc, lc), 0)
    tcol = lax.broadcasted_iota(jnp.int32, (lc, lc), 1)
    causal = tcol >= srow

    def fill(cp, carry):
        row = r0_s[pl.ds(cp, 1), :]
        r0 = pl.multiple_of(cp * lc, lc)
        for c in range(nch):
            blk = jnp.broadcast_to(row[:, c * lc:(c + 1) * lc], (lc, lc))
            blk = pltpu.roll(blk, 0, 1, stride=1, stride_axis=0)
            m_s[pl.ds(r0, lc), c * lc:(c + 1) * lc] = jnp.where(causal, blk, 0.0).astype(BF16)
        return carry

    lax.fori_loop(0, nch, fill, 0)

    bb_re, bb_im = _cplx_mul(cf_re_c, cf_im_c, b_re_ref[0], b_im_ref[0])
    ws = [_cplx_mul(q_re, q_im, bb_re[:, c:c + 1], bb_im[:, c:c + 1]) for c in range(nch)]
    wt_re = jnp.concatenate([w[0] for w in ws], axis=1).astype(BF16)
    wt_im = jnp.concatenate([w[1] for w in ws], axis=1).astype(BF16)
    ub = jnp.concatenate([u_ref[c] for c in range(nch)], axis=1).astype(BF16)
    s_re_s[...] = _dot_nt(ub, wt_re)
    s_im_s[...] = _dot_nt(ub, wt_im)

    c_re, c_im = _lam_pow(arr, air, delta, float(lc))
    xr = jnp.zeros((nb, S5_STATE), F32)
    xi = jnp.zeros((nb, S5_STATE), F32)
    for k in range(nchunks):
        rows = slice(k * nb, (k + 1) * nb)
        xp_re_s[rows, :] = xr
        xp_im_s[rows, :] = xi
        xr, xi = (xr * c_re - xi * c_im + s_re_s[rows, :],
                  xr * c_im + xi * c_re + s_im_s[rows, :])

    e1_re, e1_im = _cplx_mul(e_re, e_im, lb_re, lb_im)
    y = _dot(ub, m_s[...])
    y = y + _dot(xp_re_s[...].astype(BF16), e1_re.astype(BF16))
    y = y - _dot(xp_im_s[...].astype(BF16), e1_im.astype(BF16))
    for c in range(nch):
        yc = y[:, c * lc:(c + 1) * lc] + d_ref[0, c:c + 1, :] * u_ref[c]
        o_ref[c] = 0.5 * yc * (1.0 + jnp.tanh(math.sqrt(2.0 / math.pi) * (yc + 0.044715 * (yc * yc * yc))))


def _s5(ut, bsz, a_re, a_im, log_dt, b_re, b_im, c_re, c_im, d):
    width, t = ut.shape
    l = t // bsz
    g, p, ch, lc = S5_GROUPS, S5_STATE, S5_GROUP_CH, S5_CHUNK
    nchunks = l // lc
    nrows = nchunks * bsz
    u3 = ut.reshape(width, bsz, nchunks, lc).transpose(0, 2, 1, 3).reshape(width, nrows, lc)

    def per_group(shape):
        return pl.BlockSpec((1,) + shape, lambda i: (i, 0, 0))

    y3 = pl.pallas_call(
        functools.partial(_s5_kernel, nb=bsz, nchunks=nchunks),
        grid=(g,),
        in_specs=[pl.BlockSpec((ch, nrows, lc), lambda i: (i, 0, 0)),
                  per_group((p, 1)), per_group((p, 1)), per_group((1, p)), per_group((1, p)),
                  per_group((1, 1)),
                  per_group((ch, p)), per_group((ch, p)),
                  per_group((p, ch)), per_group((p, ch)),
                  per_group((p, ch)), per_group((p, ch)),
                  per_group((ch, 1))],
        out_specs=pl.BlockSpec((ch, nrows, lc), lambda i: (i, 0, 0)),
        out_shape=jax.ShapeDtypeStruct((width, nrows, lc), F32),
        scratch_shapes=[pltpu.VMEM((ch * lc, ch * lc), BF16), pltpu.VMEM((ch, ch * lc), F32),
                        pltpu.VMEM((nrows, p), F32), pltpu.VMEM((nrows, p), F32),
                        pltpu.VMEM((nrows, p), F32), pltpu.VMEM((nrows, p), F32)],
        compiler_params=_cparams(("parallel",)),
        name="s5_ssm",
    )(u3,
      a_re.reshape(g, p, 1), a_im.reshape(g, p, 1), a_re.reshape(g, 1, p), a_im.reshape(g, 1, p),
      log_dt.reshape(g, 1, 1),
      b_re.transpose(0, 2, 1), b_im.transpose(0, 2, 1), b_re, b_im,
      c_re.transpose(0, 2, 1), c_im.transpose(0, 2, 1),
      d.reshape(g, ch, 1))
    return y3.reshape(width, nchunks, bsz, lc).transpose(0, 2, 1, 3).reshape(width, t)


def _glu_kernel(yt_ref, w_ref, b_ref, o_ref):
    y = yt_ref[...].T
    gate = jax.nn.sigmoid(_dot(y.astype(BF16), w_ref[...]) + b_ref[...])
    o_ref[0] = (y * gate).astype(o_ref.dtype)


def _glu(yt, bsz, w, b, *, tm=512):
    n, t = yt.shape
    l = t // bsz
    tm = min(tm, l)
    nt = l // tm
    return pl.pallas_call(
        _glu_kernel,
        grid=(bsz, nt),
        in_specs=[pl.BlockSpec((n, tm), lambda b_, i: (0, b_ * nt + i)),
                  pl.BlockSpec((n, n), lambda b_, i: (0, 0)),
                  pl.BlockSpec((1, n), lambda b_, i: (0, 0))],
        out_specs=pl.BlockSpec((1, tm, n), lambda b_, i: (b_, i, 0)),
        out_shape=jax.ShapeDtypeStruct((bsz, l, n), BF16),
        compiler_params=_cparams(("parallel", "parallel")),
        name="s5_glu",
    )(yt, w, b.reshape(1, n))


def _ret_kernel(q_ref, k_ref, v_ref, g_ref, lg_ref, invf_ref, gn_ref, o_ref,
                r_s, d_s, cb_s, sb_s, *, tb):
    j = pl.program_id(1)
    idx = lax.broadcasted_iota(jnp.int32, (tb, 1), 0).astype(F32)

    @pl.when(j == 0)
    def _():
        r_s[...] = jnp.zeros_like(r_s)
        ri = lax.broadcasted_iota(jnp.int32, (tb, tb), 0)
        ci = lax.broadcasted_iota(jnp.int32, (tb, tb), 1)
        rc, cc = ri // RET_CHUNK, ci // RET_CHUNK
        dist = jnp.where(rc == cc, jnp.abs(ri - ci), jnp.maximum(ri - ci, 0)).astype(F32)
        for h in range(RET_HEADS):
            d_s[h] = jnp.where(cc > rc, 0.0, jnp.exp(lg_ref[h] * dist) * (RET_QK ** -0.5))
        ang = idx * invf_ref[...]
        cb_s[...] = jnp.cos(ang)
        sb_s[...] = jnp.sin(ang)

    base = (j * tb).astype(F32) * invf_ref[...]
    cos_a, sin_a = jnp.cos(base), jnp.sin(base)
    cos = cos_a * cb_s[...] - sin_a * sb_s[...]
    sin = sin_a * cb_s[...] + cos_a * sb_s[...]
    half = RET_QK // 2

    def rot(t):
        t1, t2 = t[:, :half], t[:, half:]
        return jnp.concatenate([t1 * cos - t2 * sin, t1 * sin + t2 * cos], axis=1)

    for h in range(RET_HEADS):
        lg = lg_ref[h]
        qs = slice(h * RET_QK, (h + 1) * RET_QK)
        vs = slice(h * RET_V, (h + 1) * RET_V)
        q = rot(q_ref[0, :, qs])
        k = rot(k_ref[0, :, qs])
        vb = v_ref[0, :, vs].astype(BF16)
        xi = jnp.exp(lg * (idx + 1.0))
        zeta = jnp.exp(lg * (tb - 1.0 - idx)) * (RET_QK ** -0.5)
        gblk = jnp.exp(lg * float(tb))

        r = r_s[h]
        s = _dot_nt(q.astype(BF16), k.astype(BF16)) * d_s[h]
        o = _dot(s.astype(BF16), vb) + _dot((q * xi).astype(BF16), r.astype(BF16))
        r_s[h] = r * gblk + _dot_tn((k * zeta).astype(BF16), vb)

        mu = jnp.mean(o, axis=-1, keepdims=True)
        oc = o - mu
        y = oc * lax.rsqrt(jnp.mean(oc * oc, axis=-1, keepdims=True) + EPS)
        o_ref[0, :, vs] = (y * gn_ref[:, vs] * _silu(g_ref[0, :, vs])).astype(o_ref.dtype)


def _retention(proj, gn_g):
    bsz, l, _ = proj.shape
    tb = min(RET_BLOCK, l)
    hq = RET_HEADS
    dq, dv = hq * RET_QK, hq * RET_V
    log_g = jnp.log(1.0 - 2.0 ** (-5.0 - jnp.arange(hq, dtype=F32))).reshape(hq, 1, 1)
    inv_freq = (ROPE_BASE ** (-jnp.arange(0, RET_QK, 2, dtype=F32) / RET_QK)).reshape(1, RET_QK // 2)
    return pl.pallas_call(
        functools.partial(_ret_kernel, tb=tb),
        grid=(bsz, l // tb),
        in_specs=[pl.BlockSpec((1, tb, dq), lambda b, j: (b, j, 0)),
                  pl.BlockSpec((1, tb, dq), lambda b, j: (b, j, 1)),
                  pl.BlockSpec((1, tb, dv), lambda b, j: (b, j, 1)),
                  pl.BlockSpec((1, tb, dv), lambda b, j: (b, j, 2)),
                  pl.BlockSpec((hq, 1, 1), lambda b, j: (0, 0, 0)),
                  pl.BlockSpec((1, RET_QK // 2), lambda b, j: (0, 0)),
                  pl.BlockSpec((1, dv), lambda b, j: (0, 0))],
        out_specs=pl.BlockSpec((1, tb, dv), lambda b, j: (b, j, 0)),
        out_shape=jax.ShapeDtypeStruct((bsz, l, dv), BF16),
        scratch_shapes=[pltpu.VMEM((hq, RET_QK, RET_V), F32), pltpu.VMEM((hq, tb, tb), F32),
                        pltpu.VMEM((tb, RET_QK // 2), F32), pltpu.VMEM((tb, RET_QK // 2), F32)],
        compiler_params=_cparams(("parallel", "arbitrary")),
        name="retention",
    )(proj, proj, proj, proj, log_g, inv_freq, gn_g.reshape(1, -1))


def _split_mod(m, bsz):
    d = m.shape[-1] // 3
    return tuple(m[:, k * d:(k + 1) * d].reshape(bsz, 1, d) for k in range(3))


def kernel(x, c, norm_mix_g, ada_mix_w, ada_mix_b, norm_ffn_g, ada_ffn_w, ada_ffn_b, ffn_w_in, ffn_w_out, ab_w_in, ssd_conv_w, ssd_conv_b, ssd_dt_bias, ssd_a_log, ssd_d, ssd_norm_g, s5_a_re, s5_a_im, s5_log_dt, s5_b_re, s5_b_im, s5_c_re, s5_c_im, s5_d, s5_glu_w, s5_glu_b, ab_w_out, ret_w_in, ret_gn_g, ret_w_out, final_norm_g):
    bsz = x.shape[0]
    mod_mix = _ada_modulation(c, ada_mix_w, ada_mix_b)
    mod_ffn = _ada_modulation(c, ada_ffn_w, ada_ffn_b)

    shift, scale, gate = _split_mod(mod_mix[0], bsz)
    w0 = ab_w_in[0]
    n_main = SSD_INNER + SSD_XBC
    w_main = w0[:, :n_main].astype(BF16)
    w_dt = jnp.pad(w0[:, n_main:n_main + SSD_HEADS], ((0, 0), (0, LANES - SSD_HEADS))).astype(BF16)
    w_ut = w0[:, n_main + SSD_HEADS:].T.astype(BF16)
    proj0, dtp, ut = _in_proj0(x, norm_mix_g[0], scale, shift, w_main, w_dt, w_ut)
    y_a = _ssd(proj0, dtp, ssd_conv_w[0], ssd_conv_b[0], ssd_dt_bias[0], ssd_a_log[0], ssd_d[0],
               ssd_norm_g[0])
    yt = _s5(ut, bsz, s5_a_re[0], s5_a_im[0], s5_log_dt[0], s5_b_re[0], s5_b_im[0],
             s5_c_re[0], s5_c_im[0], s5_d[0])
    y_b = _glu(yt, bsz, s5_glu_w[0].astype(BF16), s5_glu_b[0])
    x = _matmul_residual([y_a, y_b], ab_w_out[0].astype(BF16), x, gate, name="out_proj0")

    shift, scale, gate = _split_mod(mod_ffn[0], bsz)
    x = _ffn(x, norm_ffn_g[0], scale, shift, gate, ffn_w_in[0].astype(BF16),
             ffn_w_out[0].astype(BF16), final_norm_g, final_norm=False, name="ffn0")

    shift, scale, gate = _split_mod(mod_mix[1], bsz)
    proj1 = _norm_matmul(x, norm_mix_g[1], scale, shift, ret_w_in[0].astype(BF16), name="in_proj1")
    y_r = _retention(proj1, ret_gn_g[0])
    x = _matmul_residual([y_r], ret_w_out[0].astype(BF16), x, gate, name="out_proj1")

    shift, scale, gate = _split_mod(mod_ffn[1], bsz)
    return _ffn(x, norm_ffn_g[1], scale, shift, gate, ffn_w_in[1].astype(BF16),
                ffn_w_out[1].astype(BF16), final_norm_g, final_norm=True, name="ffn1")
```

```python
import functools
import math

import numpy as np
import jax
import jax.numpy as jnp
from jax import lax
from jax.experimental import pallas as pl
from jax.experimental.pallas import tpu as pltpu

F32 = jnp.float32
BF16 = jnp.bfloat16
EPS = 1e-6

D_MODEL = 2048
MIX_WIDTH = 2 * D_MODEL
SSD_HEADDIM = 64
SSD_INNER = 3 * MIX_WIDTH // 4
SSD_HEADS = SSD_INNER // SSD_HEADDIM
SSD_GROUPS = 8
SSD_HPG = SSD_HEADS // SSD_GROUPS
SSD_STATE = 128
SSD_CONV = 4
SSD_GW = SSD_HPG * SSD_HEADDIM
SSD_BC = SSD_GROUPS * SSD_STATE
SSD_XBC = SSD_INNER + 2 * SSD_BC
S5_WIDTH = MIX_WIDTH - SSD_INNER
S5_GROUP_CH = 16
S5_GROUPS = S5_WIDTH // S5_GROUP_CH
S5_STATE = 64
S5_CHUNK = 128
RET_HEADS = 8
RET_QK = D_MODEL // RET_HEADS
RET_V = MIX_WIDTH // RET_HEADS
RET_CHUNK = 64
RET_BLOCK = 256
ROPE_BASE = 10000.0
FFN_HIDDEN = ((-(-8 * D_MODEL // 3) + 255) // 256) * 256

LANES = 128
SSD_LC = 128
VMEM_LIMIT = 56 * 1024 * 1024


def _cparams(sem):
    return pltpu.CompilerParams(dimension_semantics=sem, vmem_limit_bytes=VMEM_LIMIT)


def _dot(a, b):
    return jnp.dot(a, b, preferred_element_type=F32)


def _dot_nt(a, b):
    return lax.dot_general(a, b, (((1,), (1,)), ((), ())), preferred_element_type=F32)


def _dot_tn(a, b):
    return lax.dot_general(a, b, (((0,), (0,)), ((), ())), preferred_element_type=F32)


def _split3(v):
    hi = v.astype(BF16)
    r = v - hi.astype(F32)
    mid = r.astype(BF16)
    lo = (r - mid.astype(F32)).astype(BF16)
    return hi, mid, lo


def _sel_right(v, e):
    hi, mid, lo = _split3(v)
    return (_dot(lo, e) + _dot(mid, e)) + _dot(hi, e)


def _sel_left(e, v):
    hi, mid, lo = _split3(v)
    return (_dot(e, lo) + _dot(e, mid)) + _dot(e, hi)


def _silu(v):
    h = 0.5 * v
    return h + h * jnp.tanh(h)


NORM_ROWS = 32


def _store_modulated_norm(x_ref, g_ref, sc_ref, sh_ref, hn_ref):
    gm = g_ref[...] * (1.0 + sc_ref[0])
    sh = sh_ref[0]

    def body(i, carry):
        r0 = pl.multiple_of(i * NORM_ROWS, NORM_ROWS)
        x = x_ref[0, pl.ds(r0, NORM_ROWS), :]
        ms = jnp.mean(x * x, axis=-1, keepdims=True)
        hn_ref[pl.ds(r0, NORM_ROWS), :] = ((x * lax.rsqrt(ms + EPS)) * gm + sh).astype(BF16)
        return carry

    lax.fori_loop(0, hn_ref.shape[0] // NORM_ROWS, body, 0, unroll=4)


def _ada_kernel(c_ref, w_ref, b_ref, o_ref):
    sc = _silu(c_ref[...]).astype(BF16)
    o_ref[0] = _dot(sc, w_ref[0].astype(BF16)) + b_ref[0]


def _ada_modulation(c, w, b):
    depth, d, n = w.shape
    bsz = c.shape[0]
    tn = 1024
    return pl.pallas_call(
        _ada_kernel,
        grid=(depth, n // tn),
        in_specs=[pl.BlockSpec((bsz, d), lambda i, j: (0, 0)),
                  pl.BlockSpec((1, d, tn), lambda i, j: (i, 0, j)),
                  pl.BlockSpec((1, 1, tn), lambda i, j: (i, 0, j))],
        out_specs=pl.BlockSpec((1, bsz, tn), lambda i, j: (i, 0, j)),
        out_shape=jax.ShapeDtypeStruct((depth, bsz, n), F32),
        compiler_params=_cparams(("parallel", "parallel")),
        name="ada_modulation",
    )(c, w, b.reshape(depth, 1, n))


def _norm_mm_kernel(x_ref, g_ref, sc_ref, sh_ref, w_ref, o_ref, hn_ref):
    @pl.when(pl.program_id(2) == 0)
    def _():
        _store_modulated_norm(x_ref, g_ref, sc_ref, sh_ref, hn_ref)

    o_ref[0] = _dot(hn_ref[...], w_ref[...].astype(BF16)).astype(o_ref.dtype)


def _in_proj0_kernel(x_ref, g_ref, sc_ref, sh_ref, w_ref, wdt_ref, wut_ref, o_ref, dt_ref, ut_ref, hn_ref,
                     *, n_main):
    n = pl.program_id(2)

    @pl.when(n == 0)
    def _():
        _store_modulated_norm(x_ref, g_ref, sc_ref, sh_ref, hn_ref)
        dt_ref[0] = _dot(hn_ref[...], wdt_ref[...])

    @pl.when(n < n_main)
    def _():
        o_ref[0] = _dot(hn_ref[...], w_ref[...].astype(BF16))

    @pl.when(n >= n_main)
    def _():
        ut_ref[...] = _dot_nt(wut_ref[...], hn_ref[...])


def _norm_matmul(x, g, scale, shift, w, layer, *, tm=1024, tn=512, name):
    bsz, l, d = x.shape
    n = w.shape[2]
    tm = min(tm, l)
    return pl.pallas_call(
        _norm_mm_kernel,
        grid=(bsz, l // tm, n // tn),
        in_specs=[pl.BlockSpec((1, tm, d), lambda b, i, j: (b, i, 0)),
                  pl.BlockSpec((1, d), lambda b, i, j: (0, 0)),
                  pl.BlockSpec((1, 1, d), lambda b, i, j: (b, 0, 0)),
                  pl.BlockSpec((1, 1, d), lambda b, i, j: (b, 0, 0)),
                  pl.BlockSpec((None, d, tn), lambda b, i, j: (layer, 0, j))],
        out_specs=pl.BlockSpec((1, tm, tn), lambda b, i, j: (b, i, j)),
        out_shape=jax.ShapeDtypeStruct((bsz, l, n), F32),
        scratch_shapes=[pltpu.VMEM((tm, d), BF16)],
        compiler_params=_cparams(("parallel", "parallel", "arbitrary")),
        name=name,
    )(x, g.reshape(1, d), scale, shift, w)


def _in_proj0(x, g, scale, shift, w, layer, n, w_dt, w_ut, *, tm=1024, tn=512, tc=512):
    bsz, l, d = x.shape
    ndt, nut = w_dt.shape[1], w_ut.shape[0]
    tm = min(tm, l)
    n_main, n_ut, nt = n // tn, nut // tc, l // tm
    return pl.pallas_call(
        functools.partial(_in_proj0_kernel, n_main=n_main),
        grid=(bsz, nt, n_main + n_ut),
        in_specs=[pl.BlockSpec((1, tm, d), lambda b, i, j: (b, i, 0)),
                  pl.BlockSpec((1, d), lambda b, i, j: (0, 0)),
                  pl.BlockSpec((1, 1, d), lambda b, i, j: (b, 0, 0)),
                  pl.BlockSpec((1, 1, d), lambda b, i, j: (b, 0, 0)),
                  pl.BlockSpec((None, d, tn), lambda b, i, j: (layer, 0, jnp.minimum(j, n_main - 1))),
                  pl.BlockSpec((d, ndt), lambda b, i, j: (0, 0)),
                  pl.BlockSpec((tc, d), lambda b, i, j: (jnp.clip(j - n_main, 0, n_ut - 1), 0))],
        out_specs=[pl.BlockSpec((1, tm, tn), lambda b, i, j: (b, i, jnp.minimum(j, n_main - 1))),
                   pl.BlockSpec((1, tm, ndt), lambda b, i, j: (b, i, 0)),
                   pl.BlockSpec((tc, tm), lambda b, i, j: (jnp.clip(j - n_main, 0, n_ut - 1), b * nt + i))],
        out_shape=[jax.ShapeDtypeStruct((bsz, l, n), F32),
                   jax.ShapeDtypeStruct((bsz, l, ndt), F32),
                   jax.ShapeDtypeStruct((nut, bsz * l), F32)],
        scratch_shapes=[pltpu.VMEM((tm, d), BF16)],
        compiler_params=_cparams(("parallel", "parallel", "arbitrary")),
        name="in_proj0",
    )(x, g.reshape(1, d), scale, shift, w, w_dt, w_ut)


def _mm_res_kernel(*refs, bounds):
    na = len(bounds) - 1
    a_refs = refs[:na]
    w_ref, x_ref, gate_ref, o_ref, acc_ref = refs[na:]
    k = pl.program_id(2)

    @pl.when(k == 0)
    def _():
        acc_ref[...] = jnp.zeros_like(acc_ref)

    for i, a_ref in enumerate(a_refs):
        @pl.when((k >= bounds[i]) & (k < bounds[i + 1]))
        def _(a_ref=a_ref):
            acc_ref[...] += _dot(a_ref[0], w_ref[...])

    @pl.when(k == bounds[-1] - 1)
    def _():
        o_ref[0] = x_ref[0] + gate_ref[0] * acc_ref[...]


def _matmul_residual(a_list, w, x, gate, *, tm=512, tk=1024, name):
    bsz, l, d = x.shape
    kk, n = w.shape
    tm = min(tm, l)
    bounds = [0]
    for a in a_list:
        bounds.append(bounds[-1] + a.shape[2] // tk)
    assert bounds[-1] * tk == kk

    def a_spec(lo, hi):
        return pl.BlockSpec((1, tm, tk), lambda b, i, k: (b, i, jnp.clip(k - lo, 0, hi - lo - 1)))

    return pl.pallas_call(
        functools.partial(_mm_res_kernel, bounds=tuple(bounds)),
        grid=(bsz, l // tm, bounds[-1]),
        in_specs=[a_spec(bounds[i], bounds[i + 1]) for i in range(len(a_list))] + [
            pl.BlockSpec((tk, n), lambda b, i, k: (k, 0)),
            pl.BlockSpec((1, tm, d), lambda b, i, k: (b, i, 0)),
            pl.BlockSpec((1, 1, d), lambda b, i, k: (b, 0, 0))],
        out_specs=pl.BlockSpec((1, tm, n), lambda b, i, k: (b, i, 0)),
        out_shape=jax.ShapeDtypeStruct((bsz, l, n), F32),
        scratch_shapes=[pltpu.VMEM((tm, n), F32)],
        compiler_params=_cparams(("parallel", "parallel", "arbitrary")),
        name=name,
    )(*a_list, w, x, gate)


def _ffn_kernel(x_ref, g_ref, sc_ref, sh_ref, gate_ref, wg_ref, wu_ref, wo_ref, fg_ref,
                o_ref, hn_ref, *, nh, final_norm):
    h = pl.program_id(2)

    @pl.when(h == 0)
    def _():
        _store_modulated_norm(x_ref, g_ref, sc_ref, sh_ref, hn_ref)
        o_ref[...] = jnp.zeros_like(o_ref)

    hn = hn_ref[...]
    gt = _dot(hn, wg_ref[...].astype(BF16))
    up = _dot(hn, wu_ref[...].astype(BF16))
    act = (_silu(gt) * up).astype(BF16)
    o_ref[0] += _dot(act, wo_ref[...].astype(BF16))

    @pl.when(h == nh - 1)
    def _():
        y = x_ref[0] + gate_ref[0] * o_ref[0]
        if final_norm:
            ms = jnp.mean(y * y, axis=-1, keepdims=True)
            y = y * lax.rsqrt(ms + EPS) * fg_ref[...]
        o_ref[0] = y


def _ffn(x, g, scale, shift, gate, w_in, w_out, layer, final_g, *, final_norm, tm=1024, th=256, name):
    bsz, l, d = x.shape
    hid = w_out.shape[1]
    tm = min(tm, l)
    nh = hid // th
    vec = pl.BlockSpec((1, 1, d), lambda b, i, h: (b, 0, 0))
    return pl.pallas_call(
        functools.partial(_ffn_kernel, nh=nh, final_norm=final_norm),
        grid=(bsz, l // tm, nh),
        in_specs=[pl.BlockSpec((1, tm, d), lambda b, i, h: (b, i, 0), pipeline_mode=pl.Buffered(1)),
                  pl.BlockSpec((1, d), lambda b, i, h: (0, 0)),
                  vec, vec, vec,
                  pl.BlockSpec((None, d, th), lambda b, i, h: (layer, 0, h)),
                  pl.BlockSpec((None, d, th), lambda b, i, h: (layer, 0, h + nh)),
                  pl.BlockSpec((None, th, d), lambda b, i, h: (layer, h, 0)),
                  pl.BlockSpec((1, d), lambda b, i, h: (0, 0))],
        out_specs=pl.BlockSpec((1, tm, d), lambda b, i, h: (b, i, 0)),
        out_shape=jax.ShapeDtypeStruct((bsz, l, d), F32),
        scratch_shapes=[pltpu.VMEM((tm, d), BF16)],
        compiler_params=_cparams(("parallel", "parallel", "arbitrary")),
        name=name,
    )(x, g.reshape(1, d), scale, shift, gate, w_in, w_in, w_out, final_g.reshape(1, d))


def _conv_silu(raw_ref, pad_ref, w_ref, b_ref, out_ref, first, lc, width, cb=256):
    @pl.when(first)
    def _():
        pad_ref[0:8, :] = jnp.zeros((8, width), F32)

    pad_ref[8:8 + lc, :] = raw_ref[0]
    ntap = SSD_CONV
    for c0 in range(0, width, cb):
        cs = slice(c0, c0 + cb)
        blk = pad_ref[:, cs]
        acc = b_ref[:, cs] + w_ref[ntap - 1:ntap, cs] * blk[8:]
        for j in range(1, ntap):
            acc = acc + w_ref[ntap - 1 - j:ntap - j, cs] * pltpu.roll(blk, j, 0)[8:]
        out_ref[:, cs] = _silu(acc)
    pad_ref[0:8, :] = pad_ref[lc:lc + 8, :]


def _ssd_kernel(z_ref, xs_ref, bm_ref, cm_ref, dt_ref,
                cwx_ref, cwb_ref, cwc_ref, cbx_ref, cbb_ref, cbc_ref,
                dtb_ref, alog_ref, dsk_ref, ng_ref, e_ref, tri_ref,
                o_ref,
                padx_s, padb_s, padc_s, xs_s, bm_s, cm_s, rhs_s, st_s, *, lc):
    first = pl.program_id(1) == 0

    @pl.when(first)
    def _():
        st_s[...] = jnp.zeros_like(st_s)

    _conv_silu(xs_ref, padx_s, cwx_ref, cbx_ref, xs_s, first, lc, SSD_INNER)
    _conv_silu(bm_ref, padb_s, cwb_ref, cbb_ref, bm_s, first, lc, SSD_BC)
    _conv_silu(cm_ref, padc_s, cwc_ref, cbc_ref, cm_s, first, lc, SSD_BC)

    rhs_s[0:lc, :] = xs_s[...].astype(BF16)
    rhs_s[lc:lc + SSD_STATE, :] = st_s[...].astype(BF16)

    v = dt_ref[0] + dtb_ref[...]
    dt = jnp.maximum(v, 0.0) + jnp.log(1.0 + jnp.exp(-jnp.abs(v)))
    a = -jnp.exp(alog_ref[...])
    acum = _sel_left(tri_ref[...], dt * a)
    acum_last = acum[lc - 1:lc, :]
    wdec = jnp.exp(acum_last - acum) * dt
    eacum = jnp.exp(acum)
    acum_t = acum.T
    dt_t = dt.T
    wexp = _sel_right(wdec, e_ref[...])
    eal = _sel_right(jnp.broadcast_to(jnp.exp(acum_last), (8, LANES)), e_ref[...])[0:1, :]

    row = lax.broadcasted_iota(jnp.int32, (lc, lc), 0)
    col = lax.broadcasted_iota(jnp.int32, (lc, lc), 1)
    causal = row >= col
    lane = lax.broadcasted_iota(jnp.int32, (lc, LANES), 1)
    low_half = lane < SSD_HEADDIM

    for g in range(SSD_GROUPS):
        gs = slice(g * SSD_STATE, (g + 1) * SSD_STATE)
        bm_g = bm_s[:, gs]
        cm_g = cm_s[:, gs]
        bm_gb = bm_g.astype(BF16)
        scores = _dot_nt(cm_g.astype(BF16), bm_gb)
        pieces = []
        for j in range(SSD_HPG // 2):
            ps = slice(g * SSD_GW + j * LANES, g * SSD_GW + (j + 1) * LANES)
            rhs = rhs_s[:, ps]
            ys = []
            for hh in range(2):
                h = g * SSD_HPG + 2 * j + hh
                seg = acum[:, h:h + 1] - acum_t[h:h + 1, :]
                lmat = jnp.exp(jnp.where(causal, seg, -1e30))
                amat = scores * lmat * dt_t[h:h + 1, :]
                cs = cm_g * eacum[:, h:h + 1]
                lhs = jnp.concatenate([amat, cs], axis=1).astype(BF16)
                ys.append(_dot(lhs, rhs))
            pieces.append(jnp.where(low_half, ys[0], ys[1]))
        y = jnp.concatenate(pieces, axis=1)

        ws = slice(g * SSD_GW, (g + 1) * SSD_GW)
        xs_g = xs_s[:, ws]
        xw = (xs_g * wexp[:, ws]).astype(BF16)
        st_s[:, ws] = st_s[:, ws] * eal[:, ws] + _dot_tn(bm_gb, xw)

        y = y + dsk_ref[:, ws] * xs_g
        y = y * _silu(z_ref[0, :, ws])
        ms = jnp.mean(y * y, axis=-1, keepdims=True)
        o_ref[0, :, ws] = (y * lax.rsqrt(ms + EPS) * ng_ref[:, ws]).astype(o_ref.dtype)


def _ssd(proj, dtp, conv_w, conv_b, dt_bias, a_log, d_skip, norm_g):
    bsz, l, _ = proj.shape
    lc = min(SSD_LC, l)
    i0 = SSD_INNER
    cwx, cwb, cwc = conv_w[:, :i0], conv_w[:, i0:i0 + SSD_BC], conv_w[:, i0 + SSD_BC:]
    cb = conv_b.reshape(1, -1)
    cbx, cbb, cbc = cb[:, :i0], cb[:, i0:i0 + SSD_BC], cb[:, i0 + SSD_BC:]
    pad = LANES - SSD_HEADS
    dtb = jnp.pad(dt_bias, (0, pad)).reshape(1, LANES)
    alog = jnp.pad(a_log, (0, pad)).reshape(1, LANES)
    dsk = jnp.repeat(d_skip, SSD_HEADDIM).reshape(1, i0)
    head_of_lane = np.arange(i0) // SSD_HEADDIM
    expand = jnp.asarray(np.arange(LANES)[:, None] == head_of_lane[None, :], BF16)
    tri = jnp.asarray(np.tril(np.ones((lc, lc))), BF16)

    def full(shape):
        return pl.BlockSpec(shape, lambda b, k: (0,) * len(shape))

    nb3 = i0 // SSD_BC
    return pl.pallas_call(
        functools.partial(_ssd_kernel, lc=lc),
        grid=(bsz, l // lc),
        in_specs=[pl.BlockSpec((1, lc, i0), lambda b, k: (b, k, 0)),
                  pl.BlockSpec((1, lc, i0), lambda b, k: (b, k, 1)),
                  pl.BlockSpec((1, lc, SSD_BC), lambda b, k: (b, k, 2 * nb3)),
                  pl.BlockSpec((1, lc, SSD_BC), lambda b, k: (b, k, 2 * nb3 + 1)),
                  pl.BlockSpec((1, lc, LANES), lambda b, k: (b, k, 0)),
                  full((SSD_CONV, i0)), full((SSD_CONV, SSD_BC)), full((SSD_CONV, SSD_BC)),
                  full((1, i0)), full((1, SSD_BC)), full((1, SSD_BC)),
                  full((1, LANES)), full((1, LANES)), full((1, i0)), full((1, i0)),
                  full((LANES, i0)), full((lc, lc))],
        out_specs=pl.BlockSpec((1, lc, i0), lambda b, k: (b, k, 0)),
        out_shape=jax.ShapeDtypeStruct((bsz, l, i0), BF16),
        scratch_shapes=[pltpu.VMEM((lc + 8, i0), F32), pltpu.VMEM((lc + 8, SSD_BC), F32),
                        pltpu.VMEM((lc + 8, SSD_BC), F32),
                        pltpu.VMEM((lc, i0), F32), pltpu.VMEM((lc, SSD_BC), F32),
                        pltpu.VMEM((lc, SSD_BC), F32),
                        pltpu.VMEM((lc + SSD_STATE, i0), BF16), pltpu.VMEM((SSD_STATE, i0), F32)],
        compiler_params=_cparams(("parallel", "arbitrary")),
        name="ssd_scan",
    )(proj, proj, proj, proj, dtp, cwx, cwb, cwc, cbx, cbb, cbc, dtb, alog, dsk,
      norm_g.reshape(1, i0), expand, tri)


def _cplx_mul(ar, ai, br, bi):
    return ar * br - ai * bi, ar * bi + ai * br


def _zoh_coef(are, aim, delta):
    er = jnp.exp(are * delta)
    br = er * jnp.cos(aim * delta) - 1.0
    bi = er * jnp.sin(aim * delta)
    den = are * are + aim * aim
    return (br * are + bi * aim) / den, (bi * are - br * aim) / den


def _lam_pow(are, aim, delta, n):
    mag = jnp.exp((are * delta) * n)
    ang = (aim * delta) * n
    return mag * jnp.cos(ang), mag * jnp.sin(ang)


def _s5_kernel(u_ref, arc_ref, aic_ref, arr_ref, air_ref, ldt_ref,
               bT_re_ref, bT_im_ref, b_re_ref, b_im_ref, ct_re_ref, ct_im_ref, d_ref,
               o_ref,
               m_s, r0_s, s_re_s, s_im_s, xp_re_s, xp_im_s, *, nb, nchunks):
    hi = lax.Precision.HIGHEST
    nch, lc = S5_GROUP_CH, S5_CHUNK
    delta = jnp.exp(ldt_ref[0])
    arc, aic = arc_ref[0], aic_ref[0]
    arr, air = arr_ref[0], air_ref[0]
    tau = lax.broadcasted_iota(jnp.int32, (1, lc), 1).astype(F32)
    p_re, p_im = _lam_pow(arc, aic, delta, tau)
    q_re, q_im = _lam_pow(arc, aic, delta, (lc - 1.0) - tau)
    lb_re, lb_im = _lam_pow(arc, aic, delta, 1.0)
    cf_re_c, cf_im_c = _zoh_coef(arc, aic, delta)
    cf_re_r, cf_im_r = _zoh_coef(arr, air, delta)
    ct_re, ct_im = ct_re_ref[0], ct_im_ref[0]

    es = [_cplx_mul(ct_re[:, c:c + 1], ct_im[:, c:c + 1], p_re, p_im) for c in range(nch)]
    e_re = jnp.concatenate([e[0] for e in es], axis=1)
    e_im = jnp.concatenate([e[1] for e in es], axis=1)
    bbT_re, bbT_im = _cplx_mul(cf_re_r, cf_im_r, bT_re_ref[0], bT_im_ref[0])
    r0_s[...] = (jnp.dot(bbT_re, e_re, precision=hi, preferred_element_type=F32)
                 - jnp.dot(bbT_im, e_im, precision=hi, preferred_element_type=F32))

    srow = lax.broadcasted_iota(jnp.int32, (lc, lc), 0)
    tcol = lax.broadcasted_iota(jnp.int32, (lc, lc), 1)
    causal = tcol >= srow

    def fill(cp, carry):
        row = r0_s[pl.ds(cp, 1), :]
        r0 = pl.multiple_of(cp * lc, lc)
        for c in range(nch):
            blk = jnp.broadcast_to(row[:, c * lc:(c + 1) * lc], (lc, lc))
            blk = pltpu.roll(blk, 0, 1, stride=1, stride_axis=0)
            m_s[pl.ds(r0, lc), c * lc:(c + 1) * lc] = jnp.where(causal, blk, 0.0).astype(BF16)
        return carry

    lax.fori_loop(0, nch, fill, 0)

    bb_re, bb_im = _cplx_mul(cf_re_c, cf_im_c, b_re_ref[0], b_im_ref[0])
    ws = [_cplx_mul(q_re, q_im, bb_re[:, c:c + 1], bb_im[:, c:c + 1]) for c in range(nch)]
    wt_re = jnp.concatenate([w[0] for w in ws], axis=1).astype(BF16)
    wt_im = jnp.concatenate([w[1] for w in ws], axis=1).astype(BF16)
    ub = jnp.concatenate([u_ref[c] for c in range(nch)], axis=1).astype(BF16)
    s_re_s[...] = _dot_nt(ub, wt_re)
    s_im_s[...] = _dot_nt(ub, wt_im)

    c_re, c_im = _lam_pow(arr, air, delta, float(lc))
    xr = jnp.zeros((nb, S5_STATE), F32)
    xi = jnp.zeros((nb, S5_STATE), F32)
    for k in range(nchunks):
        rows = slice(k * nb, (k + 1) * nb)
        xp_re_s[rows, :] = xr
        xp_im_s[rows, :] = xi
        xr, xi = (xr * c_re - xi * c_im + s_re_s[rows, :],
                  xr * c_im + xi * c_re + s_im_s[rows, :])

    e1_re, e1_im = _cplx_mul(e_re, e_im, lb_re, lb_im)
    y = _dot(ub, m_s[...])
    y = y + _dot(xp_re_s[...].astype(BF16), e1_re.astype(BF16))
    y = y - _dot(xp_im_s[...].astype(BF16), e1_im.astype(BF16))
    for c in range(nch):
        yc = y[:, c * lc:(c + 1) * lc] + d_ref[0, c:c + 1, :] * u_ref[c]
        o_ref[c] = 0.5 * yc * (1.0 + jnp.tanh(math.sqrt(2.0 / math.pi) * (yc + 0.044715 * (yc * yc * yc))))


def _s5(ut, bsz, a_re, a_im, log_dt, b_re, b_im, c_re, c_im, d):
    width, t = ut.shape
    l = t // bsz
    g, p, ch, lc = S5_GROUPS, S5_STATE, S5_GROUP_CH, S5_CHUNK
    nchunks = l // lc
    nrows = nchunks * bsz
    u3 = ut.reshape(width, bsz, nchunks, lc).transpose(0, 2, 1, 3).reshape(width, nrows, lc)

    def per_group(shape):
        return pl.BlockSpec((1,) + shape, lambda i: (i, 0, 0))

    y3 = pl.pallas_call(
        functools.partial(_s5_kernel, nb=bsz, nchunks=nchunks),
        grid=(g,),
        in_specs=[pl.BlockSpec((ch, nrows, lc), lambda i: (i, 0, 0)),
                  per_group((p, 1)), per_group((p, 1)), per_group((1, p)), per_group((1, p)),
                  per_group((1, 1)),
                  per_group((ch, p)), per_group((ch, p)),
                  per_group((p, ch)), per_group((p, ch)),
                  per_group((p, ch)), per_group((p, ch)),
                  per_group((ch, 1))],
        out_specs=pl.BlockSpec((ch, nrows, lc), lambda i: (i, 0, 0)),
        out_shape=jax.ShapeDtypeStruct((width, nrows, lc), F32),
        scratch_shapes=[pltpu.VMEM((ch * lc, ch * lc), BF16), pltpu.VMEM((ch, ch * lc), F32),
                        pltpu.VMEM((nrows, p), F32), pltpu.VMEM((nrows, p), F32),
                        pltpu.VMEM((nrows, p), F32), pltpu.VMEM((nrows, p), F32)],
        compiler_params=_cparams(("parallel",)),
        name="s5_ssm",
    )(u3,
      a_re.reshape(g, p, 1), a_im.reshape(g, p, 1), a_re.reshape(g, 1, p), a_im.reshape(g, 1, p),
      log_dt.reshape(g, 1, 1),
      b_re.transpose(0, 2, 1), b_im.transpose(0, 2, 1), b_re, b_im,
      c_re.transpose(0, 2, 1), c_im.transpose(0, 2, 1),
      d.reshape(g, ch, 1))
    return y3.reshape(width, nchunks, bsz, lc).transpose(0, 2, 1, 3).reshape(width, t)


def _glu_kernel(yt_ref, w_ref, b_ref, o_ref):
    y = yt_ref[...].T
    gate = jax.nn.sigmoid(_dot(y.astype(BF16), w_ref[...]) + b_ref[...])
    o_ref[0] = (y * gate).astype(o_ref.dtype)


def _glu(yt, bsz, w, b, *, tm=512):
    n, t = yt.shape
    l = t // bsz
    tm = min(tm, l)
    nt = l // tm
    return pl.pallas_call(
        _glu_kernel,
        grid=(bsz, nt),
        in_specs=[pl.BlockSpec((n, tm), lambda b_, i: (0, b_ * nt + i)),
                  pl.BlockSpec((n, n), lambda b_, i: (0, 0)),
                  pl.BlockSpec((1, n), lambda b_, i: (0, 0))],
        out_specs=pl.BlockSpec((1, tm, n), lambda b_, i: (b_, i, 0)),
        out_shape=jax.ShapeDtypeStruct((bsz, l, n), BF16),
        compiler_params=_cparams(("parallel", "parallel")),
        name="s5_glu",
    )(yt, w, b.reshape(1, n))


def _ret_kernel(q_ref, k_ref, v_ref, g_ref, lg_ref, invf_ref, gn_ref, o_ref,
                r_s, d_s, cb_s, sb_s, *, tb):
    j = pl.program_id(1)
    idx = lax.broadcasted_iota(jnp.int32, (tb, 1), 0).astype(F32)

    @pl.when(j == 0)
    def _():
        r_s[...] = jnp.zeros_like(r_s)
        ri = lax.broadcasted_iota(jnp.int32, (tb, tb), 0)
        ci = lax.broadcasted_iota(jnp.int32, (tb, tb), 1)
        rc, cc = ri // RET_CHUNK, ci // RET_CHUNK
        dist = jnp.where(rc == cc, jnp.abs(ri - ci), jnp.maximum(ri - ci, 0)).astype(F32)
        for h in range(RET_HEADS):
            d_s[h] = jnp.where(cc > rc, 0.0, jnp.exp(lg_ref[h] * dist) * (RET_QK ** -0.5))
        ang = idx * invf_ref[...]
        cb_s[...] = jnp.cos(ang)
        sb_s[...] = jnp.sin(ang)

    base = (j * tb).astype(F32) * invf_ref[...]
    cos_a, sin_a = jnp.cos(base), jnp.sin(base)
    cos = cos_a * cb_s[...] - sin_a * sb_s[...]
    sin = sin_a * cb_s[...] + cos_a * sb_s[...]
    half = RET_QK // 2

    def rot(t):
        t1, t2 = t[:, :half], t[:, half:]
        return jnp.concatenate([t1 * cos - t2 * sin, t1 * sin + t2 * cos], axis=1)

    for h in range(RET_HEADS):
        lg = lg_ref[h]
        qs = slice(h * RET_QK, (h + 1) * RET_QK)
        vs = slice(h * RET_V, (h + 1) * RET_V)
        q = rot(q_ref[0, :, qs])
        k = rot(k_ref[0, :, qs])
        vb = v_ref[0, :, vs].astype(BF16)
        xi = jnp.exp(lg * (idx + 1.0))
        zeta = jnp.exp(lg * (tb - 1.0 - idx)) * (RET_QK ** -0.5)
        gblk = jnp.exp(lg * float(tb))

        r = r_s[h]
        s = _dot_nt(q.astype(BF16), k.astype(BF16)) * d_s[h]
        o = _dot(s.astype(BF16), vb) + _dot((q * xi).astype(BF16), r.astype(BF16))
        r_s[h] = r * gblk + _dot_tn((k * zeta).astype(BF16), vb)

        mu = jnp.mean(o, axis=-1, keepdims=True)
        oc = o - mu
        y = oc * lax.rsqrt(jnp.mean(oc * oc, axis=-1, keepdims=True) + EPS)
        o_ref[0, :, vs] = (y * gn_ref[:, vs] * _silu(g_ref[0, :, vs])).astype(o_ref.dtype)


def _retention(proj, gn_g):
    bsz, l, _ = proj.shape
    tb = min(RET_BLOCK, l)
    hq = RET_HEADS
    dq, dv = hq * RET_QK, hq * RET_V
    log_g = jnp.log(1.0 - 2.0 ** (-5.0 - jnp.arange(hq, dtype=F32))).reshape(hq, 1, 1)
    inv_freq = (ROPE_BASE ** (-jnp.arange(0, RET_QK, 2, dtype=F32) / RET_QK)).reshape(1, RET_QK // 2)
    return pl.pallas_call(
        functools.partial(_ret_kernel, tb=tb),
        grid=(bsz, l // tb),
        in_specs=[pl.BlockSpec((1, tb, dq), lambda b, j: (b, j, 0)),
                  pl.BlockSpec((1, tb, dq), lambda b, j: (b, j, 1)),
                  pl.BlockSpec((1, tb, dv), lambda b, j: (b, j, 1)),
                  pl.BlockSpec((1, tb, dv), lambda b, j: (b, j, 2)),
                  pl.BlockSpec((hq, 1, 1), lambda b, j: (0, 0, 0)),
                  pl.BlockSpec((1, RET_QK // 2), lambda b, j: (0, 0)),
                  pl.BlockSpec((1, dv), lambda b, j: (0, 0))],
        out_specs=pl.BlockSpec((1, tb, dv), lambda b, j: (b, j, 0)),
        out_shape=jax.ShapeDtypeStruct((bsz, l, dv), BF16),
        scratch_shapes=[pltpu.VMEM((hq, RET_QK, RET_V), F32), pltpu.VMEM((hq, tb, tb), F32),
                        pltpu.VMEM((tb, RET_QK // 2), F32), pltpu.VMEM((tb, RET_QK // 2), F32)],
        compiler_params=_cparams(("parallel", "arbitrary")),
        name="retention",
    )(proj, proj, proj, proj, log_g, inv_freq, gn_g.reshape(1, -1))


def _split_mod(m, bsz):
    d = m.shape[-1] // 3
    return tuple(m[:, k * d:(k + 1) * d].reshape(bsz, 1, d) for k in range(3))


def kernel(x, c, norm_mix_g, ada_mix_w, ada_mix_b, norm_ffn_g, ada_ffn_w, ada_ffn_b, ffn_w_in, ffn_w_out, ab_w_in, ssd_conv_w, ssd_conv_b, ssd_dt_bias, ssd_a_log, ssd_d, ssd_norm_g, s5_a_re, s5_a_im, s5_log_dt, s5_b_re, s5_b_im, s5_c_re, s5_c_im, s5_d, s5_glu_w, s5_glu_b, ab_w_out, ret_w_in, ret_gn_g, ret_w_out, final_norm_g):
    bsz = x.shape[0]
    mod_mix = _ada_modulation(c, ada_mix_w, ada_mix_b)
    mod_ffn = _ada_modulation(c, ada_ffn_w, ada_ffn_b)

    shift, scale, gate = _split_mod(mod_mix[0], bsz)
    w0 = ab_w_in[0]
    n_main = SSD_INNER + SSD_XBC
    w_dt = jnp.pad(w0[:, n_main:n_main + SSD_HEADS], ((0, 0), (0, LANES - SSD_HEADS))).astype(BF16)
    w_ut = w0[:, n_main + SSD_HEADS:].T.astype(BF16)
    proj0, dtp, ut = _in_proj0(x, norm_mix_g[0], scale, shift, ab_w_in, 0, n_main, w_dt, w_ut)
    y_a = _ssd(proj0, dtp, ssd_conv_w[0], ssd_conv_b[0], ssd_dt_bias[0], ssd_a_log[0], ssd_d[0],
               ssd_norm_g[0])
    yt = _s5(ut, bsz, s5_a_re[0], s5_a_im[0], s5_log_dt[0], s5_b_re[0], s5_b_im[0],
             s5_c_re[0], s5_c_im[0], s5_d[0])
    y_b = _glu(yt, bsz, s5_glu_w[0].astype(BF16), s5_glu_b[0])
    x = _matmul_residual([y_a, y_b], ab_w_out[0].astype(BF16), x, gate, name="out_proj0")

    shift, scale, gate = _split_mod(mod_ffn[0], bsz)
    x = _ffn(x, norm_ffn_g[0], scale, shift, gate, ffn_w_in, ffn_w_out, 0, final_norm_g,
             final_norm=False, name="ffn0")

    shift, scale, gate = _split_mod(mod_mix[1], bsz)
    proj1 = _norm_matmul(x, norm_mix_g[1], scale, shift, ret_w_in, 0, name="in_proj1")
    y_r = _retention(proj1, ret_gn_g[0])
    x = _matmul_residual([y_r], ret_w_out[0].astype(BF16), x, gate, name="out_proj1")

    shift, scale, gate = _split_mod(mod_ffn[1], bsz)
    return _ffn(x, norm_ffn_g[1], scale, shift, gate, ffn_w_in, ffn_w_out, 1, final_norm_g,
                final_norm=True, name="ffn1")
```

```python
import functools
import math

import numpy as np
import jax
import jax.numpy as jnp
from jax import lax
from jax.experimental import pallas as pl
from jax.experimental.pallas import tpu as pltpu

F32 = jnp.float32
BF16 = jnp.bfloat16
EPS = 1e-6

D_MODEL = 2048
MIX_WIDTH = 2 * D_MODEL
SSD_HEADDIM = 64
SSD_INNER = 3 * MIX_WIDTH // 4
SSD_HEADS = SSD_INNER // SSD_HEADDIM
SSD_GROUPS = 8
SSD_HPG = SSD_HEADS // SSD_GROUPS
SSD_STATE = 128
SSD_CONV = 4
SSD_GW = SSD_HPG * SSD_HEADDIM
SSD_BC = SSD_GROUPS * SSD_STATE
SSD_XBC = SSD_INNER + 2 * SSD_BC
S5_WIDTH = MIX_WIDTH - SSD_INNER
S5_GROUP_CH = 16
S5_GROUPS = S5_WIDTH // S5_GROUP_CH
S5_STATE = 64
S5_CHUNK = 128
RET_HEADS = 8
RET_QK = D_MODEL // RET_HEADS
RET_V = MIX_WIDTH // RET_HEADS
RET_CHUNK = 64
RET_BLOCK = 256
ROPE_BASE = 10000.0
FFN_HIDDEN = ((-(-8 * D_MODEL // 3) + 255) // 256) * 256

LANES = 128
SSD_LC = 128
VMEM_LIMIT = 56 * 1024 * 1024


def _cparams(sem):
    return pltpu.CompilerParams(dimension_semantics=sem, vmem_limit_bytes=VMEM_LIMIT)


def _dot(a, b):
    return jnp.dot(a, b, preferred_element_type=F32)


def _dot_nt(a, b):
    return lax.dot_general(a, b, (((1,), (1,)), ((), ())), preferred_element_type=F32)


def _dot_tn(a, b):
    return lax.dot_general(a, b, (((0,), (0,)), ((), ())), preferred_element_type=F32)


def _split3(v):
    hi = v.astype(BF16)
    r = v - hi.astype(F32)
    mid = r.astype(BF16)
    lo = (r - mid.astype(F32)).astype(BF16)
    return hi, mid, lo


def _sel_right(v, e):
    hi, mid, lo = _split3(v)
    return (_dot(lo, e) + _dot(mid, e)) + _dot(hi, e)


def _sel_left(e, v):
    hi, mid, lo = _split3(v)
    return (_dot(e, lo) + _dot(e, mid)) + _dot(e, hi)


def _silu(v):
    h = 0.5 * v
    return h + h * jnp.tanh(h)


NORM_ROWS = 32


def _store_modulated_norm(x_ref, g_ref, sc_ref, sh_ref, hn_ref):
    gm = g_ref[...] * (1.0 + sc_ref[0])
    sh = sh_ref[0]

    def body(i, carry):
        r0 = pl.multiple_of(i * NORM_ROWS, NORM_ROWS)
        x = x_ref[0, pl.ds(r0, NORM_ROWS), :]
        ms = jnp.mean(x * x, axis=-1, keepdims=True)
        hn_ref[pl.ds(r0, NORM_ROWS), :] = ((x * lax.rsqrt(ms + EPS)) * gm + sh).astype(BF16)
        return carry

    lax.fori_loop(0, hn_ref.shape[0] // NORM_ROWS, body, 0, unroll=4)


def _ada_kernel(c_ref, w_ref, b_ref, o_ref):
    sc = _silu(c_ref[...]).astype(BF16)
    o_ref[0] = _dot(sc, w_ref[0].astype(BF16)) + b_ref[0]


def _ada_modulation(c, w, b):
    depth, d, n = w.shape
    bsz = c.shape[0]
    tn = 1024
    return pl.pallas_call(
        _ada_kernel,
        grid=(depth, n // tn),
        in_specs=[pl.BlockSpec((bsz, d), lambda i, j: (0, 0)),
                  pl.BlockSpec((1, d, tn), lambda i, j: (i, 0, j)),
                  pl.BlockSpec((1, 1, tn), lambda i, j: (i, 0, j))],
        out_specs=pl.BlockSpec((1, bsz, tn), lambda i, j: (i, 0, j)),
        out_shape=jax.ShapeDtypeStruct((depth, bsz, n), F32),
        compiler_params=_cparams(("parallel", "parallel")),
        name="ada_modulation",
    )(c, w, b.reshape(depth, 1, n))


def _norm_mm_kernel(x_ref, g_ref, sc_ref, sh_ref, w_ref, o_ref, hn_ref):
    @pl.when(pl.program_id(2) == 0)
    def _():
        _store_modulated_norm(x_ref, g_ref, sc_ref, sh_ref, hn_ref)

    o_ref[0] = _dot(hn_ref[...], w_ref[...]).astype(o_ref.dtype)


def _in_proj0_kernel(x_ref, g_ref, sc_ref, sh_ref, w_ref, wdt_ref, wut_ref, o_ref, dt_ref, ut_ref, hn_ref,
                     *, n_main):
    n = pl.program_id(2)

    @pl.when(n == 0)
    def _():
        _store_modulated_norm(x_ref, g_ref, sc_ref, sh_ref, hn_ref)
        dt_ref[0] = _dot(hn_ref[...], wdt_ref[...])

    @pl.when(n < n_main)
    def _():
        o_ref[0] = _dot(hn_ref[...], w_ref[...]).astype(o_ref.dtype)

    @pl.when(n >= n_main)
    def _():
        ut_ref[...] = _dot_nt(wut_ref[...], hn_ref[...])


def _norm_matmul(x, g, scale, shift, w, layer, *, tm=1024, tn=512, name):
    bsz, l, d = x.shape
    n = w.shape[2]
    tm = min(tm, l)
    return pl.pallas_call(
        _norm_mm_kernel,
        grid=(bsz, l // tm, n // tn),
        in_specs=[pl.BlockSpec((1, tm, d), lambda b, i, j: (b, i, 0)),
                  pl.BlockSpec((1, d), lambda b, i, j: (0, 0)),
                  pl.BlockSpec((1, 1, d), lambda b, i, j: (b, 0, 0)),
                  pl.BlockSpec((1, 1, d), lambda b, i, j: (b, 0, 0)),
                  pl.BlockSpec((None, d, tn), lambda b, i, j: (layer, 0, j))],
        out_specs=pl.BlockSpec((1, tm, tn), lambda b, i, j: (b, i, j)),
        out_shape=jax.ShapeDtypeStruct((bsz, l, n), BF16),
        scratch_shapes=[pltpu.VMEM((tm, d), BF16)],
        compiler_params=_cparams(("parallel", "parallel", "arbitrary")),
        name=name,
    )(x, g.reshape(1, d), scale, shift, w)


def _in_proj0(x, g, scale, shift, w, layer, n, w_dt, w_ut, *, tm=1024, tn=512, tc=512):
    bsz, l, d = x.shape
    ndt, nut = w_dt.shape[1], w_ut.shape[0]
    tm = min(tm, l)
    n_main, n_ut, nt = n // tn, nut // tc, l // tm
    return pl.pallas_call(
        functools.partial(_in_proj0_kernel, n_main=n_main),
        grid=(bsz, nt, n_main + n_ut),
        in_specs=[pl.BlockSpec((1, tm, d), lambda b, i, j: (b, i, 0)),
                  pl.BlockSpec((1, d), lambda b, i, j: (0, 0)),
                  pl.BlockSpec((1, 1, d), lambda b, i, j: (b, 0, 0)),
                  pl.BlockSpec((1, 1, d), lambda b, i, j: (b, 0, 0)),
                  pl.BlockSpec((None, d, tn), lambda b, i, j: (layer, 0, jnp.minimum(j, n_main - 1))),
                  pl.BlockSpec((d, ndt), lambda b, i, j: (0, 0)),
                  pl.BlockSpec((tc, d), lambda b, i, j: (jnp.clip(j - n_main, 0, n_ut - 1), 0))],
        out_specs=[pl.BlockSpec((1, tm, tn), lambda b, i, j: (b, i, jnp.minimum(j, n_main - 1))),
                   pl.BlockSpec((1, tm, ndt), lambda b, i, j: (b, i, 0)),
                   pl.BlockSpec((tc, tm), lambda b, i, j: (jnp.clip(j - n_main, 0, n_ut - 1), b * nt + i))],
        out_shape=[jax.ShapeDtypeStruct((bsz, l, n), BF16),
                   jax.ShapeDtypeStruct((bsz, l, ndt), F32),
                   jax.ShapeDtypeStruct((nut, bsz * l), F32)],
        scratch_shapes=[pltpu.VMEM((tm, d), BF16)],
        compiler_params=_cparams(("parallel", "parallel", "arbitrary")),
        name="in_proj0",
    )(x, g.reshape(1, d), scale, shift, w, w_dt, w_ut)


def _mm_res_kernel(*refs, bounds):
    na = len(bounds) - 1
    a_refs = refs[:na]
    w_ref, x_ref, gate_ref, o_ref, acc_ref = refs[na:]
    k = pl.program_id(2)

    @pl.when(k == 0)
    def _():
        acc_ref[...] = jnp.zeros_like(acc_ref)

    for i, a_ref in enumerate(a_refs):
        @pl.when((k >= bounds[i]) & (k < bounds[i + 1]))
        def _(a_ref=a_ref):
            acc_ref[...] += _dot(a_ref[0], w_ref[...])

    @pl.when(k == bounds[-1] - 1)
    def _():
        o_ref[0] = x_ref[0] + gate_ref[0] * acc_ref[...]


def _matmul_residual(a_list, w, x, gate, *, tm=512, tk=1024, name):
    bsz, l, d = x.shape
    kk, n = w.shape
    tm = min(tm, l)
    bounds = [0]
    for a in a_list:
        bounds.append(bounds[-1] + a.shape[2] // tk)
    assert bounds[-1] * tk == kk

    def a_spec(lo, hi):
        return pl.BlockSpec((1, tm, tk), lambda b, i, k: (b, i, jnp.clip(k - lo, 0, hi - lo - 1)))

    return pl.pallas_call(
        functools.partial(_mm_res_kernel, bounds=tuple(bounds)),
        grid=(bsz, l // tm, bounds[-1]),
        in_specs=[a_spec(bounds[i], bounds[i + 1]) for i in range(len(a_list))] + [
            pl.BlockSpec((tk, n), lambda b, i, k: (k, 0)),
            pl.BlockSpec((1, tm, d), lambda b, i, k: (b, i, 0)),
            pl.BlockSpec((1, 1, d), lambda b, i, k: (b, 0, 0))],
        out_specs=pl.BlockSpec((1, tm, n), lambda b, i, k: (b, i, 0)),
        out_shape=jax.ShapeDtypeStruct((bsz, l, n), F32),
        scratch_shapes=[pltpu.VMEM((tm, n), F32)],
        compiler_params=_cparams(("parallel", "parallel", "arbitrary")),
        name=name,
    )(*a_list, w, x, gate)


def _ffn_kernel(x_ref, g_ref, sc_ref, sh_ref, gate_ref, wg_ref, wu_ref, wo_ref, fg_ref,
                o_ref, hn_ref, *, nh, final_norm):
    h = pl.program_id(2)

    @pl.when(h == 0)
    def _():
        _store_modulated_norm(x_ref, g_ref, sc_ref, sh_ref, hn_ref)
        o_ref[...] = jnp.zeros_like(o_ref)

    hn = hn_ref[...]
    gt = _dot(hn, wg_ref[...])
    up = _dot(hn, wu_ref[...])
    act = (_silu(gt) * up).astype(BF16)
    o_ref[0] += _dot(act, wo_ref[...])

    @pl.when(h == nh - 1)
    def _():
        gate = gate_ref[0]

        def body(i, carry):
            rows = pl.ds(pl.multiple_of(i * NORM_ROWS, NORM_ROWS), NORM_ROWS)
            y = x_ref[0, rows, :] + gate * o_ref[0, rows, :]
            if final_norm:
                ms = jnp.mean(y * y, axis=-1, keepdims=True)
                y = y * lax.rsqrt(ms + EPS) * fg_ref[...]
            o_ref[0, rows, :] = y
            return carry

        lax.fori_loop(0, hn_ref.shape[0] // NORM_ROWS, body, 0, unroll=4)


def _ffn(x, g, scale, shift, gate, w_in, w_out, layer, final_g, *, final_norm, tm=1024, th=512, name):
    bsz, l, d = x.shape
    hid = w_out.shape[1]
    tm = min(tm, l)
    nh = hid // th
    vec = pl.BlockSpec((1, 1, d), lambda b, i, h: (b, 0, 0))
    return pl.pallas_call(
        functools.partial(_ffn_kernel, nh=nh, final_norm=final_norm),
        grid=(bsz, l // tm, nh),
        in_specs=[pl.BlockSpec((1, tm, d), lambda b, i, h: (b, i, 0)),
                  pl.BlockSpec((1, d), lambda b, i, h: (0, 0)),
                  vec, vec, vec,
                  pl.BlockSpec((None, d, th), lambda b, i, h: (layer, 0, h)),
                  pl.BlockSpec((None, d, th), lambda b, i, h: (layer, 0, h + nh)),
                  pl.BlockSpec((None, th, d), lambda b, i, h: (layer, h, 0)),
                  pl.BlockSpec((1, d), lambda b, i, h: (0, 0))],
        out_specs=pl.BlockSpec((1, tm, d), lambda b, i, h: (b, i, 0)),
        out_shape=jax.ShapeDtypeStruct((bsz, l, d), F32),
        scratch_shapes=[pltpu.VMEM((tm, d), BF16)],
        compiler_params=_cparams(("parallel", "parallel", "arbitrary")),
        name=name,
    )(x, g.reshape(1, d), scale, shift, gate, w_in, w_in, w_out, final_g.reshape(1, d))


def _conv_silu(raw_ref, pad_ref, w_ref, b_ref, out_ref, first, lc, width, cb=256):
    @pl.when(first)
    def _():
        pad_ref[0:8, :] = jnp.zeros((8, width), F32)

    pad_ref[8:8 + lc, :] = raw_ref[0].astype(F32)
    ntap = SSD_CONV
    for c0 in range(0, width, cb):
        cs = slice(c0, c0 + cb)
        blk = pad_ref[:, cs]
        acc = b_ref[:, cs] + w_ref[ntap - 1:ntap, cs] * blk[8:]
        for j in range(1, ntap):
            acc = acc + w_ref[ntap - 1 - j:ntap - j, cs] * pltpu.roll(blk, j, 0)[8:]
        out_ref[:, cs] = _silu(acc)
    pad_ref[0:8, :] = pad_ref[lc:lc + 8, :]


def _ssd_kernel(z_ref, xs_ref, bm_ref, cm_ref, dt_ref,
                cwx_ref, cwb_ref, cwc_ref, cbx_ref, cbb_ref, cbc_ref,
                dtb_ref, alog_ref, dsk_ref, ng_ref, e_ref, tri_ref,
                o_ref,
                padx_s, padb_s, padc_s, xs_s, bm_s, cm_s, rhs_s, st_s, *, lc):
    first = pl.program_id(1) == 0

    @pl.when(first)
    def _():
        st_s[...] = jnp.zeros_like(st_s)

    _conv_silu(xs_ref, padx_s, cwx_ref, cbx_ref, xs_s, first, lc, SSD_INNER)
    _conv_silu(bm_ref, padb_s, cwb_ref, cbb_ref, bm_s, first, lc, SSD_BC)
    _conv_silu(cm_ref, padc_s, cwc_ref, cbc_ref, cm_s, first, lc, SSD_BC)

    rhs_s[0:lc, :] = xs_s[...].astype(BF16)
    rhs_s[lc:lc + SSD_STATE, :] = st_s[...].astype(BF16)

    v = dt_ref[0] + dtb_ref[...]
    dt = jnp.maximum(v, 0.0) + jnp.log(1.0 + jnp.exp(-jnp.abs(v)))
    a = -jnp.exp(alog_ref[...])
    acum = _sel_left(tri_ref[...], dt * a)
    acum_last = acum[lc - 1:lc, :]
    wdec = jnp.exp(acum_last - acum) * dt
    eacum = jnp.exp(acum)
    acum_t = acum.T
    dt_t = dt.T
    wexp = _sel_right(wdec, e_ref[...])
    eal = _sel_right(jnp.broadcast_to(jnp.exp(acum_last), (8, LANES)), e_ref[...])[0:1, :]

    row = lax.broadcasted_iota(jnp.int32, (lc, lc), 0)
    col = lax.broadcasted_iota(jnp.int32, (lc, lc), 1)
    causal = row >= col
    lane = lax.broadcasted_iota(jnp.int32, (lc, LANES), 1)
    low_half = lane < SSD_HEADDIM

    for g in range(SSD_GROUPS):
        gs = slice(g * SSD_STATE, (g + 1) * SSD_STATE)
        bm_g = bm_s[:, gs]
        cm_g = cm_s[:, gs]
        bm_gb = bm_g.astype(BF16)
        scores = _dot_nt(cm_g.astype(BF16), bm_gb)
        pieces = []
        for j in range(SSD_HPG // 2):
            ps = slice(g * SSD_GW + j * LANES, g * SSD_GW + (j + 1) * LANES)
            rhs = rhs_s[:, ps]
            ys = []
            for hh in range(2):
                h = g * SSD_HPG + 2 * j + hh
                seg = acum[:, h:h + 1] - acum_t[h:h + 1, :]
                lmat = jnp.exp(jnp.where(causal, seg, -1e30))
                amat = scores * lmat * dt_t[h:h + 1, :]
                cs = cm_g * eacum[:, h:h + 1]
                lhs = jnp.concatenate([amat, cs], axis=1).astype(BF16)
                ys.append(_dot(lhs, rhs))
            pieces.append(jnp.where(low_half, ys[0], ys[1]))
        y = jnp.concatenate(pieces, axis=1)

        ws = slice(g * SSD_GW, (g + 1) * SSD_GW)
        xs_g = xs_s[:, ws]
        xw = (xs_g * wexp[:, ws]).astype(BF16)
        st_s[:, ws] = st_s[:, ws] * eal[:, ws] + _dot_tn(bm_gb, xw)

        y = y + dsk_ref[:, ws] * xs_g
        y = y * _silu(z_ref[0, :, ws].astype(F32))
        ms = jnp.mean(y * y, axis=-1, keepdims=True)
        o_ref[0, :, ws] = (y * lax.rsqrt(ms + EPS) * ng_ref[:, ws]).astype(o_ref.dtype)


def _ssd(proj, dtp, conv_w, conv_b, dt_bias, a_log, d_skip, norm_g):
    bsz, l, _ = proj.shape
    lc = min(SSD_LC, l)
    i0 = SSD_INNER
    cwx, cwb, cwc = conv_w[:, :i0], conv_w[:, i0:i0 + SSD_BC], conv_w[:, i0 + SSD_BC:]
    cb = conv_b.reshape(1, -1)
    cbx, cbb, cbc = cb[:, :i0], cb[:, i0:i0 + SSD_BC], cb[:, i0 + SSD_BC:]
    pad = LANES - SSD_HEADS
    dtb = jnp.pad(dt_bias, (0, pad)).reshape(1, LANES)
    alog = jnp.pad(a_log, (0, pad)).reshape(1, LANES)
    dsk = jnp.repeat(d_skip, SSD_HEADDIM).reshape(1, i0)
    head_of_lane = np.arange(i0) // SSD_HEADDIM
    expand = jnp.asarray(np.arange(LANES)[:, None] == head_of_lane[None, :], BF16)
    tri = jnp.asarray(np.tril(np.ones((lc, lc))), BF16)

    def full(shape):
        return pl.BlockSpec(shape, lambda b, k: (0,) * len(shape))

    nb3 = i0 // SSD_BC
    return pl.pallas_call(
        functools.partial(_ssd_kernel, lc=lc),
        grid=(bsz, l // lc),
        in_specs=[pl.BlockSpec((1, lc, i0), lambda b, k: (b, k, 0)),
                  pl.BlockSpec((1, lc, i0), lambda b, k: (b, k, 1)),
                  pl.BlockSpec((1, lc, SSD_BC), lambda b, k: (b, k, 2 * nb3)),
                  pl.BlockSpec((1, lc, SSD_BC), lambda b, k: (b, k, 2 * nb3 + 1)),
                  pl.BlockSpec((1, lc, LANES), lambda b, k: (b, k, 0)),
                  full((SSD_CONV, i0)), full((SSD_CONV, SSD_BC)), full((SSD_CONV, SSD_BC)),
                  full((1, i0)), full((1, SSD_BC)), full((1, SSD_BC)),
                  full((1, LANES)), full((1, LANES)), full((1, i0)), full((1, i0)),
                  full((LANES, i0)), full((lc, lc))],
        out_specs=pl.BlockSpec((1, lc, i0), lambda b, k: (b, k, 0)),
        out_shape=jax.ShapeDtypeStruct((bsz, l, i0), BF16),
        scratch_shapes=[pltpu.VMEM((lc + 8, i0), F32), pltpu.VMEM((lc + 8, SSD_BC), F32),
                        pltpu.VMEM((lc + 8, SSD_BC), F32),
                        pltpu.VMEM((lc, i0), F32), pltpu.VMEM((lc, SSD_BC), F32),
                        pltpu.VMEM((lc, SSD_BC), F32),
                        pltpu.VMEM((lc + SSD_STATE, i0), BF16), pltpu.VMEM((SSD_STATE, i0), F32)],
        compiler_params=_cparams(("parallel", "arbitrary")),
        name="ssd_scan",
    )(proj, proj, proj, proj, dtp, cwx, cwb, cwc, cbx, cbb, cbc, dtb, alog, dsk,
      norm_g.reshape(1, i0), expand, tri)


def _cplx_mul(ar, ai, br, bi):
    return ar * br - ai * bi, ar * bi + ai * br


def _zoh_coef(are, aim, delta):
    er = jnp.exp(are * delta)
    br = er * jnp.cos(aim * delta) - 1.0
    bi = er * jnp.sin(aim * delta)
    den = are * are + aim * aim
    return (br * are + bi * aim) / den, (bi * are - br * aim) / den


def _lam_pow(are, aim, delta, n):
    mag = jnp.exp((are * delta) * n)
    ang = (aim * delta) * n
    return mag * jnp.cos(ang), mag * jnp.sin(ang)


def _s5_kernel(u_ref, arc_ref, aic_ref, arr_ref, air_ref, ldt_ref,
               bT_re_ref, bT_im_ref, b_re_ref, b_im_ref, ct_re_ref, ct_im_ref, d_ref,
               o_ref,
               m_s, r0_s, s_re_s, s_im_s, xp_re_s, xp_im_s, u_s, y_s, *, nb, nchunks):
    hi = lax.Precision.HIGHEST
    nch, lc = S5_GROUP_CH, S5_CHUNK
    nrows = nb * nchunks
    pitch = u_s.shape[0] // nch

    def chunk_row(r):
        b, k = divmod(r, nchunks)
        return k * nb + b

    for r in range(nrows):
        u_s[pl.ds(chunk_row(r), nch, stride=pitch), :] = u_ref[:, r * lc:(r + 1) * lc]

    def u_tile(c):
        return u_s[c * pitch:c * pitch + nrows, :]
    delta = jnp.exp(ldt_ref[0])
    arc, aic = arc_ref[0], aic_ref[0]
    arr, air = arr_ref[0], air_ref[0]
    tau = lax.broadcasted_iota(jnp.int32, (1, lc), 1).astype(F32)
    p_re, p_im = _lam_pow(arc, aic, delta, tau)
    q_re, q_im = _lam_pow(arc, aic, delta, (lc - 1.0) - tau)
    lb_re, lb_im = _lam_pow(arc, aic, delta, 1.0)
    cf_re_c, cf_im_c = _zoh_coef(arc, aic, delta)
    cf_re_r, cf_im_r = _zoh_coef(arr, air, delta)
    ct_re, ct_im = ct_re_ref[0], ct_im_ref[0]

    es = [_cplx_mul(ct_re[:, c:c + 1], ct_im[:, c:c + 1], p_re, p_im) for c in range(nch)]
    e_re = jnp.concatenate([e[0] for e in es], axis=1)
    e_im = jnp.concatenate([e[1] for e in es], axis=1)
    bbT_re, bbT_im = _cplx_mul(cf_re_r, cf_im_r, bT_re_ref[0], bT_im_ref[0])
    r0_s[...] = (jnp.dot(bbT_re, e_re, precision=hi, preferred_element_type=F32)
                 - jnp.dot(bbT_im, e_im, precision=hi, preferred_element_type=F32))

    srow = lax.broadcasted_iota(jnp.int32, (lc, lc), 0)
    tcol = lax.broadcasted_iota(jnp.int32, (lc, lc), 1)
    causal = tcol >= srow

    def fill(cp, carry):
        row = r0_s[pl.ds(cp, 1), :]
        r0 = pl.multiple_of(cp * lc, lc)
        for c in range(nch):
            blk = jnp.broadcast_to(row[:, c * lc:(c + 1) * lc], (lc, lc))
            blk = pltpu.roll(blk, 0, 1, stride=1, stride_axis=0)
            m_s[pl.ds(r0, lc), c * lc:(c + 1) * lc] = jnp.where(causal, blk, 0.0).astype(BF16)
        return carry

    lax.fori_loop(0, nch, fill, 0)

    bb_re, bb_im = _cplx_mul(cf_re_c, cf_im_c, b_re_ref[0], b_im_ref[0])
    ws = [_cplx_mul(q_re, q_im, bb_re[:, c:c + 1], bb_im[:, c:c + 1]) for c in range(nch)]
    wt_re = jnp.concatenate([w[0] for w in ws], axis=1).astype(BF16)
    wt_im = jnp.concatenate([w[1] for w in ws], axis=1).astype(BF16)
    ub = jnp.concatenate([u_tile(c) for c in range(nch)], axis=1).astype(BF16)
    s_re_s[...] = _dot_nt(ub, wt_re)
    s_im_s[...] = _dot_nt(ub, wt_im)

    c_re, c_im = _lam_pow(arr, air, delta, float(lc))
    xr = jnp.zeros((nb, S5_STATE), F32)
    xi = jnp.zeros((nb, S5_STATE), F32)
    for k in range(nchunks):
        rows = slice(k * nb, (k + 1) * nb)
        xp_re_s[rows, :] = xr
        xp_im_s[rows, :] = xi
        xr, xi = (xr * c_re - xi * c_im + s_re_s[rows, :],
                  xr * c_im + xi * c_re + s_im_s[rows, :])

    e1_re, e1_im = _cplx_mul(e_re, e_im, lb_re, lb_im)
    y = _dot(ub, m_s[...])
    y = y + _dot(xp_re_s[...].astype(BF16), e1_re.astype(BF16))
    y = y - _dot(xp_im_s[...].astype(BF16), e1_im.astype(BF16))
    for c in range(nch):
        yc = y[:, c * lc:(c + 1) * lc] + d_ref[0, c:c + 1, :] * u_tile(c)
        y_s[c * pitch:c * pitch + nrows, :] = (
            0.5 * yc * (1.0 + jnp.tanh(math.sqrt(2.0 / math.pi) * (yc + 0.044715 * (yc * yc * yc)))))
    for r in range(nrows):
        o_ref[:, r * lc:(r + 1) * lc] = y_s[pl.ds(chunk_row(r), nch, stride=pitch), :]


def _s5(ut, bsz, a_re, a_im, log_dt, b_re, b_im, c_re, c_im, d):
    width, t = ut.shape
    l = t // bsz
    g, p, ch, lc = S5_GROUPS, S5_STATE, S5_GROUP_CH, S5_CHUNK
    nchunks = l // lc
    nrows = nchunks * bsz
    pitch = nrows + 8

    def per_group(shape):
        return pl.BlockSpec((1,) + shape, lambda i: (i, 0, 0))

    return pl.pallas_call(
        functools.partial(_s5_kernel, nb=bsz, nchunks=nchunks),
        grid=(g,),
        in_specs=[pl.BlockSpec((ch, t), lambda i: (i, 0)),
                  per_group((p, 1)), per_group((p, 1)), per_group((1, p)), per_group((1, p)),
                  per_group((1, 1)),
                  per_group((ch, p)), per_group((ch, p)),
                  per_group((p, ch)), per_group((p, ch)),
                  per_group((p, ch)), per_group((p, ch)),
                  per_group((ch, 1))],
        out_specs=pl.BlockSpec((ch, t), lambda i: (i, 0)),
        out_shape=jax.ShapeDtypeStruct((width, t), F32),
        scratch_shapes=[pltpu.VMEM((ch * lc, ch * lc), BF16), pltpu.VMEM((ch, ch * lc), F32),
                        pltpu.VMEM((nrows, p), F32), pltpu.VMEM((nrows, p), F32),
                        pltpu.VMEM((nrows, p), F32), pltpu.VMEM((nrows, p), F32),
                        pltpu.VMEM((ch * pitch, lc), F32), pltpu.VMEM((ch * pitch, lc), F32)],
        compiler_params=_cparams(("parallel",)),
        name="s5_ssm",
    )(ut,
      a_re.reshape(g, p, 1), a_im.reshape(g, p, 1), a_re.reshape(g, 1, p), a_im.reshape(g, 1, p),
      log_dt.reshape(g, 1, 1),
      b_re.transpose(0, 2, 1), b_im.transpose(0, 2, 1), b_re, b_im,
      c_re.transpose(0, 2, 1), c_im.transpose(0, 2, 1),
      d.reshape(g, ch, 1))


def _glu_kernel(yt_ref, w_ref, b_ref, o_ref):
    y = yt_ref[...].T
    gate = jax.nn.sigmoid(_dot(y.astype(BF16), w_ref[...]) + b_ref[...])
    o_ref[0] = (y * gate).astype(o_ref.dtype)


def _glu(yt, bsz, w, b, *, tm=512):
    n, t = yt.shape
    l = t // bsz
    tm = min(tm, l)
    nt = l // tm
    return pl.pallas_call(
        _glu_kernel,
        grid=(bsz, nt),
        in_specs=[pl.BlockSpec((n, tm), lambda b_, i: (0, b_ * nt + i)),
                  pl.BlockSpec((n, n), lambda b_, i: (0, 0)),
                  pl.BlockSpec((1, n), lambda b_, i: (0, 0))],
        out_specs=pl.BlockSpec((1, tm, n), lambda b_, i: (b_, i, 0)),
        out_shape=jax.ShapeDtypeStruct((bsz, l, n), BF16),
        compiler_params=_cparams(("parallel", "parallel")),
        name="s5_glu",
    )(yt, w, b.reshape(1, n))


def _ret_kernel(q_ref, k_ref, v_ref, g_ref, lg_ref, invf_ref, gn_ref, o_ref,
                r_s, d_s, cb_s, sb_s, *, tb):
    j = pl.program_id(1)
    idx = lax.broadcasted_iota(jnp.int32, (tb, 1), 0).astype(F32)

    @pl.when(j == 0)
    def _():
        r_s[...] = jnp.zeros_like(r_s)
        ri = lax.broadcasted_iota(jnp.int32, (tb, tb), 0)
        ci = lax.broadcasted_iota(jnp.int32, (tb, tb), 1)
        rc, cc = ri // RET_CHUNK, ci // RET_CHUNK
        dist = jnp.where(rc == cc, jnp.abs(ri - ci), jnp.maximum(ri - ci, 0)).astype(F32)
        for h in range(RET_HEADS):
            d_s[h] = jnp.where(cc > rc, 0.0, jnp.exp(lg_ref[h] * dist) * (RET_QK ** -0.5))
        ang = idx * invf_ref[...]
        cb_s[...] = jnp.cos(ang)
        sb_s[...] = jnp.sin(ang)

    base = (j * tb).astype(F32) * invf_ref[...]
    cos_a, sin_a = jnp.cos(base), jnp.sin(base)
    cos = cos_a * cb_s[...] - sin_a * sb_s[...]
    sin = sin_a * cb_s[...] + cos_a * sb_s[...]
    half = RET_QK // 2

    def rot(t):
        t1, t2 = t[:, :half], t[:, half:]
        return jnp.concatenate([t1 * cos - t2 * sin, t1 * sin + t2 * cos], axis=1)

    for h in range(RET_HEADS):
        lg = lg_ref[h]
        qs = slice(h * RET_QK, (h + 1) * RET_QK)
        vs = slice(h * RET_V, (h + 1) * RET_V)
        q = rot(q_ref[0, :, qs].astype(F32))
        k = rot(k_ref[0, :, qs].astype(F32))
        vb = v_ref[0, :, vs]
        xi = jnp.exp(lg * (idx + 1.0))
        zeta = jnp.exp(lg * (tb - 1.0 - idx)) * (RET_QK ** -0.5)
        gblk = jnp.exp(lg * float(tb))

        r = r_s[h]
        s = _dot_nt(q.astype(BF16), k.astype(BF16)) * d_s[h]
        o = _dot(s.astype(BF16), vb) + _dot((q * xi).astype(BF16), r.astype(BF16))
        r_s[h] = r * gblk + _dot_tn((k * zeta).astype(BF16), vb)

        mu = jnp.mean(o, axis=-1, keepdims=True)
        oc = o - mu
        y = oc * lax.rsqrt(jnp.mean(oc * oc, axis=-1, keepdims=True) + EPS)
        o_ref[0, :, vs] = (y * gn_ref[:, vs] * _silu(g_ref[0, :, vs].astype(F32))).astype(o_ref.dtype)


def _retention(proj, gn_g):
    bsz, l, _ = proj.shape
    tb = min(RET_BLOCK, l)
    hq = RET_HEADS
    dq, dv = hq * RET_QK, hq * RET_V
    log_g = jnp.log(1.0 - 2.0 ** (-5.0 - jnp.arange(hq, dtype=F32))).reshape(hq, 1, 1)
    inv_freq = (ROPE_BASE ** (-jnp.arange(0, RET_QK, 2, dtype=F32) / RET_QK)).reshape(1, RET_QK // 2)
    return pl.pallas_call(
        functools.partial(_ret_kernel, tb=tb),
        grid=(bsz, l // tb),
        in_specs=[pl.BlockSpec((1, tb, dq), lambda b, j: (b, j, 0)),
                  pl.BlockSpec((1, tb, dq), lambda b, j: (b, j, 1)),
                  pl.BlockSpec((1, tb, dv), lambda b, j: (b, j, 1)),
                  pl.BlockSpec((1, tb, dv), lambda b, j: (b, j, 2)),
                  pl.BlockSpec((hq, 1, 1), lambda b, j: (0, 0, 0)),
                  pl.BlockSpec((1, RET_QK // 2), lambda b, j: (0, 0)),
                  pl.BlockSpec((1, dv), lambda b, j: (0, 0))],
        out_specs=pl.BlockSpec((1, tb, dv), lambda b, j: (b, j, 0)),
        out_shape=jax.ShapeDtypeStruct((bsz, l, dv), BF16),
        scratch_shapes=[pltpu.VMEM((hq, RET_QK, RET_V), F32), pltpu.VMEM((hq, tb, tb), F32),
                        pltpu.VMEM((tb, RET_QK // 2), F32), pltpu.VMEM((tb, RET_QK // 2), F32)],
        compiler_params=_cparams(("parallel", "arbitrary")),
        name="retention",
    )(proj, proj, proj, proj, log_g, inv_freq, gn_g.reshape(1, -1))


def _split_mod(m, bsz):
    d = m.shape[-1] // 3
    return tuple(m[:, k * d:(k + 1) * d].reshape(bsz, 1, d) for k in range(3))


def kernel(x, c, norm_mix_g, ada_mix_w, ada_mix_b, norm_ffn_g, ada_ffn_w, ada_ffn_b, ffn_w_in, ffn_w_out, ab_w_in, ssd_conv_w, ssd_conv_b, ssd_dt_bias, ssd_a_log, ssd_d, ssd_norm_g, s5_a_re, s5_a_im, s5_log_dt, s5_b_re, s5_b_im, s5_c_re, s5_c_im, s5_d, s5_glu_w, s5_glu_b, ab_w_out, ret_w_in, ret_gn_g, ret_w_out, final_norm_g):
    bsz = x.shape[0]
    mod_mix = _ada_modulation(c, ada_mix_w, ada_mix_b)
    mod_ffn = _ada_modulation(c, ada_ffn_w, ada_ffn_b)

    shift, scale, gate = _split_mod(mod_mix[0], bsz)
    w0 = ab_w_in[0]
    n_main = SSD_INNER + SSD_XBC
    w_dt = jnp.pad(w0[:, n_main:n_main + SSD_HEADS], ((0, 0), (0, LANES - SSD_HEADS))).astype(BF16)
    w_ut = w0[:, n_main + SSD_HEADS:].T.astype(BF16)
    proj0, dtp, ut = _in_proj0(x, norm_mix_g[0], scale, shift, ab_w_in.astype(BF16), 0, n_main, w_dt, w_ut)
    y_a = _ssd(proj0, dtp, ssd_conv_w[0], ssd_conv_b[0], ssd_dt_bias[0], ssd_a_log[0], ssd_d[0],
               ssd_norm_g[0])
    yt = _s5(ut, bsz, s5_a_re[0], s5_a_im[0], s5_log_dt[0], s5_b_re[0], s5_b_im[0],
             s5_c_re[0], s5_c_im[0], s5_d[0])
    y_b = _glu(yt, bsz, s5_glu_w[0].astype(BF16), s5_glu_b[0])
    x = _matmul_residual([y_a, y_b], ab_w_out[0].astype(BF16), x, gate, name="out_proj0")

    shift, scale, gate = _split_mod(mod_ffn[0], bsz)
    ffn_w_in_b, ffn_w_out_b = ffn_w_in.astype(BF16), ffn_w_out.astype(BF16)
    x = _ffn(x, norm_ffn_g[0], scale, shift, gate, ffn_w_in_b, ffn_w_out_b, 0, final_norm_g,
             final_norm=False, name="ffn0")

    shift, scale, gate = _split_mod(mod_mix[1], bsz)
    proj1 = _norm_matmul(x, norm_mix_g[1], scale, shift, ret_w_in.astype(BF16), 0, name="in_proj1")
    y_r = _retention(proj1, ret_gn_g[0])
    x = _matmul_residual([y_r], ret_w_out[0].astype(BF16), x, gate, name="out_proj1")

    shift, scale, gate = _split_mod(mod_ffn[1], bsz)
    return _ffn(x, norm_ffn_g[1], scale, shift, gate, ffn_w_in_b, ffn_w_out_b, 1, final_norm_g,
                final_norm=True, name="ffn1")
```

```python
import functools
import math

import numpy as np
import jax
import jax.numpy as jnp
from jax import lax
from jax.experimental import pallas as pl
from jax.experimental.pallas import tpu as pltpu

F32 = jnp.float32
BF16 = jnp.bfloat16
EPS = 1e-6

D_MODEL = 2048
MIX_WIDTH = 2 * D_MODEL
SSD_HEADDIM = 64
SSD_INNER = 3 * MIX_WIDTH // 4
SSD_HEADS = SSD_INNER // SSD_HEADDIM
SSD_GROUPS = 8
SSD_HPG = SSD_HEADS // SSD_GROUPS
SSD_STATE = 128
SSD_CONV = 4
SSD_GW = SSD_HPG * SSD_HEADDIM
SSD_BC = SSD_GROUPS * SSD_STATE
SSD_XBC = SSD_INNER + 2 * SSD_BC
S5_WIDTH = MIX_WIDTH - SSD_INNER
S5_GROUP_CH = 16
S5_GROUPS = S5_WIDTH // S5_GROUP_CH
S5_STATE = 64
S5_CHUNK = 128
RET_HEADS = 8
RET_QK = D_MODEL // RET_HEADS
RET_V = MIX_WIDTH // RET_HEADS
RET_CHUNK = 64
RET_BLOCK = 256
ROPE_BASE = 10000.0
FFN_HIDDEN = ((-(-8 * D_MODEL // 3) + 255) // 256) * 256

LANES = 128
SSD_LC = 128
VMEM_LIMIT = 56 * 1024 * 1024


def _cparams(sem):
    return pltpu.CompilerParams(dimension_semantics=sem, vmem_limit_bytes=VMEM_LIMIT)


def _dot(a, b):
    return jnp.dot(a, b, preferred_element_type=F32)


def _dot_nt(a, b):
    return lax.dot_general(a, b, (((1,), (1,)), ((), ())), preferred_element_type=F32)


def _dot_tn(a, b):
    return lax.dot_general(a, b, (((0,), (0,)), ((), ())), preferred_element_type=F32)


def _split3(v):
    hi = v.astype(BF16)
    r = v - hi.astype(F32)
    mid = r.astype(BF16)
    lo = (r - mid.astype(F32)).astype(BF16)
    return hi, mid, lo


def _sel_right(v, e):
    hi, mid, lo = _split3(v)
    return (_dot(lo, e) + _dot(mid, e)) + _dot(hi, e)


def _sel_left(e, v):
    hi, mid, lo = _split3(v)
    return (_dot(e, lo) + _dot(e, mid)) + _dot(e, hi)


def _silu(v):
    h = 0.5 * v
    return h + h * jnp.tanh(h)


NORM_ROWS = 32


def _store_modulated_norm(x_ref, g_ref, sc_ref, sh_ref, hn_ref):
    gm = g_ref[...] * (1.0 + sc_ref[0])
    sh = sh_ref[0]

    def body(i, carry):
        r0 = pl.multiple_of(i * NORM_ROWS, NORM_ROWS)
        x = x_ref[0, pl.ds(r0, NORM_ROWS), :]
        ms = jnp.mean(x * x, axis=-1, keepdims=True)
        hn_ref[pl.ds(r0, NORM_ROWS), :] = ((x * lax.rsqrt(ms + EPS)) * gm + sh).astype(BF16)
        return carry

    lax.fori_loop(0, hn_ref.shape[0] // NORM_ROWS, body, 0, unroll=4)


def _ada_kernel(c_ref, w_ref, b_ref, o_ref):
    sc = _silu(c_ref[...]).astype(BF16)
    o_ref[0] = _dot(sc, w_ref[0].astype(BF16)) + b_ref[0]


def _ada_modulation(c, w, b):
    depth, d, n = w.shape
    bsz = c.shape[0]
    tn = 1024
    return pl.pallas_call(
        _ada_kernel,
        grid=(depth, n // tn),
        in_specs=[pl.BlockSpec((bsz, d), lambda i, j: (0, 0)),
                  pl.BlockSpec((1, d, tn), lambda i, j: (i, 0, j)),
                  pl.BlockSpec((1, 1, tn), lambda i, j: (i, 0, j))],
        out_specs=pl.BlockSpec((1, bsz, tn), lambda i, j: (i, 0, j)),
        out_shape=jax.ShapeDtypeStruct((depth, bsz, n), F32),
        compiler_params=_cparams(("parallel", "parallel")),
        name="ada_modulation",
    )(c, w, b.reshape(depth, 1, n))


def _norm_mm_kernel(x_ref, g_ref, sc_ref, sh_ref, w_ref, o_ref, hn_ref):
    @pl.when(pl.program_id(2) == 0)
    def _():
        _store_modulated_norm(x_ref, g_ref, sc_ref, sh_ref, hn_ref)

    o_ref[0] = _dot(hn_ref[...], w_ref[...]).astype(o_ref.dtype)


def _in_proj0_kernel(x_ref, g_ref, sc_ref, sh_ref, w_ref, wdt_ref, wut_ref, o_ref, dt_ref, ut_ref, hn_ref,
                     *, n_main):
    n = pl.program_id(2)

    @pl.when(n == 0)
    def _():
        _store_modulated_norm(x_ref, g_ref, sc_ref, sh_ref, hn_ref)
        dt_ref[0] = _dot(hn_ref[...], wdt_ref[...])

    @pl.when(n < n_main)
    def _():
        o_ref[0] = _dot(hn_ref[...], w_ref[...]).astype(o_ref.dtype)

    @pl.when(n >= n_main)
    def _():
        ut_ref[...] = _dot_nt(wut_ref[...], hn_ref[...])


def _norm_matmul(x, g, scale, shift, w, layer, *, tm=1024, tn=1024, name):
    bsz, l, d = x.shape
    n = w.shape[2]
    tm = min(tm, l)
    return pl.pallas_call(
        _norm_mm_kernel,
        grid=(bsz, l // tm, n // tn),
        in_specs=[pl.BlockSpec((1, tm, d), lambda b, i, j: (b, i, 0)),
                  pl.BlockSpec((1, d), lambda b, i, j: (0, 0)),
                  pl.BlockSpec((1, 1, d), lambda b, i, j: (b, 0, 0)),
                  pl.BlockSpec((1, 1, d), lambda b, i, j: (b, 0, 0)),
                  pl.BlockSpec((None, d, tn), lambda b, i, j: (layer, 0, j))],
        out_specs=pl.BlockSpec((1, tm, tn), lambda b, i, j: (b, i, j)),
        out_shape=jax.ShapeDtypeStruct((bsz, l, n), BF16),
        scratch_shapes=[pltpu.VMEM((tm, d), BF16)],
        compiler_params=_cparams(("parallel", "parallel", "arbitrary")),
        name=name,
    )(x, g.reshape(1, d), scale, shift, w)


def _in_proj0(x, g, scale, shift, w, layer, n, w_dt, w_ut, *, tm=1024, tn=1024, tc=512):
    bsz, l, d = x.shape
    ndt, nut = w_dt.shape[1], w_ut.shape[0]
    tm = min(tm, l)
    n_main, n_ut, nt = n // tn, nut // tc, l // tm
    return pl.pallas_call(
        functools.partial(_in_proj0_kernel, n_main=n_main),
        grid=(bsz, nt, n_main + n_ut),
        in_specs=[pl.BlockSpec((1, tm, d), lambda b, i, j: (b, i, 0)),
                  pl.BlockSpec((1, d), lambda b, i, j: (0, 0)),
                  pl.BlockSpec((1, 1, d), lambda b, i, j: (b, 0, 0)),
                  pl.BlockSpec((1, 1, d), lambda b, i, j: (b, 0, 0)),
                  pl.BlockSpec((None, d, tn), lambda b, i, j: (layer, 0, jnp.minimum(j, n_main - 1))),
                  pl.BlockSpec((d, ndt), lambda b, i, j: (0, 0)),
                  pl.BlockSpec((tc, d), lambda b, i, j: (jnp.clip(j - n_main, 0, n_ut - 1), 0))],
        out_specs=[pl.BlockSpec((1, tm, tn), lambda b, i, j: (b, i, jnp.minimum(j, n_main - 1))),
                   pl.BlockSpec((1, tm, ndt), lambda b, i, j: (b, i, 0)),
                   pl.BlockSpec((tc, tm), lambda b, i, j: (jnp.clip(j - n_main, 0, n_ut - 1), b * nt + i))],
        out_shape=[jax.ShapeDtypeStruct((bsz, l, n), BF16),
                   jax.ShapeDtypeStruct((bsz, l, ndt), F32),
                   jax.ShapeDtypeStruct((nut, bsz * l), F32)],
        scratch_shapes=[pltpu.VMEM((tm, d), BF16)],
        compiler_params=_cparams(("parallel", "parallel", "arbitrary")),
        name="in_proj0",
    )(x, g.reshape(1, d), scale, shift, w, w_dt, w_ut)


def _mm_res_kernel(*refs, bounds):
    na = len(bounds) - 1
    a_refs = refs[:na]
    w_ref, x_ref, gate_ref, o_ref = refs[na:]
    k = pl.program_id(2)

    @pl.when(k == 0)
    def _():
        o_ref[...] = jnp.zeros_like(o_ref)

    for i, a_ref in enumerate(a_refs):
        @pl.when((k >= bounds[i]) & (k < bounds[i + 1]))
        def _(a_ref=a_ref):
            o_ref[0] += _dot(a_ref[0], w_ref[...])

    @pl.when(k == bounds[-1] - 1)
    def _():
        o_ref[0] = x_ref[0] + gate_ref[0] * o_ref[0]


def _matmul_residual(a_list, w, x, gate, *, tm=1024, tk=1024, name):
    bsz, l, d = x.shape
    kk, n = w.shape
    tm = min(tm, l)
    bounds = [0]
    for a in a_list:
        bounds.append(bounds[-1] + a.shape[2] // tk)
    assert bounds[-1] * tk == kk

    def a_spec(lo, hi):
        return pl.BlockSpec((1, tm, tk), lambda b, i, k: (b, i, jnp.clip(k - lo, 0, hi - lo - 1)))

    return pl.pallas_call(
        functools.partial(_mm_res_kernel, bounds=tuple(bounds)),
        grid=(bsz, l // tm, bounds[-1]),
        in_specs=[a_spec(bounds[i], bounds[i + 1]) for i in range(len(a_list))] + [
            pl.BlockSpec((tk, n), lambda b, i, k: (k, 0)),
            pl.BlockSpec((1, tm, d), lambda b, i, k: (b, i, 0)),
            pl.BlockSpec((1, 1, d), lambda b, i, k: (b, 0, 0))],
        out_specs=pl.BlockSpec((1, tm, n), lambda b, i, k: (b, i, 0)),
        out_shape=jax.ShapeDtypeStruct((bsz, l, n), F32),
        compiler_params=_cparams(("parallel", "parallel", "arbitrary")),
        name=name,
    )(*a_list, w, x, gate)


def _ffn_kernel(x_ref, g_ref, sc_ref, sh_ref, gate_ref, wg_ref, wu_ref, wo_ref, fg_ref,
                o_ref, hn_ref, *, nh, final_norm):
    h = pl.program_id(2)

    @pl.when(h == 0)
    def _():
        _store_modulated_norm(x_ref, g_ref, sc_ref, sh_ref, hn_ref)
        o_ref[...] = jnp.zeros_like(o_ref)

    hn = hn_ref[...]
    gt = _dot(hn, wg_ref[...])
    up = _dot(hn, wu_ref[...])
    act = (_silu(gt) * up).astype(BF16)
    o_ref[0] += _dot(act, wo_ref[...])

    @pl.when(h == nh - 1)
    def _():
        gate = gate_ref[0]

        def body(i, carry):
            rows = pl.ds(pl.multiple_of(i * NORM_ROWS, NORM_ROWS), NORM_ROWS)
            y = x_ref[0, rows, :] + gate * o_ref[0, rows, :]
            if final_norm:
                ms = jnp.mean(y * y, axis=-1, keepdims=True)
                y = y * lax.rsqrt(ms + EPS) * fg_ref[...]
            o_ref[0, rows, :] = y
            return carry

        lax.fori_loop(0, hn_ref.shape[0] // NORM_ROWS, body, 0, unroll=4)


def _ffn(x, g, scale, shift, gate, w_in, w_out, layer, final_g, *, final_norm, tm=1024, th=512, name):
    bsz, l, d = x.shape
    hid = w_out.shape[1]
    tm = min(tm, l)
    nh = hid // th
    vec = pl.BlockSpec((1, 1, d), lambda b, i, h: (b, 0, 0))
    return pl.pallas_call(
        functools.partial(_ffn_kernel, nh=nh, final_norm=final_norm),
        grid=(bsz, l // tm, nh),
        in_specs=[pl.BlockSpec((1, tm, d), lambda b, i, h: (b, i, 0)),
                  pl.BlockSpec((1, d), lambda b, i, h: (0, 0)),
                  vec, vec, vec,
                  pl.BlockSpec((None, d, th), lambda b, i, h: (layer, 0, h)),
                  pl.BlockSpec((None, d, th), lambda b, i, h: (layer, 0, h + nh)),
                  pl.BlockSpec((None, th, d), lambda b, i, h: (layer, h, 0)),
                  pl.BlockSpec((1, d), lambda b, i, h: (0, 0))],
        out_specs=pl.BlockSpec((1, tm, d), lambda b, i, h: (b, i, 0)),
        out_shape=jax.ShapeDtypeStruct((bsz, l, d), F32),
        scratch_shapes=[pltpu.VMEM((tm, d), BF16)],
        compiler_params=_cparams(("parallel", "parallel", "arbitrary")),
        name=name,
    )(x, g.reshape(1, d), scale, shift, gate, w_in, w_in, w_out, final_g.reshape(1, d))


CONV_PAD = 16


def _conv_silu(raw_ref, pad_ref, sh_ref, w_ref, b_ref, out_ref, first, lc, width, cb=512):
    @pl.when(first)
    def _():
        pad_ref[0:CONV_PAD, :] = jnp.zeros((CONV_PAD, width), BF16)

    pad_ref[CONV_PAD:CONV_PAD + lc, :] = raw_ref[0]
    ntap = SSD_CONV
    for c0 in range(0, width, cb):
        cs = slice(c0, c0 + cb)
        blk = pad_ref[:, cs]
        acc = b_ref[:, cs] + w_ref[ntap - 1:ntap, cs] * raw_ref[0, :, cs].astype(F32)
        for j in range(1, ntap):
            acc = acc + w_ref[ntap - 1 - j:ntap - j, cs] * _dot(sh_ref[j - 1], blk)
        out_ref[:, cs] = _silu(acc)
    pad_ref[0:CONV_PAD, :] = pad_ref[lc:lc + CONV_PAD, :]


def _ssd_kernel(z_ref, xs_ref, bm_ref, cm_ref, dt_ref,
                cwx_ref, cwb_ref, cwc_ref, cbx_ref, cbb_ref, cbc_ref,
                dtb_ref, alog_ref, dsk_ref, ng_ref, e_ref, tri_ref, sh_ref,
                o_ref,
                padx_s, padb_s, padc_s, xs_s, bm_s, cm_s, rhs_s, st_s, *, lc):
    first = pl.program_id(1) == 0

    @pl.when(first)
    def _():
        st_s[...] = jnp.zeros_like(st_s)

    _conv_silu(xs_ref, padx_s, sh_ref, cwx_ref, cbx_ref, xs_s, first, lc, SSD_INNER)
    _conv_silu(bm_ref, padb_s, sh_ref, cwb_ref, cbb_ref, bm_s, first, lc, SSD_BC)
    _conv_silu(cm_ref, padc_s, sh_ref, cwc_ref, cbc_ref, cm_s, first, lc, SSD_BC)

    rhs_s[0:lc, :] = xs_s[...].astype(BF16)
    rhs_s[lc:lc + SSD_STATE, :] = st_s[...].astype(BF16)

    v = dt_ref[0] + dtb_ref[...]
    dt = jnp.maximum(v, 0.0) + jnp.log(1.0 + jnp.exp(-jnp.abs(v)))
    a = -jnp.exp(alog_ref[...])
    acum = _sel_left(tri_ref[...], dt * a)
    acum_last = acum[lc - 1:lc, :]
    wdec = jnp.exp(acum_last - acum) * dt
    eacum = jnp.exp(acum)
    acum_t = acum.T
    dt_t = dt.T
    wexp = _sel_right(wdec, e_ref[...])
    eal = _sel_right(jnp.broadcast_to(jnp.exp(acum_last), (8, LANES)), e_ref[...])[0:1, :]

    row = lax.broadcasted_iota(jnp.int32, (lc, lc), 0)
    col = lax.broadcasted_iota(jnp.int32, (lc, lc), 1)
    causal = row >= col
    lane = lax.broadcasted_iota(jnp.int32, (lc, LANES), 1)
    low_half = lane < SSD_HEADDIM

    for g in range(SSD_GROUPS):
        gs = slice(g * SSD_STATE, (g + 1) * SSD_STATE)
        bm_g = bm_s[:, gs]
        cm_g = cm_s[:, gs]
        bm_gb = bm_g.astype(BF16)
        scores = _dot_nt(cm_g.astype(BF16), bm_gb)
        pieces = []
        for j in range(SSD_HPG // 2):
            ps = slice(g * SSD_GW + j * LANES, g * SSD_GW + (j + 1) * LANES)
            rhs = rhs_s[:, ps]
            ys = []
            for hh in range(2):
                h = g * SSD_HPG + 2 * j + hh
                seg = acum[:, h:h + 1] - acum_t[h:h + 1, :]
                lmat = jnp.exp(jnp.where(causal, seg, -1e30))
                amat = scores * lmat * dt_t[h:h + 1, :]
                cs = cm_g * eacum[:, h:h + 1]
                lhs = jnp.concatenate([amat, cs], axis=1).astype(BF16)
                ys.append(_dot(lhs, rhs))
            pieces.append(jnp.where(low_half, ys[0], ys[1]))
        y = jnp.concatenate(pieces, axis=1)

        ws = slice(g * SSD_GW, (g + 1) * SSD_GW)
        xs_g = xs_s[:, ws]
        xw = (xs_g * wexp[:, ws]).astype(BF16)
        st_s[:, ws] = st_s[:, ws] * eal[:, ws] + _dot_tn(bm_gb, xw)

        y = y + dsk_ref[:, ws] * xs_g
        y = y * _silu(z_ref[0, :, ws].astype(F32))
        ms = jnp.mean(y * y, axis=-1, keepdims=True)
        o_ref[0, :, ws] = (y * lax.rsqrt(ms + EPS) * ng_ref[:, ws]).astype(o_ref.dtype)


def _ssd(proj, dtp, conv_w, conv_b, dt_bias, a_log, d_skip, norm_g):
    bsz, l, _ = proj.shape
    lc = min(SSD_LC, l)
    i0 = SSD_INNER
    cwx, cwb, cwc = conv_w[:, :i0], conv_w[:, i0:i0 + SSD_BC], conv_w[:, i0 + SSD_BC:]
    cb = conv_b.reshape(1, -1)
    cbx, cbb, cbc = cb[:, :i0], cb[:, i0:i0 + SSD_BC], cb[:, i0 + SSD_BC:]
    pad = LANES - SSD_HEADS
    dtb = jnp.pad(dt_bias, (0, pad)).reshape(1, LANES)
    alog = jnp.pad(a_log, (0, pad)).reshape(1, LANES)
    dsk = jnp.repeat(d_skip, SSD_HEADDIM).reshape(1, i0)
    head_of_lane = np.arange(i0) // SSD_HEADDIM
    expand = jnp.asarray(np.arange(LANES)[:, None] == head_of_lane[None, :], BF16)
    tri = jnp.asarray(np.tril(np.ones((lc, lc))), BF16)
    tt = np.arange(lc)[:, None]
    shift = jnp.asarray(np.stack([np.arange(CONV_PAD + lc)[None, :] == CONV_PAD + tt - j
                                  for j in range(1, SSD_CONV)]), BF16)

    def full(shape):
        return pl.BlockSpec(shape, lambda b, k: (0,) * len(shape))

    nb3 = i0 // SSD_BC
    return pl.pallas_call(
        functools.partial(_ssd_kernel, lc=lc),
        grid=(bsz, l // lc),
        in_specs=[pl.BlockSpec((1, lc, i0), lambda b, k: (b, k, 0)),
                  pl.BlockSpec((1, lc, i0), lambda b, k: (b, k, 1)),
                  pl.BlockSpec((1, lc, SSD_BC), lambda b, k: (b, k, 2 * nb3)),
                  pl.BlockSpec((1, lc, SSD_BC), lambda b, k: (b, k, 2 * nb3 + 1)),
                  pl.BlockSpec((1, lc, LANES), lambda b, k: (b, k, 0)),
                  full((SSD_CONV, i0)), full((SSD_CONV, SSD_BC)), full((SSD_CONV, SSD_BC)),
                  full((1, i0)), full((1, SSD_BC)), full((1, SSD_BC)),
                  full((1, LANES)), full((1, LANES)), full((1, i0)), full((1, i0)),
                  full((LANES, i0)), full((lc, lc)), full((SSD_CONV - 1, lc, CONV_PAD + lc))],
        out_specs=pl.BlockSpec((1, lc, i0), lambda b, k: (b, k, 0)),
        out_shape=jax.ShapeDtypeStruct((bsz, l, i0), BF16),
        scratch_shapes=[pltpu.VMEM((lc + CONV_PAD, i0), BF16), pltpu.VMEM((lc + CONV_PAD, SSD_BC), BF16),
                        pltpu.VMEM((lc + CONV_PAD, SSD_BC), BF16),
                        pltpu.VMEM((lc, i0), F32), pltpu.VMEM((lc, SSD_BC), F32),
                        pltpu.VMEM((lc, SSD_BC), F32),
                        pltpu.VMEM((lc + SSD_STATE, i0), BF16), pltpu.VMEM((SSD_STATE, i0), F32)],
        compiler_params=_cparams(("parallel", "arbitrary")),
        name="ssd_scan",
    )(proj, proj, proj, proj, dtp, cwx, cwb, cwc, cbx, cbb, cbc, dtb, alog, dsk,
      norm_g.reshape(1, i0), expand, tri, shift)


def _cplx_mul(ar, ai, br, bi):
    return ar * br - ai * bi, ar * bi + ai * br


def _zoh_coef(are, aim, delta):
    er = jnp.exp(are * delta)
    br = er * jnp.cos(aim * delta) - 1.0
    bi = er * jnp.sin(aim * delta)
    den = are * are + aim * aim
    return (br * are + bi * aim) / den, (bi * are - br * aim) / den


def _lam_pow(are, aim, delta, n):
    mag = jnp.exp((are * delta) * n)
    ang = (aim * delta) * n
    return mag * jnp.cos(ang), mag * jnp.sin(ang)


def _s5_kernel(u_ref, arc_ref, aic_ref, arr_ref, air_ref, ldt_ref,
               bT_re_ref, bT_im_ref, b_re_ref, b_im_ref, ct_re_ref, ct_im_ref, d_ref,
               o_ref,
               ma_s, mb_s, yacc_s, r0_s, s_re_s, s_im_s, xp_re_s, xp_im_s, u_s, y_s, *, nb, nchunks):
    nch, lc = S5_GROUP_CH, S5_CHUNK
    nrows = nb * nchunks
    pitch = u_s.shape[0] // nch

    def chunk_row(r):
        b, k = divmod(r, nchunks)
        return k * nb + b

    for r in range(nrows):
        u_s[pl.ds(chunk_row(r), nch, stride=pitch), :] = u_ref[:, r * lc:(r + 1) * lc]

    def u_tile(c):
        return u_s[c * pitch:c * pitch + nrows, :]
    delta = jnp.exp(ldt_ref[0])
    arc, aic = arc_ref[0], aic_ref[0]
    arr, air = arr_ref[0], air_ref[0]
    tau = lax.broadcasted_iota(jnp.int32, (1, lc), 1).astype(F32)
    p_re, p_im = _lam_pow(arc, aic, delta, tau)
    q_re, q_im = _lam_pow(arc, aic, delta, (lc - 1.0) - tau)
    cf_re_c, cf_im_c = _zoh_coef(arc, aic, delta)
    cf_re_r, cf_im_r = _zoh_coef(arr, air, delta)
    ct_re, ct_im = ct_re_ref[0], ct_im_ref[0]

    es = [_cplx_mul(ct_re[:, c:c + 1], ct_im[:, c:c + 1], p_re, p_im) for c in range(nch)]
    e_re = jnp.concatenate([e[0] for e in es], axis=1)
    e_im = jnp.concatenate([e[1] for e in es], axis=1)
    bbT_re, bbT_im = _cplx_mul(cf_re_r, cf_im_r, bT_re_ref[0], bT_im_ref[0])
    r0_s[...] = _dot(bbT_re.astype(BF16), e_re.astype(BF16)) - _dot(bbT_im.astype(BF16), e_im.astype(BF16))

    srow = lax.broadcasted_iota(jnp.int32, (lc, lc), 0)
    tcol = lax.broadcasted_iota(jnp.int32, (lc, lc), 1)
    causal = tcol >= srow

    def fill_rows(q, dst):
        for half in range(2):
            row = r0_s[pl.ds(2 * q + half, 1), :].astype(BF16).astype(F32)
            bits = lax.bitcast_convert_type(row, jnp.int32)
            rows = slice(half * lc, (half + 1) * lc)
            for c in range(0, nch, 2):
                word = (bits[:, c * lc:(c + 1) * lc]
                        | lax.shift_right_logical(bits[:, (c + 1) * lc:(c + 2) * lc], 16))
                blk = pltpu.roll(jnp.broadcast_to(word, (lc, lc)), 0, 1, stride=1, stride_axis=0)
                first = lax.bitcast_convert_type(blk & jnp.int32(-65536), F32)
                second = lax.bitcast_convert_type(lax.shift_left(blk, 16), F32)
                dst[rows, c * lc:(c + 1) * lc] = jnp.where(causal, first, 0.0).astype(BF16)
                dst[rows, (c + 1) * lc:(c + 2) * lc] = jnp.where(causal, second, 0.0).astype(BF16)

    def add_product(q, src):
        r_even = pl.multiple_of(2 * q * pitch, 8)
        r_odd = pl.multiple_of((2 * q + 1) * pitch, 8)
        lhs = jnp.concatenate([u_s[pl.ds(r_even, nrows), :], u_s[pl.ds(r_odd, nrows), :]], axis=1)
        yacc_s[...] += _dot(lhs.astype(BF16), src[...])

    nq = nch // 2
    yacc_s[...] = jnp.zeros_like(yacc_s)
    fill_rows(0, ma_s)

    def step(i, carry):
        q = 2 * i
        fill_rows(q + 1, mb_s)
        add_product(q, ma_s)
        fill_rows(jnp.minimum(q + 2, nq - 1), ma_s)
        add_product(q + 1, mb_s)
        return carry

    lax.fori_loop(0, nq // 2, step, 0)

    bb_re, bb_im = _cplx_mul(cf_re_c, cf_im_c, b_re_ref[0], b_im_ref[0])
    ws = [_cplx_mul(q_re, q_im, bb_re[:, c:c + 1], bb_im[:, c:c + 1]) for c in range(nch)]
    wt_re = jnp.concatenate([w[0] for w in ws], axis=1).astype(BF16)
    wt_im = jnp.concatenate([w[1] for w in ws], axis=1).astype(BF16)
    ub = jnp.concatenate([u_tile(c) for c in range(nch)], axis=1).astype(BF16)
    s_re_s[...] = _dot_nt(ub, wt_re)
    s_im_s[...] = _dot_nt(ub, wt_im)

    c_re, c_im = _lam_pow(arr, air, delta, float(lc))
    l1_re, l1_im = _lam_pow(arr, air, delta, 1.0)
    xr = jnp.zeros((nb, S5_STATE), F32)
    xi = jnp.zeros((nb, S5_STATE), F32)
    for k in range(nchunks):
        rows = slice(k * nb, (k + 1) * nb)
        xp_re_s[rows, :] = xr * l1_re - xi * l1_im
        xp_im_s[rows, :] = xr * l1_im + xi * l1_re
        xr, xi = (xr * c_re - xi * c_im + s_re_s[rows, :],
                  xr * c_im + xi * c_re + s_im_s[rows, :])

    y = yacc_s[...] + _dot(xp_re_s[...].astype(BF16), e_re.astype(BF16))
    y = y - _dot(xp_im_s[...].astype(BF16), e_im.astype(BF16))
    for c in range(nch):
        yc = y[:, c * lc:(c + 1) * lc] + d_ref[0, c:c + 1, :] * u_tile(c)
        y_s[c * pitch:c * pitch + nrows, :] = (
            0.5 * yc * (1.0 + jnp.tanh(math.sqrt(2.0 / math.pi) * (yc + 0.044715 * (yc * yc * yc)))))
    for r in range(nrows):
        o_ref[:, r * lc:(r + 1) * lc] = y_s[pl.ds(chunk_row(r), nch, stride=pitch), :]


def _s5(ut, bsz, a_re, a_im, log_dt, b_re, b_im, c_re, c_im, d):
    width, t = ut.shape
    l = t // bsz
    g, p, ch, lc = S5_GROUPS, S5_STATE, S5_GROUP_CH, S5_CHUNK
    nchunks = l // lc
    nrows = nchunks * bsz
    pitch = nrows + 8

    def per_group(shape):
        return pl.BlockSpec((1,) + shape, lambda i: (i, 0, 0))

    return pl.pallas_call(
        functools.partial(_s5_kernel, nb=bsz, nchunks=nchunks),
        grid=(g,),
        in_specs=[pl.BlockSpec((ch, t), lambda i: (i, 0)),
                  per_group((p, 1)), per_group((p, 1)), per_group((1, p)), per_group((1, p)),
                  per_group((1, 1)),
                  per_group((ch, p)), per_group((ch, p)),
                  per_group((p, ch)), per_group((p, ch)),
                  per_group((p, ch)), per_group((p, ch)),
                  per_group((ch, 1))],
        out_specs=pl.BlockSpec((ch, t), lambda i: (i, 0)),
        out_shape=jax.ShapeDtypeStruct((width, t), F32),
        scratch_shapes=[pltpu.VMEM((2 * lc, ch * lc), BF16), pltpu.VMEM((2 * lc, ch * lc), BF16),
                        pltpu.VMEM((nrows, ch * lc), F32), pltpu.VMEM((ch, ch * lc), F32),
                        pltpu.VMEM((nrows, p), F32), pltpu.VMEM((nrows, p), F32),
                        pltpu.VMEM((nrows, p), F32), pltpu.VMEM((nrows, p), F32),
                        pltpu.VMEM((ch * pitch, lc), F32), pltpu.VMEM((ch * pitch, lc), F32)],
        compiler_params=_cparams(("parallel",)),
        name="s5_ssm",
    )(ut,
      a_re.reshape(g, p, 1), a_im.reshape(g, p, 1), a_re.reshape(g, 1, p), a_im.reshape(g, 1, p),
      log_dt.reshape(g, 1, 1),
      b_re.transpose(0, 2, 1), b_im.transpose(0, 2, 1), b_re, b_im,
      c_re.transpose(0, 2, 1), c_im.transpose(0, 2, 1),
      d.reshape(g, ch, 1))


def _glu_kernel(yt_ref, w_ref, b_ref, o_ref):
    y = yt_ref[...].T
    gate = jax.nn.sigmoid(_dot(y.astype(BF16), w_ref[...]) + b_ref[...])
    o_ref[0] = (y * gate).astype(o_ref.dtype)


def _glu(yt, bsz, w, b, *, tm=512):
    n, t = yt.shape
    l = t // bsz
    tm = min(tm, l)
    nt = l // tm
    return pl.pallas_call(
        _glu_kernel,
        grid=(bsz, nt),
        in_specs=[pl.BlockSpec((n, tm), lambda b_, i: (0, b_ * nt + i)),
                  pl.BlockSpec((n, n), lambda b_, i: (0, 0)),
                  pl.BlockSpec((1, n), lambda b_, i: (0, 0))],
        out_specs=pl.BlockSpec((1, tm, n), lambda b_, i: (b_, i, 0)),
        out_shape=jax.ShapeDtypeStruct((bsz, l, n), BF16),
        compiler_params=_cparams(("parallel", "parallel")),
        name="s5_glu",
    )(yt, w, b.reshape(1, n))


def _ret_kernel(q_ref, k_ref, v_ref, g_ref, lg_ref, invf_ref, gn_ref, o_ref,
                r_s, d_s, cb_s, sb_s, *, tb):
    j = pl.program_id(1)
    idx = lax.broadcasted_iota(jnp.int32, (tb, 1), 0).astype(F32)

    @pl.when(j == 0)
    def _():
        r_s[...] = jnp.zeros_like(r_s)
        ri = lax.broadcasted_iota(jnp.int32, (tb, tb), 0)
        ci = lax.broadcasted_iota(jnp.int32, (tb, tb), 1)
        rc, cc = ri // RET_CHUNK, ci // RET_CHUNK
        dist = jnp.where(rc == cc, jnp.abs(ri - ci), jnp.maximum(ri - ci, 0)).astype(F32)
        for h in range(RET_HEADS):
            d_s[h] = jnp.where(cc > rc, 0.0, jnp.exp(lg_ref[h] * dist) * (RET_QK ** -0.5))
        ang = idx * invf_ref[...]
        cb_s[...] = jnp.cos(ang)
        sb_s[...] = jnp.sin(ang)

    base = (j * tb).astype(F32) * invf_ref[...]
    cos_a, sin_a = jnp.cos(base), jnp.sin(base)
    cos = cos_a * cb_s[...] - sin_a * sb_s[...]
    sin = sin_a * cb_s[...] + cos_a * sb_s[...]
    half = RET_QK // 2

    def rot(t):
        t1, t2 = t[:, :half], t[:, half:]
        return jnp.concatenate([t1 * cos - t2 * sin, t1 * sin + t2 * cos], axis=1)

    for h in range(RET_HEADS):
        lg = lg_ref[h]
        qs = slice(h * RET_QK, (h + 1) * RET_QK)
        vs = slice(h * RET_V, (h + 1) * RET_V)
        q = rot(q_ref[0, :, qs].astype(F32))
        k = rot(k_ref[0, :, qs].astype(F32))
        vb = v_ref[0, :, vs]
        xi = jnp.exp(lg * (idx + 1.0))
        zeta = jnp.exp(lg * (tb - 1.0 - idx)) * (RET_QK ** -0.5)
        gblk = jnp.exp(lg * float(tb))

        r = r_s[h]
        s = _dot_nt(q.astype(BF16), k.astype(BF16)) * d_s[h]
        o = _dot(s.astype(BF16), vb) + _dot((q * xi).astype(BF16), r.astype(BF16))
        r_s[h] = r * gblk + _dot_tn((k * zeta).astype(BF16), vb)

        mu = jnp.mean(o, axis=-1, keepdims=True)
        oc = o - mu
        y = oc * lax.rsqrt(jnp.mean(oc * oc, axis=-1, keepdims=True) + EPS)
        o_ref[0, :, vs] = (y * gn_ref[:, vs] * _silu(g_ref[0, :, vs].astype(F32))).astype(o_ref.dtype)


def _retention(proj, gn_g):
    bsz, l, _ = proj.shape
    tb = min(RET_BLOCK, l)
    hq = RET_HEADS
    dq, dv = hq * RET_QK, hq * RET_V
    log_g = jnp.log(1.0 - 2.0 ** (-5.0 - jnp.arange(hq, dtype=F32))).reshape(hq, 1, 1)
    inv_freq = (ROPE_BASE ** (-jnp.arange(0, RET_QK, 2, dtype=F32) / RET_QK)).reshape(1, RET_QK // 2)
    return pl.pallas_call(
        functools.partial(_ret_kernel, tb=tb),
        grid=(bsz, l // tb),
        in_specs=[pl.BlockSpec((1, tb, dq), lambda b, j: (b, j, 0)),
                  pl.BlockSpec((1, tb, dq), lambda b, j: (b, j, 1)),
                  pl.BlockSpec((1, tb, dv), lambda b, j: (b, j, 1)),
                  pl.BlockSpec((1, tb, dv), lambda b, j: (b, j, 2)),
                  pl.BlockSpec((hq, 1, 1), lambda b, j: (0, 0, 0)),
                  pl.BlockSpec((1, RET_QK // 2), lambda b, j: (0, 0)),
                  pl.BlockSpec((1, dv), lambda b, j: (0, 0))],
        out_specs=pl.BlockSpec((1, tb, dv), lambda b, j: (b, j, 0)),
        out_shape=jax.ShapeDtypeStruct((bsz, l, dv), BF16),
        scratch_shapes=[pltpu.VMEM((hq, RET_QK, RET_V), F32), pltpu.VMEM((hq, tb, tb), F32),
                        pltpu.VMEM((tb, RET_QK // 2), F32), pltpu.VMEM((tb, RET_QK // 2), F32)],
        compiler_params=_cparams(("parallel", "arbitrary")),
        name="retention",
    )(proj, proj, proj, proj, log_g, inv_freq, gn_g.reshape(1, -1))


def _split_mod(m, bsz):
    d = m.shape[-1] // 3
    return tuple(m[:, k * d:(k + 1) * d].reshape(bsz, 1, d) for k in range(3))


def kernel(x, c, norm_mix_g, ada_mix_w, ada_mix_b, norm_ffn_g, ada_ffn_w, ada_ffn_b, ffn_w_in, ffn_w_out, ab_w_in, ssd_conv_w, ssd_conv_b, ssd_dt_bias, ssd_a_log, ssd_d, ssd_norm_g, s5_a_re, s5_a_im, s5_log_dt, s5_b_re, s5_b_im, s5_c_re, s5_c_im, s5_d, s5_glu_w, s5_glu_b, ab_w_out, ret_w_in, ret_gn_g, ret_w_out, final_norm_g):
    bsz = x.shape[0]
    mod_mix = _ada_modulation(c, ada_mix_w, ada_mix_b)
    mod_ffn = _ada_modulation(c, ada_ffn_w, ada_ffn_b)

    shift, scale, gate = _split_mod(mod_mix[0], bsz)
    w0 = ab_w_in[0]
    n_main = SSD_INNER + SSD_XBC
    w_dt = jnp.pad(w0[:, n_main:n_main + SSD_HEADS], ((0, 0), (0, LANES - SSD_HEADS))).astype(BF16)
    w_ut = w0[:, n_main + SSD_HEADS:].T.astype(BF16)
    proj0, dtp, ut = _in_proj0(x, norm_mix_g[0], scale, shift, ab_w_in.astype(BF16), 0, n_main, w_dt, w_ut)
    y_a = _ssd(proj0, dtp, ssd_conv_w[0], ssd_conv_b[0], ssd_dt_bias[0], ssd_a_log[0], ssd_d[0],
               ssd_norm_g[0])
    yt = _s5(ut, bsz, s5_a_re[0], s5_a_im[0], s5_log_dt[0], s5_b_re[0], s5_b_im[0],
             s5_c_re[0], s5_c_im[0], s5_d[0])
    y_b = _glu(yt, bsz, s5_glu_w[0].astype(BF16), s5_glu_b[0])
    x = _matmul_residual([y_a, y_b], ab_w_out[0].astype(BF16), x, gate, name="out_proj0")

    shift, scale, gate = _split_mod(mod_ffn[0], bsz)
    ffn_w_in_b, ffn_w_out_b = ffn_w_in.astype(BF16), ffn_w_out.astype(BF16)
    x = _ffn(x, norm_ffn_g[0], scale, shift, gate, ffn_w_in_b, ffn_w_out_b, 0, final_norm_g,
             final_norm=False, name="ffn0")

    shift, scale, gate = _split_mod(mod_mix[1], bsz)
    proj1 = _norm_matmul(x, norm_mix_g[1], scale, shift, ret_w_in.astype(BF16), 0, name="in_proj1")
    y_r = _retention(proj1, ret_gn_g[0])
    x = _matmul_residual([y_r], ret_w_out[0].astype(BF16), x, gate, name="out_proj1")

    shift, scale, gate = _split_mod(mod_ffn[1], bsz)
    return _ffn(x, norm_ffn_g[1], scale, shift, gate, ffn_w_in_b, ffn_w_out_b, 1, final_norm_g,
                final_norm=True, name="ffn1")
```

```python
import functools
import math

import numpy as np
import jax
import jax.numpy as jnp
from jax import lax
from jax.experimental import pallas as pl
from jax.experimental.pallas import tpu as pltpu

F32 = jnp.float32
BF16 = jnp.bfloat16
EPS = 1e-6

D_MODEL = 2048
MIX_WIDTH = 2 * D_MODEL
SSD_HEADDIM = 64
SSD_INNER = 3 * MIX_WIDTH // 4
SSD_HEADS = SSD_INNER // SSD_HEADDIM
SSD_GROUPS = 8
SSD_HPG = SSD_HEADS // SSD_GROUPS
SSD_STATE = 128
SSD_CONV = 4
SSD_GW = SSD_HPG * SSD_HEADDIM
SSD_BC = SSD_GROUPS * SSD_STATE
SSD_XBC = SSD_INNER + 2 * SSD_BC
S5_WIDTH = MIX_WIDTH - SSD_INNER
S5_GROUP_CH = 16
S5_GROUPS = S5_WIDTH // S5_GROUP_CH
S5_STATE = 64
S5_CHUNK = 128
RET_HEADS = 8
RET_QK = D_MODEL // RET_HEADS
RET_V = MIX_WIDTH // RET_HEADS
RET_CHUNK = 64
RET_BLOCK = 256
ROPE_BASE = 10000.0
FFN_HIDDEN = ((-(-8 * D_MODEL // 3) + 255) // 256) * 256

LANES = 128
SSD_LC = 128
VMEM_LIMIT = 56 * 1024 * 1024


def _cparams(sem):
    return pltpu.CompilerParams(dimension_semantics=sem, vmem_limit_bytes=VMEM_LIMIT)


def _dot(a, b):
    return jnp.dot(a, b, preferred_element_type=F32)


def _dot_nt(a, b):
    return lax.dot_general(a, b, (((1,), (1,)), ((), ())), preferred_element_type=F32)


def _dot_tn(a, b):
    return lax.dot_general(a, b, (((0,), (0,)), ((), ())), preferred_element_type=F32)


def _split3(v):
    hi = v.astype(BF16)
    r = v - hi.astype(F32)
    mid = r.astype(BF16)
    lo = (r - mid.astype(F32)).astype(BF16)
    return hi, mid, lo


def _sel_right(v, e):
    hi, mid, lo = _split3(v)
    return (_dot(lo, e) + _dot(mid, e)) + _dot(hi, e)


def _sel_left(e, v):
    hi, mid, lo = _split3(v)
    return (_dot(e, lo) + _dot(e, mid)) + _dot(e, hi)


def _silu(v):
    h = 0.5 * v
    return h + h * jnp.tanh(h)


NORM_ROWS = 32


def _store_modulated_norm(x_ref, g_ref, sc_ref, sh_ref, hn_ref):
    gm = g_ref[...] * (1.0 + sc_ref[0])
    sh = sh_ref[0]

    def body(i, carry):
        r0 = pl.multiple_of(i * NORM_ROWS, NORM_ROWS)
        x = x_ref[0, pl.ds(r0, NORM_ROWS), :]
        ms = jnp.mean(x * x, axis=-1, keepdims=True)
        hn_ref[pl.ds(r0, NORM_ROWS), :] = ((x * lax.rsqrt(ms + EPS)) * gm + sh).astype(BF16)
        return carry

    lax.fori_loop(0, hn_ref.shape[0] // NORM_ROWS, body, 0, unroll=4)


def _ada_kernel(c_ref, w_ref, b_ref, o_ref):
    sc = _silu(c_ref[...]).astype(BF16)
    o_ref[0] = _dot(sc, w_ref[0].astype(BF16)) + b_ref[0]


def _ada_modulation(c, w, b):
    depth, d, n = w.shape
    bsz = c.shape[0]
    tn = 1024
    return pl.pallas_call(
        _ada_kernel,
        grid=(depth, n // tn),
        in_specs=[pl.BlockSpec((bsz, d), lambda i, j: (0, 0)),
                  pl.BlockSpec((1, d, tn), lambda i, j: (i, 0, j)),
                  pl.BlockSpec((1, 1, tn), lambda i, j: (i, 0, j))],
        out_specs=pl.BlockSpec((1, bsz, tn), lambda i, j: (i, 0, j)),
        out_shape=jax.ShapeDtypeStruct((depth, bsz, n), F32),
        compiler_params=_cparams(("parallel", "parallel")),
        name="ada_modulation",
    )(c, w, b.reshape(depth, 1, n))


def _norm_mm_kernel(x_ref, g_ref, sc_ref, sh_ref, w_ref, o_ref, hn_ref):
    @pl.when(pl.program_id(2) == 0)
    def _():
        _store_modulated_norm(x_ref, g_ref, sc_ref, sh_ref, hn_ref)

    o_ref[0] = _dot(hn_ref[...], w_ref[...]).astype(o_ref.dtype)


def _in_proj0_kernel(x_ref, g_ref, sc_ref, sh_ref, w_ref, wdt_ref, wu_ref, o_ref, dt_ref, ut_ref, hn_ref,
                     *, n_main):
    n = pl.program_id(2)

    @pl.when(n == 0)
    def _():
        _store_modulated_norm(x_ref, g_ref, sc_ref, sh_ref, hn_ref)
        dt_ref[0] = _dot(hn_ref[...], wdt_ref[...])

    @pl.when(n < n_main)
    def _():
        o_ref[0] = _dot(hn_ref[...], w_ref[...]).astype(o_ref.dtype)

    @pl.when(n >= n_main)
    def _():
        ut_ref[...] = _dot(hn_ref[...], wu_ref[...]).T


def _norm_matmul(x, g, scale, shift, w, layer, *, tm=1024, tn=1024, name):
    bsz, l, d = x.shape
    n = w.shape[2]
    tm = min(tm, l)
    return pl.pallas_call(
        _norm_mm_kernel,
        grid=(bsz, l // tm, n // tn),
        in_specs=[pl.BlockSpec((1, tm, d), lambda b, i, j: (b, i, 0)),
                  pl.BlockSpec((1, d), lambda b, i, j: (0, 0)),
                  pl.BlockSpec((1, 1, d), lambda b, i, j: (b, 0, 0)),
                  pl.BlockSpec((1, 1, d), lambda b, i, j: (b, 0, 0)),
                  pl.BlockSpec((None, d, tn), lambda b, i, j: (layer, 0, j))],
        out_specs=pl.BlockSpec((1, tm, tn), lambda b, i, j: (b, i, j)),
        out_shape=jax.ShapeDtypeStruct((bsz, l, n), BF16),
        scratch_shapes=[pltpu.VMEM((tm, d), BF16)],
        compiler_params=_cparams(("parallel", "parallel", "arbitrary")),
        name=name,
    )(x, g.reshape(1, d), scale, shift, w)


def _in_proj0(x, g, scale, shift, w, layer, n, w_dt, w_u, *, tm=1024, tn=1024, tc=512):
    bsz, l, d = x.shape
    ndt, nut = w_dt.shape[1], w_u.shape[1]
    tm = min(tm, l)
    n_main, n_ut, nt = n // tn, nut // tc, l // tm
    return pl.pallas_call(
        functools.partial(_in_proj0_kernel, n_main=n_main),
        grid=(bsz, nt, n_main + n_ut),
        in_specs=[pl.BlockSpec((1, tm, d), lambda b, i, j: (b, i, 0)),
                  pl.BlockSpec((1, d), lambda b, i, j: (0, 0)),
                  pl.BlockSpec((1, 1, d), lambda b, i, j: (b, 0, 0)),
                  pl.BlockSpec((1, 1, d), lambda b, i, j: (b, 0, 0)),
                  pl.BlockSpec((None, d, tn), lambda b, i, j: (layer, 0, jnp.minimum(j, n_main - 1))),
                  pl.BlockSpec((d, ndt), lambda b, i, j: (0, 0)),
                  pl.BlockSpec((d, tc), lambda b, i, j: (0, jnp.clip(j - n_main, 0, n_ut - 1)))],
        out_specs=[pl.BlockSpec((1, tm, tn), lambda b, i, j: (b, i, jnp.minimum(j, n_main - 1))),
                   pl.BlockSpec((1, tm, ndt), lambda b, i, j: (b, i, 0)),
                   pl.BlockSpec((tc, tm), lambda b, i, j: (jnp.clip(j - n_main, 0, n_ut - 1), b * nt + i))],
        out_shape=[jax.ShapeDtypeStruct((bsz, l, n), BF16),
                   jax.ShapeDtypeStruct((bsz, l, ndt), F32),
                   jax.ShapeDtypeStruct((nut, bsz * l), F32)],
        scratch_shapes=[pltpu.VMEM((tm, d), BF16)],
        compiler_params=_cparams(("parallel", "parallel", "arbitrary")),
        name="in_proj0",
    )(x, g.reshape(1, d), scale, shift, w, w_dt, w_u)


def _mm_res_kernel(*refs, bounds):
    na = len(bounds) - 1
    a_refs = refs[:na]
    w_ref, x_ref, gate_ref, o_ref = refs[na:]
    k = pl.program_id(2)

    @pl.when(k == 0)
    def _():
        o_ref[...] = jnp.zeros_like(o_ref)

    for i, a_ref in enumerate(a_refs):
        @pl.when((k >= bounds[i]) & (k < bounds[i + 1]))
        def _(a_ref=a_ref):
            o_ref[0] += _dot(a_ref[0], w_ref[...])

    @pl.when(k == bounds[-1] - 1)
    def _():
        o_ref[0] = x_ref[0] + gate_ref[0] * o_ref[0]


def _matmul_residual(a_list, w, x, gate, *, tm=1024, tk=1024, name):
    bsz, l, d = x.shape
    kk, n = w.shape
    tm = min(tm, l)
    bounds = [0]
    for a in a_list:
        bounds.append(bounds[-1] + a.shape[2] // tk)
    assert bounds[-1] * tk == kk

    def a_spec(lo, hi):
        return pl.BlockSpec((1, tm, tk), lambda b, i, k: (b, i, jnp.clip(k - lo, 0, hi - lo - 1)))

    return pl.pallas_call(
        functools.partial(_mm_res_kernel, bounds=tuple(bounds)),
        grid=(bsz, l // tm, bounds[-1]),
        in_specs=[a_spec(bounds[i], bounds[i + 1]) for i in range(len(a_list))] + [
            pl.BlockSpec((tk, n), lambda b, i, k: (k, 0)),
            pl.BlockSpec((1, tm, d), lambda b, i, k: (b, i, 0)),
            pl.BlockSpec((1, 1, d), lambda b, i, k: (b, 0, 0))],
        out_specs=pl.BlockSpec((1, tm, n), lambda b, i, k: (b, i, 0)),
        out_shape=jax.ShapeDtypeStruct((bsz, l, n), F32),
        compiler_params=_cparams(("parallel", "parallel", "arbitrary")),
        name=name,
    )(*a_list, w, x, gate)


def _ffn_kernel(x_ref, g_ref, sc_ref, sh_ref, gate_ref, wg_ref, wu_ref, wo_ref, fg_ref,
                o_ref, hn_ref, *, nh, final_norm):
    h = pl.program_id(2)

    @pl.when(h == 0)
    def _():
        _store_modulated_norm(x_ref, g_ref, sc_ref, sh_ref, hn_ref)
        o_ref[...] = jnp.zeros_like(o_ref)

    hn = hn_ref[...]
    gt = _dot(hn, wg_ref[...])
    up = _dot(hn, wu_ref[...])
    act = (_silu(gt) * up).astype(BF16)
    o_ref[0] += _dot(act, wo_ref[...])

    @pl.when(h == nh - 1)
    def _():
        gate = gate_ref[0]

        def body(i, carry):
            rows = pl.ds(pl.multiple_of(i * NORM_ROWS, NORM_ROWS), NORM_ROWS)
            y = x_ref[0, rows, :] + gate * o_ref[0, rows, :]
            if final_norm:
                ms = jnp.mean(y * y, axis=-1, keepdims=True)
                y = y * lax.rsqrt(ms + EPS) * fg_ref[...]
            o_ref[0, rows, :] = y
            return carry

        lax.fori_loop(0, hn_ref.shape[0] // NORM_ROWS, body, 0, unroll=4)


def _ffn(x, g, scale, shift, gate, w_in, w_out, layer, final_g, *, final_norm, tm=1024, th=512, name):
    bsz, l, d = x.shape
    hid = w_out.shape[1]
    tm = min(tm, l)
    nh = hid // th
    vec = pl.BlockSpec((1, 1, d), lambda b, i, h: (b, 0, 0))
    return pl.pallas_call(
        functools.partial(_ffn_kernel, nh=nh, final_norm=final_norm),
        grid=(bsz, l // tm, nh),
        in_specs=[pl.BlockSpec((1, tm, d), lambda b, i, h: (b, i, 0)),
                  pl.BlockSpec((1, d), lambda b, i, h: (0, 0)),
                  vec, vec, vec,
                  pl.BlockSpec((None, d, th), lambda b, i, h: (layer, 0, h)),
                  pl.BlockSpec((None, d, th), lambda b, i, h: (layer, 0, h + nh)),
                  pl.BlockSpec((None, th, d), lambda b, i, h: (layer, h, 0)),
                  pl.BlockSpec((1, d), lambda b, i, h: (0, 0))],
        out_specs=pl.BlockSpec((1, tm, d), lambda b, i, h: (b, i, 0)),
        out_shape=jax.ShapeDtypeStruct((bsz, l, d), F32),
        scratch_shapes=[pltpu.VMEM((tm, d), BF16)],
        compiler_params=_cparams(("parallel", "parallel", "arbitrary")),
        name=name,
    )(x, g.reshape(1, d), scale, shift, gate, w_in, w_in, w_out, final_g.reshape(1, d))


def _conv_silu(raw_ref, pad_ref, w_ref, b_ref, out_ref, first, lc, width, cb=256):
    @pl.when(first)
    def _():
        pad_ref[0:8, :] = jnp.zeros((8, width), F32)

    pad_ref[8:8 + lc, :] = raw_ref[0].astype(F32)
    ntap = SSD_CONV
    for c0 in range(0, width, cb):
        cs = slice(c0, c0 + cb)
        blk = pad_ref[:, cs]
        acc = b_ref[:, cs] + w_ref[ntap - 1:ntap, cs] * blk[8:]
        for j in range(1, ntap):
            acc = acc + w_ref[ntap - 1 - j:ntap - j, cs] * pltpu.roll(blk, j, 0)[8:]
        out_ref[:, cs] = _silu(acc)
    pad_ref[0:8, :] = pad_ref[lc:lc + 8, :]


def _ssd_kernel(z_ref, xs_ref, bm_ref, cm_ref, dt_ref,
                cwx_ref, cwb_ref, cwc_ref, cbx_ref, cbb_ref, cbc_ref,
                dtb_ref, alog_ref, dsk_ref, ng_ref, e_ref, tri_ref,
                o_ref,
                padx_s, padb_s, padc_s, xs_s, bm_s, cm_s, rhs_s, st_s, *, lc):
    first = pl.program_id(1) == 0

    @pl.when(first)
    def _():
        st_s[...] = jnp.zeros_like(st_s)

    _conv_silu(xs_ref, padx_s, cwx_ref, cbx_ref, xs_s, first, lc, SSD_INNER)
    _conv_silu(bm_ref, padb_s, cwb_ref, cbb_ref, bm_s, first, lc, SSD_BC)
    _conv_silu(cm_ref, padc_s, cwc_ref, cbc_ref, cm_s, first, lc, SSD_BC)

    rhs_s[0:lc, :] = xs_s[...].astype(BF16)
    rhs_s[lc:lc + SSD_STATE, :] = st_s[...].astype(BF16)

    v = dt_ref[0] + dtb_ref[...]
    dt = jnp.maximum(v, 0.0) + jnp.log(1.0 + jnp.exp(-jnp.abs(v)))
    a = -jnp.exp(alog_ref[...])
    acum = _sel_left(tri_ref[...], dt * a)
    acum_last = acum[lc - 1:lc, :]
    wdec = jnp.exp(acum_last - acum) * dt
    eacum = jnp.exp(acum)
    acum_t = acum.T
    dt_t = dt.T
    wexp = _sel_right(wdec, e_ref[...])
    eal = _sel_right(jnp.broadcast_to(jnp.exp(acum_last), (8, LANES)), e_ref[...])[0:1, :]

    row = lax.broadcasted_iota(jnp.int32, (lc, lc), 0)
    col = lax.broadcasted_iota(jnp.int32, (lc, lc), 1)
    causal = row >= col
    lane = lax.broadcasted_iota(jnp.int32, (lc, LANES), 1)
    low_half = lane < SSD_HEADDIM

    for g in range(SSD_GROUPS):
        gs = slice(g * SSD_STATE, (g + 1) * SSD_STATE)
        bm_g = bm_s[:, gs]
        cm_g = cm_s[:, gs]
        bm_gb = bm_g.astype(BF16)
        scores = _dot_nt(cm_g.astype(BF16), bm_gb)
        pieces = []
        for j in range(SSD_HPG // 2):
            ps = slice(g * SSD_GW + j * LANES, g * SSD_GW + (j + 1) * LANES)
            rhs = rhs_s[:, ps]
            ys = []
            for hh in range(2):
                h = g * SSD_HPG + 2 * j + hh
                seg = acum[:, h:h + 1] - acum_t[h:h + 1, :]
                lmat = jnp.exp(jnp.where(causal, seg, -1e30))
                amat = scores * lmat * dt_t[h:h + 1, :]
                cs = cm_g * eacum[:, h:h + 1]
                lhs = jnp.concatenate([amat, cs], axis=1).astype(BF16)
                ys.append(_dot(lhs, rhs))
            pieces.append(jnp.where(low_half, ys[0], ys[1]))
        y = jnp.concatenate(pieces, axis=1)

        ws = slice(g * SSD_GW, (g + 1) * SSD_GW)
        xs_g = xs_s[:, ws]
        xw = (xs_g * wexp[:, ws]).astype(BF16)
        st_s[:, ws] = st_s[:, ws] * eal[:, ws] + _dot_tn(bm_gb, xw)

        y = y + dsk_ref[:, ws] * xs_g
        y = y * _silu(z_ref[0, :, ws].astype(F32))
        ms = jnp.mean(y * y, axis=-1, keepdims=True)
        o_ref[0, :, ws] = (y * lax.rsqrt(ms + EPS) * ng_ref[:, ws]).astype(o_ref.dtype)


def _ssd(proj, dtp, conv_w, conv_b, dt_bias, a_log, d_skip, norm_g):
    bsz, l, _ = proj.shape
    lc = min(SSD_LC, l)
    i0 = SSD_INNER
    cwx, cwb, cwc = conv_w[:, :i0], conv_w[:, i0:i0 + SSD_BC], conv_w[:, i0 + SSD_BC:]
    cb = conv_b.reshape(1, -1)
    cbx, cbb, cbc = cb[:, :i0], cb[:, i0:i0 + SSD_BC], cb[:, i0 + SSD_BC:]
    pad = LANES - SSD_HEADS
    dtb = jnp.pad(dt_bias, (0, pad)).reshape(1, LANES)
    alog = jnp.pad(a_log, (0, pad)).reshape(1, LANES)
    dsk = jnp.repeat(d_skip, SSD_HEADDIM).reshape(1, i0)
    head_of_lane = np.arange(i0) // SSD_HEADDIM
    expand = jnp.asarray(np.arange(LANES)[:, None] == head_of_lane[None, :], BF16)
    tri = jnp.asarray(np.tril(np.ones((lc, lc))), BF16)

    def full(shape):
        return pl.BlockSpec(shape, lambda b, k: (0,) * len(shape))

    nb3 = i0 // SSD_BC
    return pl.pallas_call(
        functools.partial(_ssd_kernel, lc=lc),
        grid=(bsz, l // lc),
        in_specs=[pl.BlockSpec((1, lc, i0), lambda b, k: (b, k, 0)),
                  pl.BlockSpec((1, lc, i0), lambda b, k: (b, k, 1)),
                  pl.BlockSpec((1, lc, SSD_BC), lambda b, k: (b, k, 2 * nb3)),
                  pl.BlockSpec((1, lc, SSD_BC), lambda b, k: (b, k, 2 * nb3 + 1)),
                  pl.BlockSpec((1, lc, LANES), lambda b, k: (b, k, 0)),
                  full((SSD_CONV, i0)), full((SSD_CONV, SSD_BC)), full((SSD_CONV, SSD_BC)),
                  full((1, i0)), full((1, SSD_BC)), full((1, SSD_BC)),
                  full((1, LANES)), full((1, LANES)), full((1, i0)), full((1, i0)),
                  full((LANES, i0)), full((lc, lc))],
        out_specs=pl.BlockSpec((1, lc, i0), lambda b, k: (b, k, 0)),
        out_shape=jax.ShapeDtypeStruct((bsz, l, i0), BF16),
        scratch_shapes=[pltpu.VMEM((lc + 8, i0), F32), pltpu.VMEM((lc + 8, SSD_BC), F32),
                        pltpu.VMEM((lc + 8, SSD_BC), F32),
                        pltpu.VMEM((lc, i0), F32), pltpu.VMEM((lc, SSD_BC), F32),
                        pltpu.VMEM((lc, SSD_BC), F32),
                        pltpu.VMEM((lc + SSD_STATE, i0), BF16), pltpu.VMEM((SSD_STATE, i0), F32)],
        compiler_params=_cparams(("parallel", "arbitrary")),
        name="ssd_scan",
    )(proj, proj, proj, proj, dtp, cwx, cwb, cwc, cbx, cbb, cbc, dtb, alog, dsk,
      norm_g.reshape(1, i0), expand, tri)


def _cplx_mul(ar, ai, br, bi):
    return ar * br - ai * bi, ar * bi + ai * br


def _zoh_coef(are, aim, delta):
    er = jnp.exp(are * delta)
    br = er * jnp.cos(aim * delta) - 1.0
    bi = er * jnp.sin(aim * delta)
    den = are * are + aim * aim
    return (br * are + bi * aim) / den, (bi * are - br * aim) / den


def _lam_pow(are, aim, delta, n):
    mag = jnp.exp((are * delta) * n)
    ang = (aim * delta) * n
    return mag * jnp.cos(ang), mag * jnp.sin(ang)


def _s5_kernel(u_ref, arc_ref, aic_ref, arr_ref, air_ref, ldt_ref,
               bT_re_ref, bT_im_ref, b_re_ref, b_im_ref, ct_re_ref, ct_im_ref, d_ref,
               o_ref,
               ma_s, mb_s, yacc_s, r0_s, s_re_s, s_im_s, xp_re_s, xp_im_s, u_s, y_s, *, nb, nchunks):
    nch, lc = S5_GROUP_CH, S5_CHUNK
    nrows = nb * nchunks
    pitch = u_s.shape[0] // nch

    def chunk_row(r):
        b, k = divmod(r, nchunks)
        return k * nb + b

    for r in range(nrows):
        u_s[pl.ds(chunk_row(r), nch, stride=pitch), :] = u_ref[:, r * lc:(r + 1) * lc]

    def u_tile(c):
        return u_s[c * pitch:c * pitch + nrows, :]
    delta = jnp.exp(ldt_ref[0])
    arc, aic = arc_ref[0], aic_ref[0]
    arr, air = arr_ref[0], air_ref[0]
    tau = lax.broadcasted_iota(jnp.int32, (1, lc), 1).astype(F32)
    p_re, p_im = _lam_pow(arc, aic, delta, tau)
    q_re, q_im = _lam_pow(arc, aic, delta, (lc - 1.0) - tau)
    cf_re_c, cf_im_c = _zoh_coef(arc, aic, delta)
    cf_re_r, cf_im_r = _zoh_coef(arr, air, delta)
    ct_re, ct_im = ct_re_ref[0], ct_im_ref[0]

    es = [_cplx_mul(ct_re[:, c:c + 1], ct_im[:, c:c + 1], p_re, p_im) for c in range(nch)]
    e_re = jnp.concatenate([e[0] for e in es], axis=1)
    e_im = jnp.concatenate([e[1] for e in es], axis=1)
    bbT_re, bbT_im = _cplx_mul(cf_re_r, cf_im_r, bT_re_ref[0], bT_im_ref[0])
    r0_s[...] = _dot(bbT_re.astype(BF16), e_re.astype(BF16)) - _dot(bbT_im.astype(BF16), e_im.astype(BF16))

    srow = lax.broadcasted_iota(jnp.int32, (lc, lc), 0)
    tcol = lax.broadcasted_iota(jnp.int32, (lc, lc), 1)
    causal = tcol >= srow

    def fill_rows(q, dst):
        for half in range(2):
            row = r0_s[pl.ds(2 * q + half, 1), :].astype(BF16).astype(F32)
            bits = lax.bitcast_convert_type(row, jnp.int32)
            rows = slice(half * lc, (half + 1) * lc)
            for c in range(0, nch, 2):
                word = (bits[:, c * lc:(c + 1) * lc]
                        | lax.shift_right_logical(bits[:, (c + 1) * lc:(c + 2) * lc], 16))
                blk = pltpu.roll(jnp.broadcast_to(word, (lc, lc)), 0, 1, stride=1, stride_axis=0)
                first = lax.bitcast_convert_type(blk & jnp.int32(-65536), F32)
                second = lax.bitcast_convert_type(lax.shift_left(blk, 16), F32)
                dst[rows, c * lc:(c + 1) * lc] = jnp.where(causal, first, 0.0).astype(BF16)
                dst[rows, (c + 1) * lc:(c + 2) * lc] = jnp.where(causal, second, 0.0).astype(BF16)

    def add_product(q, src):
        r_even = pl.multiple_of(2 * q * pitch, 8)
        r_odd = pl.multiple_of((2 * q + 1) * pitch, 8)
        lhs = jnp.concatenate([u_s[pl.ds(r_even, nrows), :], u_s[pl.ds(r_odd, nrows), :]], axis=1)
        yacc_s[...] += _dot(lhs.astype(BF16), src[...])

    nq = nch // 2
    yacc_s[...] = jnp.zeros_like(yacc_s)
    fill_rows(0, ma_s)

    def step(i, carry):
        q = 2 * i
        fill_rows(q + 1, mb_s)
        add_product(q, ma_s)
        fill_rows(jnp.minimum(q + 2, nq - 1), ma_s)
        add_product(q + 1, mb_s)
        return carry

    lax.fori_loop(0, nq // 2, step, 0)

    bb_re, bb_im = _cplx_mul(cf_re_c, cf_im_c, b_re_ref[0], b_im_ref[0])
    ws = [_cplx_mul(q_re, q_im, bb_re[:, c:c + 1], bb_im[:, c:c + 1]) for c in range(nch)]
    wt_re = jnp.concatenate([w[0] for w in ws], axis=1).astype(BF16)
    wt_im = jnp.concatenate([w[1] for w in ws], axis=1).astype(BF16)
    ub = jnp.concatenate([u_tile(c) for c in range(nch)], axis=1).astype(BF16)
    s_re_s[...] = _dot_nt(ub, wt_re)
    s_im_s[...] = _dot_nt(ub, wt_im)

    c_re, c_im = _lam_pow(arr, air, delta, float(lc))
    l1_re, l1_im = _lam_pow(arr, air, delta, 1.0)
    xr = jnp.zeros((nb, S5_STATE), F32)
    xi = jnp.zeros((nb, S5_STATE), F32)
    for k in range(nchunks):
        rows = slice(k * nb, (k + 1) * nb)
        xp_re_s[rows, :] = xr * l1_re - xi * l1_im
        xp_im_s[rows, :] = xr * l1_im + xi * l1_re
        xr, xi = (xr * c_re - xi * c_im + s_re_s[rows, :],
                  xr * c_im + xi * c_re + s_im_s[rows, :])

    y = yacc_s[...] + _dot(xp_re_s[...].astype(BF16), e_re.astype(BF16))
    y = y - _dot(xp_im_s[...].astype(BF16), e_im.astype(BF16))
    for c in range(nch):
        yc = y[:, c * lc:(c + 1) * lc] + d_ref[0, c:c + 1, :] * u_tile(c)
        y_s[c * pitch:c * pitch + nrows, :] = (
            0.5 * yc * (1.0 + jnp.tanh(math.sqrt(2.0 / math.pi) * (yc + 0.044715 * (yc * yc * yc)))))
    for r in range(nrows):
        o_ref[:, r * lc:(r + 1) * lc] = y_s[pl.ds(chunk_row(r), nch, stride=pitch), :]


def _s5(ut, bsz, a_re, a_im, log_dt, b_re, b_im, c_re, c_im, d):
    width, t = ut.shape
    l = t // bsz
    g, p, ch, lc = S5_GROUPS, S5_STATE, S5_GROUP_CH, S5_CHUNK
    nchunks = l // lc
    nrows = nchunks * bsz
    pitch = nrows + 8

    def per_group(shape):
        return pl.BlockSpec((1,) + shape, lambda i: (i, 0, 0))

    return pl.pallas_call(
        functools.partial(_s5_kernel, nb=bsz, nchunks=nchunks),
        grid=(g,),
        in_specs=[pl.BlockSpec((ch, t), lambda i: (i, 0)),
                  per_group((p, 1)), per_group((p, 1)), per_group((1, p)), per_group((1, p)),
                  per_group((1, 1)),
                  per_group((ch, p)), per_group((ch, p)),
                  per_group((p, ch)), per_group((p, ch)),
                  per_group((p, ch)), per_group((p, ch)),
                  per_group((ch, 1))],
        out_specs=pl.BlockSpec((ch, t), lambda i: (i, 0)),
        out_shape=jax.ShapeDtypeStruct((width, t), F32),
        scratch_shapes=[pltpu.VMEM((2 * lc, ch * lc), BF16), pltpu.VMEM((2 * lc, ch * lc), BF16),
                        pltpu.VMEM((nrows, ch * lc), F32), pltpu.VMEM((ch, ch * lc), F32),
                        pltpu.VMEM((nrows, p), F32), pltpu.VMEM((nrows, p), F32),
                        pltpu.VMEM((nrows, p), F32), pltpu.VMEM((nrows, p), F32),
                        pltpu.VMEM((ch * pitch, lc), F32), pltpu.VMEM((ch * pitch, lc), F32)],
        compiler_params=_cparams(("parallel",)),
        name="s5_ssm",
    )(ut,
      a_re.reshape(g, p, 1), a_im.reshape(g, p, 1), a_re.reshape(g, 1, p), a_im.reshape(g, 1, p),
      log_dt.reshape(g, 1, 1),
      b_re.transpose(0, 2, 1), b_im.transpose(0, 2, 1), b_re, b_im,
      c_re.transpose(0, 2, 1), c_im.transpose(0, 2, 1),
      d.reshape(g, ch, 1))


def _glu_kernel(yt_ref, w_ref, b_ref, o_ref):
    y = yt_ref[...].T
    gate = jax.nn.sigmoid(_dot(y.astype(BF16), w_ref[...]) + b_ref[...])
    o_ref[0] = (y * gate).astype(o_ref.dtype)


def _glu(yt, bsz, w, b, *, tm=512):
    n, t = yt.shape
    l = t // bsz
    tm = min(tm, l)
    nt = l // tm
    return pl.pallas_call(
        _glu_kernel,
        grid=(bsz, nt),
        in_specs=[pl.BlockSpec((n, tm), lambda b_, i: (0, b_ * nt + i)),
                  pl.BlockSpec((n, n), lambda b_, i: (0, 0)),
                  pl.BlockSpec((1, n), lambda b_, i: (0, 0))],
        out_specs=pl.BlockSpec((1, tm, n), lambda b_, i: (b_, i, 0)),
        out_shape=jax.ShapeDtypeStruct((bsz, l, n), BF16),
        compiler_params=_cparams(("parallel", "parallel")),
        name="s5_glu",
    )(yt, w, b.reshape(1, n))


def _ret_kernel(q_ref, k_ref, v_ref, g_ref, lg_ref, invf_ref, gn_ref, o_ref,
                r_s, d_s, cb_s, sb_s, *, tb):
    j = pl.program_id(1)
    idx = lax.broadcasted_iota(jnp.int32, (tb, 1), 0).astype(F32)

    @pl.when(j == 0)
    def _():
        r_s[...] = jnp.zeros_like(r_s)
        ri = lax.broadcasted_iota(jnp.int32, (tb, tb), 0)
        ci = lax.broadcasted_iota(jnp.int32, (tb, tb), 1)
        rc, cc = ri // RET_CHUNK, ci // RET_CHUNK
        dist = jnp.where(rc == cc, jnp.abs(ri - ci), jnp.maximum(ri - ci, 0)).astype(F32)
        for h in range(RET_HEADS):
            d_s[h] = jnp.where(cc > rc, 0.0, jnp.exp(lg_ref[h] * dist) * (RET_QK ** -0.5))
        ang = idx * invf_ref[...]
        cb_s[...] = jnp.cos(ang)
        sb_s[...] = jnp.sin(ang)

    base = (j * tb).astype(F32) * invf_ref[...]
    cos_a, sin_a = jnp.cos(base), jnp.sin(base)
    cos = cos_a * cb_s[...] - sin_a * sb_s[...]
    sin = sin_a * cb_s[...] + cos_a * sb_s[...]
    half = RET_QK // 2

    def rot(t):
        t1, t2 = t[:, :half], t[:, half:]
        return jnp.concatenate([t1 * cos - t2 * sin, t1 * sin + t2 * cos], axis=1)

    for h in range(RET_HEADS):
        lg = lg_ref[h]
        qs = slice(h * RET_QK, (h + 1) * RET_QK)
        vs = slice(h * RET_V, (h + 1) * RET_V)
        q = rot(q_ref[0, :, qs].astype(F32))
        k = rot(k_ref[0, :, qs].astype(F32))
        vb = v_ref[0, :, vs]
        xi = jnp.exp(lg * (idx + 1.0))
        zeta = jnp.exp(lg * (tb - 1.0 - idx)) * (RET_QK ** -0.5)
        gblk = jnp.exp(lg * float(tb))

        r = r_s[h]
        s = _dot_nt(q.astype(BF16), k.astype(BF16)) * d_s[h]
        o = _dot(s.astype(BF16), vb) + _dot((q * xi).astype(BF16), r.astype(BF16))
        r_s[h] = r * gblk + _dot_tn((k * zeta).astype(BF16), vb)

        mu = jnp.mean(o, axis=-1, keepdims=True)
        oc = o - mu
        y = oc * lax.rsqrt(jnp.mean(oc * oc, axis=-1, keepdims=True) + EPS)
        o_ref[0, :, vs] = (y * gn_ref[:, vs] * _silu(g_ref[0, :, vs].astype(F32))).astype(o_ref.dtype)


def _retention(proj, gn_g):
    bsz, l, _ = proj.shape
    tb = min(RET_BLOCK, l)
    hq = RET_HEADS
    dq, dv = hq * RET_QK, hq * RET_V
    log_g = jnp.log(1.0 - 2.0 ** (-5.0 - jnp.arange(hq, dtype=F32))).reshape(hq, 1, 1)
    inv_freq = (ROPE_BASE ** (-jnp.arange(0, RET_QK, 2, dtype=F32) / RET_QK)).reshape(1, RET_QK // 2)
    return pl.pallas_call(
        functools.partial(_ret_kernel, tb=tb),
        grid=(bsz, l // tb),
        in_specs=[pl.BlockSpec((1, tb, dq), lambda b, j: (b, j, 0)),
                  pl.BlockSpec((1, tb, dq), lambda b, j: (b, j, 1)),
                  pl.BlockSpec((1, tb, dv), lambda b, j: (b, j, 1)),
                  pl.BlockSpec((1, tb, dv), lambda b, j: (b, j, 2)),
                  pl.BlockSpec((hq, 1, 1), lambda b, j: (0, 0, 0)),
                  pl.BlockSpec((1, RET_QK // 2), lambda b, j: (0, 0)),
                  pl.BlockSpec((1, dv), lambda b, j: (0, 0))],
        out_specs=pl.BlockSpec((1, tb, dv), lambda b, j: (b, j, 0)),
        out_shape=jax.ShapeDtypeStruct((bsz, l, dv), BF16),
        scratch_shapes=[pltpu.VMEM((hq, RET_QK, RET_V), F32), pltpu.VMEM((hq, tb, tb), F32),
                        pltpu.VMEM((tb, RET_QK // 2), F32), pltpu.VMEM((tb, RET_QK // 2), F32)],
        compiler_params=_cparams(("parallel", "arbitrary")),
        name="retention",
    )(proj, proj, proj, proj, log_g, inv_freq, gn_g.reshape(1, -1))


def _split_mod(m, bsz):
    d = m.shape[-1] // 3
    return tuple(m[:, k * d:(k + 1) * d].reshape(bsz, 1, d) for k in range(3))


def kernel(x, c, norm_mix_g, ada_mix_w, ada_mix_b, norm_ffn_g, ada_ffn_w, ada_ffn_b, ffn_w_in, ffn_w_out, ab_w_in, ssd_conv_w, ssd_conv_b, ssd_dt_bias, ssd_a_log, ssd_d, ssd_norm_g, s5_a_re, s5_a_im, s5_log_dt, s5_b_re, s5_b_im, s5_c_re, s5_c_im, s5_d, s5_glu_w, s5_glu_b, ab_w_out, ret_w_in, ret_gn_g, ret_w_out, final_norm_g):
    bsz = x.shape[0]
    mod_mix = _ada_modulation(c, ada_mix_w, ada_mix_b)
    mod_ffn = _ada_modulation(c, ada_ffn_w, ada_ffn_b)

    shift, scale, gate = _split_mod(mod_mix[0], bsz)
    w0 = ab_w_in.astype(BF16)
    n_main = SSD_INNER + SSD_XBC
    w_dt = jnp.pad(w0[0, :, n_main:n_main + SSD_HEADS], ((0, 0), (0, LANES - SSD_HEADS)))
    w_u = w0[0, :, n_main + SSD_HEADS:]
    proj0, dtp, ut = _in_proj0(x, norm_mix_g[0], scale, shift, w0, 0, n_main, w_dt, w_u)
    y_a = _ssd(proj0, dtp, ssd_conv_w[0], ssd_conv_b[0], ssd_dt_bias[0], ssd_a_log[0], ssd_d[0],
               ssd_norm_g[0])
    yt = _s5(ut, bsz, s5_a_re[0], s5_a_im[0], s5_log_dt[0], s5_b_re[0], s5_b_im[0],
             s5_c_re[0], s5_c_im[0], s5_d[0])
    y_b = _glu(yt, bsz, s5_glu_w[0].astype(BF16), s5_glu_b[0])
    x = _matmul_residual([y_a, y_b], ab_w_out[0].astype(BF16), x, gate, name="out_proj0")

    shift, scale, gate = _split_mod(mod_ffn[0], bsz)
    ffn_w_in_b, ffn_w_out_b = ffn_w_in.astype(BF16), ffn_w_out.astype(BF16)
    x = _ffn(x, norm_ffn_g[0], scale, shift, gate, ffn_w_in_b, ffn_w_out_b, 0, final_norm_g,
             final_norm=False, name="ffn0")

    shift, scale, gate = _split_mod(mod_mix[1], bsz)
    proj1 = _norm_matmul(x, norm_mix_g[1], scale, shift, ret_w_in.astype(BF16), 0, name="in_proj1")
    y_r = _retention(proj1, ret_gn_g[0])
    x = _matmul_residual([y_r], ret_w_out[0].astype(BF16), x, gate, name="out_proj1")

    shift, scale, gate = _split_mod(mod_ffn[1], bsz)
    return _ffn(x, norm_ffn_g[1], scale, shift, gate, ffn_w_in_b, ffn_w_out_b, 1, final_norm_g,
                final_norm=True, name="ffn1")
```

```python
import functools
import math

import numpy as np
import jax
import jax.numpy as jnp
from jax import lax
from jax.experimental import pallas as pl
from jax.experimental.pallas import tpu as pltpu

F32 = jnp.float32
BF16 = jnp.bfloat16
EPS = 1e-6

D_MODEL = 2048
MIX_WIDTH = 2 * D_MODEL
SSD_HEADDIM = 64
SSD_INNER = 3 * MIX_WIDTH // 4
SSD_HEADS = SSD_INNER // SSD_HEADDIM
SSD_GROUPS = 8
SSD_HPG = SSD_HEADS // SSD_GROUPS
SSD_STATE = 128
SSD_CONV = 4
SSD_GW = SSD_HPG * SSD_HEADDIM
SSD_BC = SSD_GROUPS * SSD_STATE
SSD_XBC = SSD_INNER + 2 * SSD_BC
S5_WIDTH = MIX_WIDTH - SSD_INNER
S5_GROUP_CH = 16
S5_GROUPS = S5_WIDTH // S5_GROUP_CH
S5_STATE = 64
S5_CHUNK = 128
RET_HEADS = 8
RET_QK = D_MODEL // RET_HEADS
RET_V = MIX_WIDTH // RET_HEADS
RET_CHUNK = 64
RET_BLOCK = 256
ROPE_BASE = 10000.0
FFN_HIDDEN = ((-(-8 * D_MODEL // 3) + 255) // 256) * 256

LANES = 128
SSD_LC = 128
VMEM_LIMIT = 56 * 1024 * 1024


def _cparams(sem):
    return pltpu.CompilerParams(dimension_semantics=sem, vmem_limit_bytes=VMEM_LIMIT)


def _dot(a, b):
    return jnp.dot(a, b, preferred_element_type=F32)


def _dot_nt(a, b):
    return lax.dot_general(a, b, (((1,), (1,)), ((), ())), preferred_element_type=F32)


def _dot_tn(a, b):
    return lax.dot_general(a, b, (((0,), (0,)), ((), ())), preferred_element_type=F32)


def _split3(v):
    hi = v.astype(BF16)
    r = v - hi.astype(F32)
    mid = r.astype(BF16)
    lo = (r - mid.astype(F32)).astype(BF16)
    return hi, mid, lo


def _sel_right(v, e):
    hi, mid, lo = _split3(v)
    return (_dot(lo, e) + _dot(mid, e)) + _dot(hi, e)


def _sel_left(e, v):
    hi, mid, lo = _split3(v)
    return (_dot(e, lo) + _dot(e, mid)) + _dot(e, hi)


def _silu(v):
    h = 0.5 * v
    return h + h * jnp.tanh(h)


NORM_ROWS = 32


def _store_modulated_norm(x_ref, g_ref, sc_ref, sh_ref, hn_ref):
    gm = g_ref[...] * (1.0 + sc_ref[0])
    sh = sh_ref[0]

    def body(i, carry):
        r0 = pl.multiple_of(i * NORM_ROWS, NORM_ROWS)
        x = x_ref[0, pl.ds(r0, NORM_ROWS), :]
        ms = jnp.mean(x * x, axis=-1, keepdims=True)
        hn_ref[pl.ds(r0, NORM_ROWS), :] = ((x * lax.rsqrt(ms + EPS)) * gm + sh).astype(BF16)
        return carry

    lax.fori_loop(0, hn_ref.shape[0] // NORM_ROWS, body, 0, unroll=4)


def _ada_kernel(c_ref, w_ref, b_ref, o_ref):
    sc = _silu(c_ref[...]).astype(BF16)
    o_ref[0] = _dot(sc, w_ref[0].astype(BF16)) + b_ref[0]


def _ada_modulation(c, w, b):
    depth, d, n = w.shape
    bsz = c.shape[0]
    tn = 1024
    return pl.pallas_call(
        _ada_kernel,
        grid=(depth, n // tn),
        in_specs=[pl.BlockSpec((bsz, d), lambda i, j: (0, 0)),
                  pl.BlockSpec((1, d, tn), lambda i, j: (i, 0, j)),
                  pl.BlockSpec((1, 1, tn), lambda i, j: (i, 0, j))],
        out_specs=pl.BlockSpec((1, bsz, tn), lambda i, j: (i, 0, j)),
        out_shape=jax.ShapeDtypeStruct((depth, bsz, n), F32),
        compiler_params=_cparams(("parallel", "parallel")),
        name="ada_modulation",
    )(c, w, b.reshape(depth, 1, n))


def _norm_mm_kernel(x_ref, g_ref, sc_ref, sh_ref, w_ref, o_ref, hn_ref):
    @pl.when(pl.program_id(2) == 0)
    def _():
        _store_modulated_norm(x_ref, g_ref, sc_ref, sh_ref, hn_ref)

    o_ref[0] = _dot(hn_ref[...], w_ref[...]).astype(o_ref.dtype)


def _in_proj0_kernel(x_ref, g_ref, sc_ref, sh_ref, w_ref, wdt_ref, wu_ref, o_ref, dt_ref, ut_ref, hn_ref,
                     *, n_ut):
    n = pl.program_id(2)

    @pl.when(n == 0)
    def _():
        _store_modulated_norm(x_ref, g_ref, sc_ref, sh_ref, hn_ref)
        dt_ref[0] = _dot(hn_ref[...], wdt_ref[...])

    @pl.when(n < n_ut)
    def _():
        ut_ref[...] = _dot(hn_ref[...], wu_ref[...]).T

    @pl.when(n >= n_ut)
    def _():
        o_ref[0] = _dot(hn_ref[...], w_ref[...]).astype(o_ref.dtype)


def _norm_matmul(x, g, scale, shift, w, layer, *, tm=1024, tn=1024, name):
    bsz, l, d = x.shape
    n = w.shape[2]
    tm = min(tm, l)
    return pl.pallas_call(
        _norm_mm_kernel,
        grid=(bsz, l // tm, n // tn),
        in_specs=[pl.BlockSpec((1, tm, d), lambda b, i, j: (b, i, 0)),
                  pl.BlockSpec((1, d), lambda b, i, j: (0, 0)),
                  pl.BlockSpec((1, 1, d), lambda b, i, j: (b, 0, 0)),
                  pl.BlockSpec((1, 1, d), lambda b, i, j: (b, 0, 0)),
                  pl.BlockSpec((None, d, tn), lambda b, i, j: (layer, 0, j))],
        out_specs=pl.BlockSpec((1, tm, tn), lambda b, i, j: (b, i, j)),
        out_shape=jax.ShapeDtypeStruct((bsz, l, n), BF16),
        scratch_shapes=[pltpu.VMEM((tm, d), BF16)],
        compiler_params=_cparams(("parallel", "parallel", "arbitrary")),
        name=name,
    )(x, g.reshape(1, d), scale, shift, w)


def _in_proj0(x, g, scale, shift, w, layer, n, w_dt, w_u, *, tm=1024, tn=1024, tc=512):
    bsz, l, d = x.shape
    ndt, nut = w_dt.shape[1], w_u.shape[1]
    tm = min(tm, l)
    n_main, n_ut, nt = n // tn, nut // tc, l // tm
    return pl.pallas_call(
        functools.partial(_in_proj0_kernel, n_ut=n_ut),
        grid=(bsz, nt, n_ut + n_main),
        in_specs=[pl.BlockSpec((1, tm, d), lambda b, i, j: (b, i, 0)),
                  pl.BlockSpec((1, d), lambda b, i, j: (0, 0)),
                  pl.BlockSpec((1, 1, d), lambda b, i, j: (b, 0, 0)),
                  pl.BlockSpec((1, 1, d), lambda b, i, j: (b, 0, 0)),
                  pl.BlockSpec((None, d, tn), lambda b, i, j: (layer, 0, jnp.clip(j - n_ut, 0, n_main - 1))),
                  pl.BlockSpec((d, ndt), lambda b, i, j: (0, 0)),
                  pl.BlockSpec((d, tc), lambda b, i, j: (0, jnp.minimum(j, n_ut - 1)))],
        out_specs=[pl.BlockSpec((1, tm, tn), lambda b, i, j: (b, i, jnp.clip(j - n_ut, 0, n_main - 1))),
                   pl.BlockSpec((1, tm, ndt), lambda b, i, j: (b, i, 0)),
                   pl.BlockSpec((tc, tm), lambda b, i, j: (jnp.minimum(j, n_ut - 1), b * nt + i))],
        out_shape=[jax.ShapeDtypeStruct((bsz, l, n), BF16),
                   jax.ShapeDtypeStruct((bsz, l, ndt), F32),
                   jax.ShapeDtypeStruct((nut, bsz * l), F32)],
        scratch_shapes=[pltpu.VMEM((tm, d), BF16)],
        compiler_params=_cparams(("parallel", "parallel", "arbitrary")),
        name="in_proj0",
    )(x, g.reshape(1, d), scale, shift, w, w_dt, w_u)


def _mm_res_kernel(*refs, bounds):
    na = len(bounds) - 1
    a_refs = refs[:na]
    w_ref, x_ref, gate_ref, o_ref = refs[na:]
    k = pl.program_id(2)

    @pl.when(k == 0)
    def _():
        o_ref[...] = jnp.zeros_like(o_ref)

    for i, a_ref in enumerate(a_refs):
        @pl.when((k >= bounds[i]) & (k < bounds[i + 1]))
        def _(a_ref=a_ref):
            o_ref[0] += _dot(a_ref[0], w_ref[...])

    @pl.when(k == bounds[-1] - 1)
    def _():
        o_ref[0] = x_ref[0] + gate_ref[0] * o_ref[0]


def _matmul_residual(a_list, w, x, gate, *, tm=1024, tk=1024, name):
    bsz, l, d = x.shape
    kk, n = w.shape
    tm = min(tm, l)
    bounds = [0]
    for a in a_list:
        bounds.append(bounds[-1] + a.shape[2] // tk)
    assert bounds[-1] * tk == kk

    def a_spec(lo, hi):
        return pl.BlockSpec((1, tm, tk), lambda b, i, k: (b, i, jnp.clip(k - lo, 0, hi - lo - 1)))

    return pl.pallas_call(
        functools.partial(_mm_res_kernel, bounds=tuple(bounds)),
        grid=(bsz, l // tm, bounds[-1]),
        in_specs=[a_spec(bounds[i], bounds[i + 1]) for i in range(len(a_list))] + [
            pl.BlockSpec((tk, n), lambda b, i, k: (k, 0)),
            pl.BlockSpec((1, tm, d), lambda b, i, k: (b, i, 0)),
            pl.BlockSpec((1, 1, d), lambda b, i, k: (b, 0, 0))],
        out_specs=pl.BlockSpec((1, tm, n), lambda b, i, k: (b, i, 0)),
        out_shape=jax.ShapeDtypeStruct((bsz, l, n), F32),
        compiler_params=_cparams(("parallel", "parallel", "arbitrary")),
        name=name,
    )(*a_list, w, x, gate)


def _ffn_kernel(x_ref, g_ref, sc_ref, sh_ref, gate_ref, wg_ref, wu_ref, wo_ref, fg_ref,
                o_ref, hn_ref, *, nh, final_norm):
    h = pl.program_id(2)

    @pl.when(h == 0)
    def _():
        _store_modulated_norm(x_ref, g_ref, sc_ref, sh_ref, hn_ref)
        o_ref[...] = jnp.zeros_like(o_ref)

    hn = hn_ref[...]
    gt = _dot(hn, wg_ref[...])
    up = _dot(hn, wu_ref[...])
    act = (_silu(gt) * up).astype(BF16)
    o_ref[0] += _dot(act, wo_ref[...])

    @pl.when(h == nh - 1)
    def _():
        gate = gate_ref[0]

        def body(i, carry):
            rows = pl.ds(pl.multiple_of(i * NORM_ROWS, NORM_ROWS), NORM_ROWS)
            y = x_ref[0, rows, :] + gate * o_ref[0, rows, :]
            if final_norm:
                ms = jnp.mean(y * y, axis=-1, keepdims=True)
                y = y * lax.rsqrt(ms + EPS) * fg_ref[...]
            o_ref[0, rows, :] = y
            return carry

        lax.fori_loop(0, hn_ref.shape[0] // NORM_ROWS, body, 0, unroll=4)


def _ffn(x, g, scale, shift, gate, w_in, w_out, layer, final_g, *, final_norm, tm=1024, th=512, name):
    bsz, l, d = x.shape
    hid = w_out.shape[1]
    tm = min(tm, l)
    nh = hid // th
    vec = pl.BlockSpec((1, 1, d), lambda b, i, h: (b, 0, 0))
    return pl.pallas_call(
        functools.partial(_ffn_kernel, nh=nh, final_norm=final_norm),
        grid=(bsz, l // tm, nh),
        in_specs=[pl.BlockSpec((1, tm, d), lambda b, i, h: (b, i, 0)),
                  pl.BlockSpec((1, d), lambda b, i, h: (0, 0)),
                  vec, vec, vec,
                  pl.BlockSpec((None, d, th), lambda b, i, h: (layer, 0, h)),
                  pl.BlockSpec((None, d, th), lambda b, i, h: (layer, 0, h + nh)),
                  pl.BlockSpec((None, th, d), lambda b, i, h: (layer, h, 0)),
                  pl.BlockSpec((1, d), lambda b, i, h: (0, 0))],
        out_specs=pl.BlockSpec((1, tm, d), lambda b, i, h: (b, i, 0)),
        out_shape=jax.ShapeDtypeStruct((bsz, l, d), F32),
        scratch_shapes=[pltpu.VMEM((tm, d), BF16)],
        compiler_params=_cparams(("parallel", "parallel", "arbitrary")),
        name=name,
    )(x, g.reshape(1, d), scale, shift, gate, w_in, w_in, w_out, final_g.reshape(1, d))


def _conv_silu(raw_ref, pad_ref, w_ref, b_ref, out_ref, first, lc, width, cb=256):
    @pl.when(first)
    def _():
        pad_ref[0:8, :] = jnp.zeros((8, width), F32)

    pad_ref[8:8 + lc, :] = raw_ref[0].astype(F32)
    ntap = SSD_CONV
    for c0 in range(0, width, cb):
        cs = slice(c0, c0 + cb)
        blk = pad_ref[:, cs]
        acc = b_ref[:, cs] + w_ref[ntap - 1:ntap, cs] * blk[8:]
        for j in range(1, ntap):
            acc = acc + w_ref[ntap - 1 - j:ntap - j, cs] * pltpu.roll(blk, j, 0)[8:]
        out_ref[:, cs] = _silu(acc)
    pad_ref[0:8, :] = pad_ref[lc:lc + 8, :]


def _ssd_kernel(z_ref, xs_ref, bm_ref, cm_ref, dt_ref,
                cwx_ref, cwb_ref, cwc_ref, cbx_ref, cbb_ref, cbc_ref,
                dtb_ref, alog_ref, dsk_ref, ng_ref, e_ref, tri_ref,
                o_ref,
                padx_s, padb_s, padc_s, xs_s, bm_s, cm_s, rhs_s, st_s, *, lc):
    first = pl.program_id(1) == 0

    @pl.when(first)
    def _():
        st_s[...] = jnp.zeros_like(st_s)

    _conv_silu(xs_ref, padx_s, cwx_ref, cbx_ref, xs_s, first, lc, SSD_INNER)
    _conv_silu(bm_ref, padb_s, cwb_ref, cbb_ref, bm_s, first, lc, SSD_BC)
    _conv_silu(cm_ref, padc_s, cwc_ref, cbc_ref, cm_s, first, lc, SSD_BC)

    rhs_s[0:lc, :] = xs_s[...].astype(BF16)
    rhs_s[lc:lc + SSD_STATE, :] = st_s[...].astype(BF16)

    v = dt_ref[0] + dtb_ref[...]
    dt = jnp.maximum(v, 0.0) + jnp.log(1.0 + jnp.exp(-jnp.abs(v)))
    a = -jnp.exp(alog_ref[...])
    acum = _sel_left(tri_ref[...], dt * a)
    acum_last = acum[lc - 1:lc, :]
    wdec = jnp.exp(acum_last - acum) * dt
    eacum = jnp.exp(acum)
    acum_t = acum.T
    dt_t = dt.T
    wexp = _sel_right(wdec, e_ref[...])
    eal = _sel_right(jnp.broadcast_to(jnp.exp(acum_last), (8, LANES)), e_ref[...])[0:1, :]

    row = lax.broadcasted_iota(jnp.int32, (lc, lc), 0)
    col = lax.broadcasted_iota(jnp.int32, (lc, lc), 1)
    causal = row >= col
    lane = lax.broadcasted_iota(jnp.int32, (lc, LANES), 1)
    low_half = lane < SSD_HEADDIM

    for g in range(SSD_GROUPS):
        gs = slice(g * SSD_STATE, (g + 1) * SSD_STATE)
        bm_g = bm_s[:, gs]
        cm_g = cm_s[:, gs]
        bm_gb = bm_g.astype(BF16)
        scores = _dot_nt(cm_g.astype(BF16), bm_gb)
        pieces = []
        for j in range(SSD_HPG // 2):
            ps = slice(g * SSD_GW + j * LANES, g * SSD_GW + (j + 1) * LANES)
            rhs = rhs_s[:, ps]
            ys = []
            for hh in range(2):
                h = g * SSD_HPG + 2 * j + hh
                seg = acum[:, h:h + 1] - acum_t[h:h + 1, :]
                lmat = jnp.exp(jnp.where(causal, seg, -1e30))
                amat = scores * lmat * dt_t[h:h + 1, :]
                cs = cm_g * eacum[:, h:h + 1]
                lhs = jnp.concatenate([amat, cs], axis=1).astype(BF16)
                ys.append(_dot(lhs, rhs))
            pieces.append(jnp.where(low_half, ys[0], ys[1]))
        y = jnp.concatenate(pieces, axis=1)

        ws = slice(g * SSD_GW, (g + 1) * SSD_GW)
        xs_g = xs_s[:, ws]
        xw = (xs_g * wexp[:, ws]).astype(BF16)
        st_s[:, ws] = st_s[:, ws] * eal[:, ws] + _dot_tn(bm_gb, xw)

        y = y + dsk_ref[:, ws] * xs_g
        y = y * _silu(z_ref[0, :, ws].astype(F32))
        ms = jnp.mean(y * y, axis=-1, keepdims=True)
        o_ref[0, :, ws] = (y * lax.rsqrt(ms + EPS) * ng_ref[:, ws]).astype(o_ref.dtype)


def _ssd(proj, dtp, conv_w, conv_b, dt_bias, a_log, d_skip, norm_g):
    bsz, l, _ = proj.shape
    lc = min(SSD_LC, l)
    i0 = SSD_INNER
    cwx, cwb, cwc = conv_w[:, :i0], conv_w[:, i0:i0 + SSD_BC], conv_w[:, i0 + SSD_BC:]
    cb = conv_b.reshape(1, -1)
    cbx, cbb, cbc = cb[:, :i0], cb[:, i0:i0 + SSD_BC], cb[:, i0 + SSD_BC:]
    pad = LANES - SSD_HEADS
    dtb = jnp.pad(dt_bias, (0, pad)).reshape(1, LANES)
    alog = jnp.pad(a_log, (0, pad)).reshape(1, LANES)
    dsk = jnp.repeat(d_skip, SSD_HEADDIM).reshape(1, i0)
    head_of_lane = np.arange(i0) // SSD_HEADDIM
    expand = jnp.asarray(np.arange(LANES)[:, None] == head_of_lane[None, :], BF16)
    tri = jnp.asarray(np.tril(np.ones((lc, lc))), BF16)

    def full(shape):
        return pl.BlockSpec(shape, lambda b, k: (0,) * len(shape))

    nb3 = i0 // SSD_BC
    return pl.pallas_call(
        functools.partial(_ssd_kernel, lc=lc),
        grid=(bsz, l // lc),
        in_specs=[pl.BlockSpec((1, lc, i0), lambda b, k: (b, k, 0)),
                  pl.BlockSpec((1, lc, i0), lambda b, k: (b, k, 1)),
                  pl.BlockSpec((1, lc, SSD_BC), lambda b, k: (b, k, 2 * nb3)),
                  pl.BlockSpec((1, lc, SSD_BC), lambda b, k: (b, k, 2 * nb3 + 1)),
                  pl.BlockSpec((1, lc, LANES), lambda b, k: (b, k, 0)),
                  full((SSD_CONV, i0)), full((SSD_CONV, SSD_BC)), full((SSD_CONV, SSD_BC)),
                  full((1, i0)), full((1, SSD_BC)), full((1, SSD_BC)),
                  full((1, LANES)), full((1, LANES)), full((1, i0)), full((1, i0)),
                  full((LANES, i0)), full((lc, lc))],
        out_specs=pl.BlockSpec((1, lc, i0), lambda b, k: (b, k, 0)),
        out_shape=jax.ShapeDtypeStruct((bsz, l, i0), BF16),
        scratch_shapes=[pltpu.VMEM((lc + 8, i0), F32), pltpu.VMEM((lc + 8, SSD_BC), F32),
                        pltpu.VMEM((lc + 8, SSD_BC), F32),
                        pltpu.VMEM((lc, i0), F32), pltpu.VMEM((lc, SSD_BC), F32),
                        pltpu.VMEM((lc, SSD_BC), F32),
                        pltpu.VMEM((lc + SSD_STATE, i0), BF16), pltpu.VMEM((SSD_STATE, i0), F32)],
        compiler_params=_cparams(("parallel", "arbitrary")),
        name="ssd_scan",
    )(proj, proj, proj, proj, dtp, cwx, cwb, cwc, cbx, cbb, cbc, dtb, alog, dsk,
      norm_g.reshape(1, i0), expand, tri)


def _cplx_mul(ar, ai, br, bi):
    return ar * br - ai * bi, ar * bi + ai * br


def _zoh_coef(are, aim, delta):
    er = jnp.exp(are * delta)
    br = er * jnp.cos(aim * delta) - 1.0
    bi = er * jnp.sin(aim * delta)
    den = are * are + aim * aim
    return (br * are + bi * aim) / den, (bi * are - br * aim) / den


def _lam_pow(are, aim, delta, n):
    mag = jnp.exp((are * delta) * n)
    ang = (aim * delta) * n
    return mag * jnp.cos(ang), mag * jnp.sin(ang)


def _s5_kernel(u_ref, arc_ref, aic_ref, arr_ref, air_ref, ldt_ref,
               bT_re_ref, bT_im_ref, b_re_ref, b_im_ref, ct_re_ref, ct_im_ref, d_ref,
               o_ref,
               ma_s, mb_s, yacc_s, r0_s, s_re_s, s_im_s, xp_re_s, xp_im_s, u_s, y_s, *, nb, nchunks):
    nch, lc = S5_GROUP_CH, S5_CHUNK
    nrows = nb * nchunks
    pitch = u_s.shape[0] // nch

    def chunk_row(r):
        b, k = divmod(r, nchunks)
        return k * nb + b

    for r in range(nrows):
        u_s[pl.ds(chunk_row(r), nch, stride=pitch), :] = u_ref[:, r * lc:(r + 1) * lc]

    def u_tile(c):
        return u_s[c * pitch:c * pitch + nrows, :]
    delta = jnp.exp(ldt_ref[0])
    arc, aic = arc_ref[0], aic_ref[0]
    arr, air = arr_ref[0], air_ref[0]
    tau = lax.broadcasted_iota(jnp.int32, (1, lc), 1).astype(F32)
    p_re, p_im = _lam_pow(arc, aic, delta, tau)
    q_re, q_im = _lam_pow(arc, aic, delta, (lc - 1.0) - tau)
    cf_re_c, cf_im_c = _zoh_coef(arc, aic, delta)
    cf_re_r, cf_im_r = _zoh_coef(arr, air, delta)
    ct_re, ct_im = ct_re_ref[0], ct_im_ref[0]

    es = [_cplx_mul(ct_re[:, c:c + 1], ct_im[:, c:c + 1], p_re, p_im) for c in range(nch)]
    e_re = jnp.concatenate([e[0] for e in es], axis=1)
    e_im = jnp.concatenate([e[1] for e in es], axis=1)
    bbT_re, bbT_im = _cplx_mul(cf_re_r, cf_im_r, bT_re_ref[0], bT_im_ref[0])
    r0_s[...] = _dot(bbT_re.astype(BF16), e_re.astype(BF16)) - _dot(bbT_im.astype(BF16), e_im.astype(BF16))

    srow = lax.broadcasted_iota(jnp.int32, (lc, lc), 0)
    tcol = lax.broadcasted_iota(jnp.int32, (lc, lc), 1)
    causal = tcol >= srow

    def fill_rows(q, dst):
        for half in range(2):
            row = r0_s[pl.ds(2 * q + half, 1), :].astype(BF16).astype(F32)
            bits = lax.bitcast_convert_type(row, jnp.int32)
            rows = slice(half * lc, (half + 1) * lc)
            for c in range(0, nch, 2):
                word = (bits[:, c * lc:(c + 1) * lc]
                        | lax.shift_right_logical(bits[:, (c + 1) * lc:(c + 2) * lc], 16))
                blk = pltpu.roll(jnp.broadcast_to(word, (lc, lc)), 0, 1, stride=1, stride_axis=0)
                first = lax.bitcast_convert_type(blk & jnp.int32(-65536), F32)
                second = lax.bitcast_convert_type(lax.shift_left(blk, 16), F32)
                dst[rows, c * lc:(c + 1) * lc] = jnp.where(causal, first, 0.0).astype(BF16)
                dst[rows, (c + 1) * lc:(c + 2) * lc] = jnp.where(causal, second, 0.0).astype(BF16)

    def add_product(q, src):
        r_even = pl.multiple_of(2 * q * pitch, 8)
        r_odd = pl.multiple_of((2 * q + 1) * pitch, 8)
        lhs = jnp.concatenate([u_s[pl.ds(r_even, nrows), :], u_s[pl.ds(r_odd, nrows), :]], axis=1)
        yacc_s[...] += _dot(lhs.astype(BF16), src[...])

    nq = nch // 2
    yacc_s[...] = jnp.zeros_like(yacc_s)
    fill_rows(0, ma_s)

    def step(i, carry):
        q = 2 * i
        fill_rows(q + 1, mb_s)
        add_product(q, ma_s)
        fill_rows(jnp.minimum(q + 2, nq - 1), ma_s)
        add_product(q + 1, mb_s)
        return carry

    lax.fori_loop(0, nq // 2, step, 0)

    bb_re, bb_im = _cplx_mul(cf_re_c, cf_im_c, b_re_ref[0], b_im_ref[0])
    ws = [_cplx_mul(q_re, q_im, bb_re[:, c:c + 1], bb_im[:, c:c + 1]) for c in range(nch)]
    wt_re = jnp.concatenate([w[0] for w in ws], axis=1).astype(BF16)
    wt_im = jnp.concatenate([w[1] for w in ws], axis=1).astype(BF16)
    ub = jnp.concatenate([u_tile(c) for c in range(nch)], axis=1).astype(BF16)
    s_re_s[...] = _dot_nt(ub, wt_re)
    s_im_s[...] = _dot_nt(ub, wt_im)

    c_re, c_im = _lam_pow(arr, air, delta, float(lc))
    l1_re, l1_im = _lam_pow(arr, air, delta, 1.0)
    xr = jnp.zeros((nb, S5_STATE), F32)
    xi = jnp.zeros((nb, S5_STATE), F32)
    for k in range(nchunks):
        rows = slice(k * nb, (k + 1) * nb)
        xp_re_s[rows, :] = xr * l1_re - xi * l1_im
        xp_im_s[rows, :] = xr * l1_im + xi * l1_re
        xr, xi = (xr * c_re - xi * c_im + s_re_s[rows, :],
                  xr * c_im + xi * c_re + s_im_s[rows, :])

    y = yacc_s[...] + _dot(xp_re_s[...].astype(BF16), e_re.astype(BF16))
    y = y - _dot(xp_im_s[...].astype(BF16), e_im.astype(BF16))
    for c in range(nch):
        yc = y[:, c * lc:(c + 1) * lc] + d_ref[0, c:c + 1, :] * u_tile(c)
        y_s[c * pitch:c * pitch + nrows, :] = (
            0.5 * yc * (1.0 + jnp.tanh(math.sqrt(2.0 / math.pi) * (yc + 0.044715 * (yc * yc * yc)))))
    for r in range(nrows):
        o_ref[:, r * lc:(r + 1) * lc] = y_s[pl.ds(chunk_row(r), nch, stride=pitch), :]


def _s5(ut, bsz, a_re, a_im, log_dt, b_re, b_im, c_re, c_im, d):
    width, t = ut.shape
    l = t // bsz
    g, p, ch, lc = S5_GROUPS, S5_STATE, S5_GROUP_CH, S5_CHUNK
    nchunks = l // lc
    nrows = nchunks * bsz
    pitch = nrows + 8

    def per_group(shape):
        return pl.BlockSpec((1,) + shape, lambda i: (i, 0, 0))

    return pl.pallas_call(
        functools.partial(_s5_kernel, nb=bsz, nchunks=nchunks),
        grid=(g,),
        in_specs=[pl.BlockSpec((ch, t), lambda i: (i, 0)),
                  per_group((p, 1)), per_group((p, 1)), per_group((1, p)), per_group((1, p)),
                  per_group((1, 1)),
                  per_group((ch, p)), per_group((ch, p)),
                  per_group((p, ch)), per_group((p, ch)),
                  per_group((p, ch)), per_group((p, ch)),
                  per_group((ch, 1))],
        out_specs=pl.BlockSpec((ch, t), lambda i: (i, 0)),
        out_shape=jax.ShapeDtypeStruct((width, t), F32),
        scratch_shapes=[pltpu.VMEM((2 * lc, ch * lc), BF16), pltpu.VMEM((2 * lc, ch * lc), BF16),
                        pltpu.VMEM((nrows, ch * lc), F32), pltpu.VMEM((ch, ch * lc), F32),
                        pltpu.VMEM((nrows, p), F32), pltpu.VMEM((nrows, p), F32),
                        pltpu.VMEM((nrows, p), F32), pltpu.VMEM((nrows, p), F32),
                        pltpu.VMEM((ch * pitch, lc), F32), pltpu.VMEM((ch * pitch, lc), F32)],
        compiler_params=_cparams(("parallel",)),
        name="s5_ssm",
    )(ut,
      a_re.reshape(g, p, 1), a_im.reshape(g, p, 1), a_re.reshape(g, 1, p), a_im.reshape(g, 1, p),
      log_dt.reshape(g, 1, 1),
      b_re.transpose(0, 2, 1), b_im.transpose(0, 2, 1), b_re, b_im,
      c_re.transpose(0, 2, 1), c_im.transpose(0, 2, 1),
      d.reshape(g, ch, 1))


def _glu_kernel(yt_ref, w_ref, b_ref, o_ref):
    y = yt_ref[...].T
    gate = jax.nn.sigmoid(_dot(y.astype(BF16), w_ref[...]) + b_ref[...])
    o_ref[0] = (y * gate).astype(o_ref.dtype)


def _glu(yt, bsz, w, b, *, tm=512):
    n, t = yt.shape
    l = t // bsz
    tm = min(tm, l)
    nt = l // tm
    return pl.pallas_call(
        _glu_kernel,
        grid=(bsz, nt),
        in_specs=[pl.BlockSpec((n, tm), lambda b_, i: (0, b_ * nt + i)),
                  pl.BlockSpec((n, n), lambda b_, i: (0, 0)),
                  pl.BlockSpec((1, n), lambda b_, i: (0, 0))],
        out_specs=pl.BlockSpec((1, tm, n), lambda b_, i: (b_, i, 0)),
        out_shape=jax.ShapeDtypeStruct((bsz, l, n), BF16),
        compiler_params=_cparams(("parallel", "parallel")),
        name="s5_glu",
    )(yt, w, b.reshape(1, n))


def _ret_kernel(q_ref, k_ref, v_ref, g_ref, lg_ref, invf_ref, gn_ref, o_ref,
                r_s, d_s, cb_s, sb_s, *, tb):
    j = pl.program_id(1)
    idx = lax.broadcasted_iota(jnp.int32, (tb, 1), 0).astype(F32)

    @pl.when(j == 0)
    def _():
        r_s[...] = jnp.zeros_like(r_s)
        ri = lax.broadcasted_iota(jnp.int32, (tb, tb), 0)
        ci = lax.broadcasted_iota(jnp.int32, (tb, tb), 1)
        rc, cc = ri // RET_CHUNK, ci // RET_CHUNK
        dist = jnp.where(rc == cc, jnp.abs(ri - ci), jnp.maximum(ri - ci, 0)).astype(F32)
        for h in range(RET_HEADS):
            d_s[h] = jnp.where(cc > rc, 0.0, jnp.exp(lg_ref[h] * dist) * (RET_QK ** -0.5))
        ang = idx * invf_ref[...]
        cb_s[...] = jnp.cos(ang)
        sb_s[...] = jnp.sin(ang)

    base = (j * tb).astype(F32) * invf_ref[...]
    cos_a, sin_a = jnp.cos(base), jnp.sin(base)
    cos = cos_a * cb_s[...] - sin_a * sb_s[...]
    sin = sin_a * cb_s[...] + cos_a * sb_s[...]
    half = RET_QK // 2

    def rot(t):
        t1, t2 = t[:, :half], t[:, half:]
        return jnp.concatenate([t1 * cos - t2 * sin, t1 * sin + t2 * cos], axis=1)

    for h in range(RET_HEADS):
        lg = lg_ref[h]
        qs = slice(h * RET_QK, (h + 1) * RET_QK)
        vs = slice(h * RET_V, (h + 1) * RET_V)
        q = rot(q_ref[0, :, qs].astype(F32))
        k = rot(k_ref[0, :, qs].astype(F32))
        vb = v_ref[0, :, vs]
        xi = jnp.exp(lg * (idx + 1.0))
        zeta = jnp.exp(lg * (tb - 1.0 - idx)) * (RET_QK ** -0.5)
        gblk = jnp.exp(lg * float(tb))

        r = r_s[h]
        s = _dot_nt(q.astype(BF16), k.astype(BF16)) * d_s[h]
        o = _dot(s.astype(BF16), vb) + _dot((q * xi).astype(BF16), r.astype(BF16))
        r_s[h] = r * gblk + _dot_tn((k * zeta).astype(BF16), vb)

        mu = jnp.mean(o, axis=-1, keepdims=True)
        oc = o - mu
        y = oc * lax.rsqrt(jnp.mean(oc * oc, axis=-1, keepdims=True) + EPS)
        o_ref[0, :, vs] = (y * gn_ref[:, vs] * _silu(g_ref[0, :, vs].astype(F32))).astype(o_ref.dtype)


def _retention(proj, gn_g):
    bsz, l, _ = proj.shape
    tb = min(RET_BLOCK, l)
    hq = RET_HEADS
    dq, dv = hq * RET_QK, hq * RET_V
    log_g = jnp.log(1.0 - 2.0 ** (-5.0 - jnp.arange(hq, dtype=F32))).reshape(hq, 1, 1)
    inv_freq = (ROPE_BASE ** (-jnp.arange(0, RET_QK, 2, dtype=F32) / RET_QK)).reshape(1, RET_QK // 2)
    return pl.pallas_call(
        functools.partial(_ret_kernel, tb=tb),
        grid=(bsz, l // tb),
        in_specs=[pl.BlockSpec((1, tb, dq), lambda b, j: (b, j, 0)),
                  pl.BlockSpec((1, tb, dq), lambda b, j: (b, j, 1)),
                  pl.BlockSpec((1, tb, dv), lambda b, j: (b, j, 1)),
                  pl.BlockSpec((1, tb, dv), lambda b, j: (b, j, 2)),
                  pl.BlockSpec((hq, 1, 1), lambda b, j: (0, 0, 0)),
                  pl.BlockSpec((1, RET_QK // 2), lambda b, j: (0, 0)),
                  pl.BlockSpec((1, dv), lambda b, j: (0, 0))],
        out_specs=pl.BlockSpec((1, tb, dv), lambda b, j: (b, j, 0)),
        out_shape=jax.ShapeDtypeStruct((bsz, l, dv), BF16),
        scratch_shapes=[pltpu.VMEM((hq, RET_QK, RET_V), F32), pltpu.VMEM((hq, tb, tb), F32),
                        pltpu.VMEM((tb, RET_QK // 2), F32), pltpu.VMEM((tb, RET_QK // 2), F32)],
        compiler_params=_cparams(("parallel", "arbitrary")),
        name="retention",
    )(proj, proj, proj, proj, log_g, inv_freq, gn_g.reshape(1, -1))


def _split_mod(m, bsz):
    d = m.shape[-1] // 3
    return tuple(m[:, k * d:(k + 1) * d].reshape(bsz, 1, d) for k in range(3))


def kernel(x, c, norm_mix_g, ada_mix_w, ada_mix_b, norm_ffn_g, ada_ffn_w, ada_ffn_b, ffn_w_in, ffn_w_out, ab_w_in, ssd_conv_w, ssd_conv_b, ssd_dt_bias, ssd_a_log, ssd_d, ssd_norm_g, s5_a_re, s5_a_im, s5_log_dt, s5_b_re, s5_b_im, s5_c_re, s5_c_im, s5_d, s5_glu_w, s5_glu_b, ab_w_out, ret_w_in, ret_gn_g, ret_w_out, final_norm_g):
    bsz = x.shape[0]
    mod_mix = _ada_modulation(c, ada_mix_w, ada_mix_b)
    mod_ffn = _ada_modulation(c, ada_ffn_w, ada_ffn_b)

    shift, scale, gate = _split_mod(mod_mix[0], bsz)
    w0 = ab_w_in.astype(BF16)
    n_main = SSD_INNER + SSD_XBC
    w_dt = jnp.pad(w0[0, :, n_main:n_main + SSD_HEADS], ((0, 0), (0, LANES - SSD_HEADS)))
    w_u = w0[0, :, n_main + SSD_HEADS:]
    proj0, dtp, ut = _in_proj0(x, norm_mix_g[0], scale, shift, w0, 0, n_main, w_dt, w_u)
    y_a = _ssd(proj0, dtp, ssd_conv_w[0], ssd_conv_b[0], ssd_dt_bias[0], ssd_a_log[0], ssd_d[0],
               ssd_norm_g[0])
    yt = _s5(ut, bsz, s5_a_re[0], s5_a_im[0], s5_log_dt[0], s5_b_re[0], s5_b_im[0],
             s5_c_re[0], s5_c_im[0], s5_d[0])
    y_b = _glu(yt, bsz, s5_glu_w[0].astype(BF16), s5_glu_b[0])
    x = _matmul_residual([y_a, y_b], ab_w_out[0].astype(BF16), x, gate, name="out_proj0")

    shift, scale, gate = _split_mod(mod_ffn[0], bsz)
    ffn_w_in_b, ffn_w_out_b = ffn_w_in.astype(BF16), ffn_w_out.astype(BF16)
    x = _ffn(x, norm_ffn_g[0], scale, shift, gate, ffn_w_in_b, ffn_w_out_b, 0, final_norm_g,
             final_norm=False, name="ffn0")

    shift, scale, gate = _split_mod(mod_mix[1], bsz)
    proj1 = _norm_matmul(x, norm_mix_g[1], scale, shift, ret_w_in.astype(BF16), 0, name="in_proj1")
    y_r = _retention(proj1, ret_gn_g[0])
    x = _matmul_residual([y_r], ret_w_out[0].astype(BF16), x, gate, name="out_proj1")

    shift, scale, gate = _split_mod(mod_ffn[1], bsz)
    return _ffn(x, norm_ffn_g[1], scale, shift, gate, ffn_w_in_b, ffn_w_out_b, 1, final_norm_g,
                final_norm=True, name="ffn1")
```

```python
import functools
import math

import numpy as np
import jax
import jax.numpy as jnp
from jax import lax
from jax.experimental import pallas as pl
from jax.experimental.pallas import tpu as pltpu

F32 = jnp.float32
BF16 = jnp.bfloat16
EPS = 1e-6

D_MODEL = 2048
MIX_WIDTH = 2 * D_MODEL
SSD_HEADDIM = 64
SSD_INNER = 3 * MIX_WIDTH // 4
SSD_HEADS = SSD_INNER // SSD_HEADDIM
SSD_GROUPS = 8
SSD_HPG = SSD_HEADS // SSD_GROUPS
SSD_STATE = 128
SSD_CONV = 4
SSD_GW = SSD_HPG * SSD_HEADDIM
SSD_BC = SSD_GROUPS * SSD_STATE
SSD_XBC = SSD_INNER + 2 * SSD_BC
S5_WIDTH = MIX_WIDTH - SSD_INNER
S5_GROUP_CH = 16
S5_GROUPS = S5_WIDTH // S5_GROUP_CH
S5_STATE = 64
S5_CHUNK = 128
RET_HEADS = 8
RET_QK = D_MODEL // RET_HEADS
RET_V = MIX_WIDTH // RET_HEADS
RET_CHUNK = 64
RET_BLOCK = 256
ROPE_BASE = 10000.0
FFN_HIDDEN = ((-(-8 * D_MODEL // 3) + 255) // 256) * 256

LANES = 128
SSD_LC = 128
VMEM_LIMIT = 56 * 1024 * 1024


def _cparams(sem):
    return pltpu.CompilerParams(dimension_semantics=sem, vmem_limit_bytes=VMEM_LIMIT)


def _dot(a, b):
    return jnp.dot(a, b, preferred_element_type=F32)


def _dot_nt(a, b):
    return lax.dot_general(a, b, (((1,), (1,)), ((), ())), preferred_element_type=F32)


def _dot_tn(a, b):
    return lax.dot_general(a, b, (((0,), (0,)), ((), ())), preferred_element_type=F32)


def _split3(v):
    hi = v.astype(BF16)
    r = v - hi.astype(F32)
    mid = r.astype(BF16)
    lo = (r - mid.astype(F32)).astype(BF16)
    return hi, mid, lo


def _sel_right(v, e):
    hi, mid, lo = _split3(v)
    return (_dot(lo, e) + _dot(mid, e)) + _dot(hi, e)


def _sel_left(e, v):
    hi, mid, lo = _split3(v)
    return (_dot(e, lo) + _dot(e, mid)) + _dot(e, hi)


def _silu(v):
    h = 0.5 * v
    return h + h * jnp.tanh(h)


NORM_ROWS = 32


def _store_modulated_norm(x_ref, g_ref, sc_ref, sh_ref, hn_ref):
    gm = g_ref[...] * (1.0 + sc_ref[0])
    sh = sh_ref[0]

    def body(i, carry):
        r0 = pl.multiple_of(i * NORM_ROWS, NORM_ROWS)
        x = x_ref[0, pl.ds(r0, NORM_ROWS), :]
        ms = jnp.mean(x * x, axis=-1, keepdims=True)
        hn_ref[pl.ds(r0, NORM_ROWS), :] = ((x * lax.rsqrt(ms + EPS)) * gm + sh).astype(BF16)
        return carry

    lax.fori_loop(0, hn_ref.shape[0] // NORM_ROWS, body, 0, unroll=4)


def _ada_kernel(c_ref, w_ref, b_ref, o_ref):
    sc = _silu(c_ref[...]).astype(BF16)
    o_ref[0] = _dot(sc, w_ref[0].astype(BF16)) + b_ref[0]


def _ada_modulation(c, w, b):
    depth, d, n = w.shape
    bsz = c.shape[0]
    tn = 1024
    return pl.pallas_call(
        _ada_kernel,
        grid=(depth, n // tn),
        in_specs=[pl.BlockSpec((bsz, d), lambda i, j: (0, 0)),
                  pl.BlockSpec((1, d, tn), lambda i, j: (i, 0, j)),
                  pl.BlockSpec((1, 1, tn), lambda i, j: (i, 0, j))],
        out_specs=pl.BlockSpec((1, bsz, tn), lambda i, j: (i, 0, j)),
        out_shape=jax.ShapeDtypeStruct((depth, bsz, n), F32),
        compiler_params=_cparams(("parallel", "parallel")),
        name="ada_modulation",
    )(c, w, b.reshape(depth, 1, n))


def _norm_mm_kernel(x_ref, g_ref, sc_ref, sh_ref, w_ref, o_ref, hn_ref):
    @pl.when(pl.program_id(2) == 0)
    def _():
        _store_modulated_norm(x_ref, g_ref, sc_ref, sh_ref, hn_ref)

    o_ref[0] = _dot(hn_ref[...], w_ref[...]).astype(o_ref.dtype)


def _in_proj0_kernel(x_ref, g_ref, sc_ref, sh_ref, w_ref, wdt_ref, wu_ref, o_ref, dt_ref, ut_ref, hn_ref,
                     *, n_ut):
    n = pl.program_id(2)

    @pl.when(n == 0)
    def _():
        _store_modulated_norm(x_ref, g_ref, sc_ref, sh_ref, hn_ref)
        dt_ref[0] = _dot(hn_ref[...], wdt_ref[...])

    @pl.when(n < n_ut)
    def _():
        ut_ref[...] = _dot(hn_ref[...], wu_ref[...]).T

    @pl.when(n >= n_ut)
    def _():
        o_ref[0] = _dot(hn_ref[...], w_ref[...]).astype(o_ref.dtype)


def _norm_matmul(x, g, scale, shift, w, layer, *, tm=1024, tn=1024, name):
    bsz, l, d = x.shape
    n = w.shape[2]
    tm = min(tm, l)
    nt, nj = l // tm, n // tn

    def col(b, i, j):
        return _snake(b * nt + i, j, nj)

    return pl.pallas_call(
        _norm_mm_kernel,
        grid=(bsz, nt, nj),
        in_specs=[pl.BlockSpec((1, tm, d), lambda b, i, j: (b, i, 0)),
                  pl.BlockSpec((1, d), lambda b, i, j: (0, 0)),
                  pl.BlockSpec((1, 1, d), lambda b, i, j: (b, 0, 0)),
                  pl.BlockSpec((1, 1, d), lambda b, i, j: (b, 0, 0)),
                  pl.BlockSpec((None, d, tn), lambda b, i, j: (layer, 0, col(b, i, j)))],
        out_specs=pl.BlockSpec((1, tm, tn), lambda b, i, j: (b, i, col(b, i, j))),
        out_shape=jax.ShapeDtypeStruct((bsz, l, n), BF16),
        scratch_shapes=[pltpu.VMEM((tm, d), BF16)],
        compiler_params=_cparams(("parallel", "parallel", "arbitrary")),
        name=name,
    )(x, g.reshape(1, d), scale, shift, w)


def _in_proj0(x, g, scale, shift, w, layer, n, w_dt, w_u, *, tm=1024, tn=1024, tc=512):
    bsz, l, d = x.shape
    ndt, nut = w_dt.shape[1], w_u.shape[1]
    tm = min(tm, l)
    n_main, n_ut, nt = n // tn, nut // tc, l // tm

    def main_tile(b, i, j):
        return _snake(b * nt + i, jnp.clip(j - n_ut, 0, n_main - 1), n_main)

    return pl.pallas_call(
        functools.partial(_in_proj0_kernel, n_ut=n_ut),
        grid=(bsz, nt, n_ut + n_main),
        in_specs=[pl.BlockSpec((1, tm, d), lambda b, i, j: (b, i, 0)),
                  pl.BlockSpec((1, d), lambda b, i, j: (0, 0)),
                  pl.BlockSpec((1, 1, d), lambda b, i, j: (b, 0, 0)),
                  pl.BlockSpec((1, 1, d), lambda b, i, j: (b, 0, 0)),
                  pl.BlockSpec((None, d, tn), lambda b, i, j: (layer, 0, main_tile(b, i, j))),
                  pl.BlockSpec((d, ndt), lambda b, i, j: (0, 0)),
                  pl.BlockSpec((d, tc), lambda b, i, j: (0, jnp.minimum(j, n_ut - 1)))],
        out_specs=[pl.BlockSpec((1, tm, tn), lambda b, i, j: (b, i, main_tile(b, i, j))),
                   pl.BlockSpec((1, tm, ndt), lambda b, i, j: (b, i, 0)),
                   pl.BlockSpec((tc, tm), lambda b, i, j: (jnp.minimum(j, n_ut - 1), b * nt + i))],
        out_shape=[jax.ShapeDtypeStruct((bsz, l, n), BF16),
                   jax.ShapeDtypeStruct((bsz, l, ndt), F32),
                   jax.ShapeDtypeStruct((nut, bsz * l), F32)],
        scratch_shapes=[pltpu.VMEM((tm, d), BF16)],
        compiler_params=_cparams(("parallel", "parallel", "arbitrary")),
        name="in_proj0",
    )(x, g.reshape(1, d), scale, shift, w, w_dt, w_u)


def _snake(tile, j, n):
    return jnp.where(tile % 2 == 0, j, n - 1 - j)


def _mm_res_kernel(*refs, bounds, nt):
    na = len(bounds) - 1
    a_refs = refs[:na]
    w_ref, x_ref, gate_ref, o_ref = refs[na:]
    step = pl.program_id(2)
    k = _snake(pl.program_id(0) * nt + pl.program_id(1), step, bounds[-1])

    @pl.when(step == 0)
    def _():
        o_ref[...] = jnp.zeros_like(o_ref)

    for i, a_ref in enumerate(a_refs):
        @pl.when((k >= bounds[i]) & (k < bounds[i + 1]))
        def _(a_ref=a_ref):
            o_ref[0] += _dot(a_ref[0], w_ref[...])

    @pl.when(step == bounds[-1] - 1)
    def _():
        o_ref[0] = x_ref[0] + gate_ref[0] * o_ref[0]


def _matmul_residual(a_list, w, x, gate, *, tm=1024, tk=1024, name):
    bsz, l, d = x.shape
    kk, n = w.shape
    tm = min(tm, l)
    bounds = [0]
    for a in a_list:
        bounds.append(bounds[-1] + a.shape[2] // tk)
    assert bounds[-1] * tk == kk

    nt, nk = l // tm, bounds[-1]

    def ktile(b, i, k):
        return _snake(b * nt + i, k, nk)

    def a_spec(lo, hi):
        return pl.BlockSpec((1, tm, tk), lambda b, i, k: (b, i, jnp.clip(ktile(b, i, k) - lo, 0, hi - lo - 1)))

    return pl.pallas_call(
        functools.partial(_mm_res_kernel, bounds=tuple(bounds), nt=nt),
        grid=(bsz, nt, nk),
        in_specs=[a_spec(bounds[i], bounds[i + 1]) for i in range(len(a_list))] + [
            pl.BlockSpec((tk, n), lambda b, i, k: (ktile(b, i, k), 0)),
            pl.BlockSpec((1, tm, d), lambda b, i, k: (b, i, 0)),
            pl.BlockSpec((1, 1, d), lambda b, i, k: (b, 0, 0))],
        out_specs=pl.BlockSpec((1, tm, n), lambda b, i, k: (b, i, 0)),
        out_shape=jax.ShapeDtypeStruct((bsz, l, n), F32),
        compiler_params=_cparams(("parallel", "parallel", "arbitrary")),
        name=name,
    )(*a_list, w, x, gate)


def _ffn_kernel(x_ref, g_ref, sc_ref, sh_ref, gate_ref, wg_ref, wu_ref, wo_ref, fg_ref,
                o_ref, hn_ref, *, nh, final_norm):
    h = pl.program_id(2)

    @pl.when(h == 0)
    def _():
        _store_modulated_norm(x_ref, g_ref, sc_ref, sh_ref, hn_ref)
        o_ref[...] = jnp.zeros_like(o_ref)

    hn = hn_ref[...]
    gt = _dot(hn, wg_ref[...])
    up = _dot(hn, wu_ref[...])
    act = (_silu(gt) * up).astype(BF16)
    o_ref[0] += _dot(act, wo_ref[...])

    @pl.when(h == nh - 1)
    def _():
        gate = gate_ref[0]

        def body(i, carry):
            rows = pl.ds(pl.multiple_of(i * NORM_ROWS, NORM_ROWS), NORM_ROWS)
            y = x_ref[0, rows, :] + gate * o_ref[0, rows, :]
            if final_norm:
                ms = jnp.mean(y * y, axis=-1, keepdims=True)
                y = y * lax.rsqrt(ms + EPS) * fg_ref[...]
            o_ref[0, rows, :] = y
            return carry

        lax.fori_loop(0, hn_ref.shape[0] // NORM_ROWS, body, 0, unroll=4)


def _ffn(x, g, scale, shift, gate, w_in, w_out, layer, final_g, *, final_norm, tm=1024, th=512, name):
    bsz, l, d = x.shape
    hid = w_out.shape[1]
    tm = min(tm, l)
    nh = hid // th
    nt = l // tm
    vec = pl.BlockSpec((1, 1, d), lambda b, i, h: (b, 0, 0))

    def hid_tile(b, i, h):
        return _snake(b * nt + i, h, nh)

    return pl.pallas_call(
        functools.partial(_ffn_kernel, nh=nh, final_norm=final_norm),
        grid=(bsz, l // tm, nh),
        in_specs=[pl.BlockSpec((1, tm, d), lambda b, i, h: (b, i, 0)),
                  pl.BlockSpec((1, d), lambda b, i, h: (0, 0)),
                  vec, vec, vec,
                  pl.BlockSpec((None, d, th), lambda b, i, h: (layer, 0, hid_tile(b, i, h))),
                  pl.BlockSpec((None, d, th), lambda b, i, h: (layer, 0, hid_tile(b, i, h) + nh)),
                  pl.BlockSpec((None, th, d), lambda b, i, h: (layer, hid_tile(b, i, h), 0)),
                  pl.BlockSpec((1, d), lambda b, i, h: (0, 0))],
        out_specs=pl.BlockSpec((1, tm, d), lambda b, i, h: (b, i, 0)),
        out_shape=jax.ShapeDtypeStruct((bsz, l, d), F32),
        scratch_shapes=[pltpu.VMEM((tm, d), BF16)],
        compiler_params=_cparams(("parallel", "parallel", "arbitrary")),
        name=name,
    )(x, g.reshape(1, d), scale, shift, gate, w_in, w_in, w_out, final_g.reshape(1, d))


def _conv_silu(raw_ref, pad_ref, w_ref, b_ref, out_ref, first, lc, width, cb=256):
    @pl.when(first)
    def _():
        pad_ref[0:8, :] = jnp.zeros((8, width), F32)

    pad_ref[8:8 + lc, :] = raw_ref[0].astype(F32)
    ntap = SSD_CONV
    for c0 in range(0, width, cb):
        cs = slice(c0, c0 + cb)
        blk = pad_ref[:, cs]
        acc = b_ref[:, cs] + w_ref[ntap - 1:ntap, cs] * blk[8:]
        for j in range(1, ntap):
            acc = acc + w_ref[ntap - 1 - j:ntap - j, cs] * pltpu.roll(blk, j, 0)[8:]
        out_ref[:, cs] = _silu(acc)
    pad_ref[0:8, :] = pad_ref[lc:lc + 8, :]


def _ssd_kernel(z_ref, xs_ref, bm_ref, cm_ref, dt_ref,
                cwx_ref, cwb_ref, cwc_ref, cbx_ref, cbb_ref, cbc_ref,
                dtb_ref, alog_ref, dsk_ref, ng_ref, e_ref, tri_ref,
                o_ref,
                padx_s, padb_s, padc_s, xs_s, bm_s, cm_s, rhs_s, st_s, *, lc):
    first = pl.program_id(1) == 0

    @pl.when(first)
    def _():
        st_s[...] = jnp.zeros_like(st_s)

    _conv_silu(xs_ref, padx_s, cwx_ref, cbx_ref, xs_s, first, lc, SSD_INNER)
    _conv_silu(bm_ref, padb_s, cwb_ref, cbb_ref, bm_s, first, lc, SSD_BC)
    _conv_silu(cm_ref, padc_s, cwc_ref, cbc_ref, cm_s, first, lc, SSD_BC)

    rhs_s[0:lc, :] = xs_s[...].astype(BF16)
    rhs_s[lc:lc + SSD_STATE, :] = st_s[...].astype(BF16)

    v = dt_ref[0] + dtb_ref[...]
    dt = jnp.maximum(v, 0.0) + jnp.log(1.0 + jnp.exp(-jnp.abs(v)))
    a = -jnp.exp(alog_ref[...])
    acum = _sel_left(tri_ref[...], dt * a)
    acum_last = acum[lc - 1:lc, :]
    wdec = jnp.exp(acum_last - acum) * dt
    eacum = jnp.exp(acum)
    acum_t = acum.T
    dt_t = dt.T
    wexp = _sel_right(wdec, e_ref[...])
    eal = _sel_right(jnp.broadcast_to(jnp.exp(acum_last), (8, LANES)), e_ref[...])[0:1, :]

    row = lax.broadcasted_iota(jnp.int32, (lc, lc), 0)
    col = lax.broadcasted_iota(jnp.int32, (lc, lc), 1)
    causal = row >= col
    lane = lax.broadcasted_iota(jnp.int32, (lc, LANES), 1)
    low_half = lane < SSD_HEADDIM

    for g in range(SSD_GROUPS):
        gs = slice(g * SSD_STATE, (g + 1) * SSD_STATE)
        bm_g = bm_s[:, gs]
        cm_g = cm_s[:, gs]
        bm_gb = bm_g.astype(BF16)
        scores = _dot_nt(cm_g.astype(BF16), bm_gb)
        pieces = []
        for j in range(SSD_HPG // 2):
            ps = slice(g * SSD_GW + j * LANES, g * SSD_GW + (j + 1) * LANES)
            rhs = rhs_s[:, ps]
            ys = []
            for hh in range(2):
                h = g * SSD_HPG + 2 * j + hh
                seg = acum[:, h:h + 1] - acum_t[h:h + 1, :]
                lmat = jnp.exp(jnp.where(causal, seg, -1e30))
                amat = scores * lmat * dt_t[h:h + 1, :]
                cs = cm_g * eacum[:, h:h + 1]
                lhs = jnp.concatenate([amat, cs], axis=1).astype(BF16)
                ys.append(_dot(lhs, rhs))
            pieces.append(jnp.where(low_half, ys[0], ys[1]))
        y = jnp.concatenate(pieces, axis=1)

        ws = slice(g * SSD_GW, (g + 1) * SSD_GW)
        xs_g = xs_s[:, ws]
        xw = (xs_g * wexp[:, ws]).astype(BF16)
        st_s[:, ws] = st_s[:, ws] * eal[:, ws] + _dot_tn(bm_gb, xw)

        y = y + dsk_ref[:, ws] * xs_g
        y = y * _silu(z_ref[0, :, ws].astype(F32))
        ms = jnp.mean(y * y, axis=-1, keepdims=True)
        o_ref[0, :, ws] = (y * lax.rsqrt(ms + EPS) * ng_ref[:, ws]).astype(o_ref.dtype)


def _ssd(proj, dtp, conv_w, conv_b, dt_bias, a_log, d_skip, norm_g):
    bsz, l, _ = proj.shape
    lc = min(SSD_LC, l)
    i0 = SSD_INNER
    cwx, cwb, cwc = conv_w[:, :i0], conv_w[:, i0:i0 + SSD_BC], conv_w[:, i0 + SSD_BC:]
    cb = conv_b.reshape(1, -1)
    cbx, cbb, cbc = cb[:, :i0], cb[:, i0:i0 + SSD_BC], cb[:, i0 + SSD_BC:]
    pad = LANES - SSD_HEADS
    dtb = jnp.pad(dt_bias, (0, pad)).reshape(1, LANES)
    alog = jnp.pad(a_log, (0, pad)).reshape(1, LANES)
    dsk = jnp.repeat(d_skip, SSD_HEADDIM).reshape(1, i0)
    head_of_lane = np.arange(i0) // SSD_HEADDIM
    expand = jnp.asarray(np.arange(LANES)[:, None] == head_of_lane[None, :], BF16)
    tri = jnp.asarray(np.tril(np.ones((lc, lc))), BF16)

    def full(shape):
        return pl.BlockSpec(shape, lambda b, k: (0,) * len(shape))

    nb3 = i0 // SSD_BC
    return pl.pallas_call(
        functools.partial(_ssd_kernel, lc=lc),
        grid=(bsz, l // lc),
        in_specs=[pl.BlockSpec((1, lc, i0), lambda b, k: (b, k, 0)),
                  pl.BlockSpec((1, lc, i0), lambda b, k: (b, k, 1)),
                  pl.BlockSpec((1, lc, SSD_BC), lambda b, k: (b, k, 2 * nb3)),
                  pl.BlockSpec((1, lc, SSD_BC), lambda b, k: (b, k, 2 * nb3 + 1)),
                  pl.BlockSpec((1, lc, LANES), lambda b, k: (b, k, 0)),
                  full((SSD_CONV, i0)), full((SSD_CONV, SSD_BC)), full((SSD_CONV, SSD_BC)),
                  full((1, i0)), full((1, SSD_BC)), full((1, SSD_BC)),
                  full((1, LANES)), full((1, LANES)), full((1, i0)), full((1, i0)),
                  full((LANES, i0)), full((lc, lc))],
        out_specs=pl.BlockSpec((1, lc, i0), lambda b, k: (b, k, 0)),
        out_shape=jax.ShapeDtypeStruct((bsz, l, i0), BF16),
        scratch_shapes=[pltpu.VMEM((lc + 8, i0), F32), pltpu.VMEM((lc + 8, SSD_BC), F32),
                        pltpu.VMEM((lc + 8, SSD_BC), F32),
                        pltpu.VMEM((lc, i0), F32), pltpu.VMEM((lc, SSD_BC), F32),
                        pltpu.VMEM((lc, SSD_BC), F32),
                        pltpu.VMEM((lc + SSD_STATE, i0), BF16), pltpu.VMEM((SSD_STATE, i0), F32)],
        compiler_params=_cparams(("parallel", "arbitrary")),
        name="ssd_scan",
    )(proj, proj, proj, proj, dtp, cwx, cwb, cwc, cbx, cbb, cbc, dtb, alog, dsk,
      norm_g.reshape(1, i0), expand, tri)


def _cplx_mul(ar, ai, br, bi):
    return ar * br - ai * bi, ar * bi + ai * br


def _zoh_coef(are, aim, delta):
    er = jnp.exp(are * delta)
    br = er * jnp.cos(aim * delta) - 1.0
    bi = er * jnp.sin(aim * delta)
    den = are * are + aim * aim
    return (br * are + bi * aim) / den, (bi * are - br * aim) / den


def _lam_pow(are, aim, delta, n):
    mag = jnp.exp((are * delta) * n)
    ang = (aim * delta) * n
    return mag * jnp.cos(ang), mag * jnp.sin(ang)


def _s5_kernel(u_ref, arc_ref, aic_ref, arr_ref, air_ref, ldt_ref,
               bT_re_ref, bT_im_ref, b_re_ref, b_im_ref, ct_re_ref, ct_im_ref, d_ref,
               o_ref,
               ma_s, mb_s, yacc_s, r0_s, s_re_s, s_im_s, xp_re_s, xp_im_s, u_s, y_s, *, nb, nchunks):
    nch, lc = S5_GROUP_CH, S5_CHUNK
    nrows = nb * nchunks
    pitch = u_s.shape[0] // nch

    def chunk_row(r):
        b, k = divmod(r, nchunks)
        return k * nb + b

    for r in range(nrows):
        u_s[pl.ds(chunk_row(r), nch, stride=pitch), :] = u_ref[:, r * lc:(r + 1) * lc]

    def u_tile(c):
        return u_s[c * pitch:c * pitch + nrows, :]
    delta = jnp.exp(ldt_ref[0])
    arc, aic = arc_ref[0], aic_ref[0]
    arr, air = arr_ref[0], air_ref[0]
    tau = lax.broadcasted_iota(jnp.int32, (1, lc), 1).astype(F32)
    p_re, p_im = _lam_pow(arc, aic, delta, tau)
    q_re, q_im = _lam_pow(arc, aic, delta, (lc - 1.0) - tau)
    cf_re_c, cf_im_c = _zoh_coef(arc, aic, delta)
    cf_re_r, cf_im_r = _zoh_coef(arr, air, delta)
    ct_re, ct_im = ct_re_ref[0], ct_im_ref[0]

    es = [_cplx_mul(ct_re[:, c:c + 1], ct_im[:, c:c + 1], p_re, p_im) for c in range(nch)]
    e_re = jnp.concatenate([e[0] for e in es], axis=1)
    e_im = jnp.concatenate([e[1] for e in es], axis=1)
    bbT_re, bbT_im = _cplx_mul(cf_re_r, cf_im_r, bT_re_ref[0], bT_im_ref[0])
    r0_s[...] = _dot(bbT_re.astype(BF16), e_re.astype(BF16)) - _dot(bbT_im.astype(BF16), e_im.astype(BF16))

    srow = lax.broadcasted_iota(jnp.int32, (lc, lc), 0)
    tcol = lax.broadcasted_iota(jnp.int32, (lc, lc), 1)
    causal = tcol >= srow

    def fill_rows(q, dst):
        for half in range(2):
            row = r0_s[pl.ds(2 * q + half, 1), :].astype(BF16).astype(F32)
            bits = lax.bitcast_convert_type(row, jnp.int32)
            rows = slice(half * lc, (half + 1) * lc)
            for c in range(0, nch, 2):
                word = (bits[:, c * lc:(c + 1) * lc]
                        | lax.shift_right_logical(bits[:, (c + 1) * lc:(c + 2) * lc], 16))
                blk = pltpu.roll(jnp.broadcast_to(word, (lc, lc)), 0, 1, stride=1, stride_axis=0)
                first = lax.bitcast_convert_type(blk & jnp.int32(-65536), F32)
                second = lax.bitcast_convert_type(lax.shift_left(blk, 16), F32)
                dst[rows, c * lc:(c + 1) * lc] = jnp.where(causal, first, 0.0).astype(BF16)
                dst[rows, (c + 1) * lc:(c + 2) * lc] = jnp.where(causal, second, 0.0).astype(BF16)

    def add_product(q, src):
        r_even = pl.multiple_of(2 * q * pitch, 8)
        r_odd = pl.multiple_of((2 * q + 1) * pitch, 8)
        lhs = jnp.concatenate([u_s[pl.ds(r_even, nrows), :], u_s[pl.ds(r_odd, nrows), :]], axis=1)
        yacc_s[...] += _dot(lhs.astype(BF16), src[...])

    nq = nch // 2
    yacc_s[...] = jnp.zeros_like(yacc_s)
    fill_rows(0, ma_s)

    def step(i, carry):
        q = 2 * i
        fill_rows(q + 1, mb_s)
        add_product(q, ma_s)
        fill_rows(jnp.minimum(q + 2, nq - 1), ma_s)
        add_product(q + 1, mb_s)
        return carry

    lax.fori_loop(0, nq // 2, step, 0)

    bb_re, bb_im = _cplx_mul(cf_re_c, cf_im_c, b_re_ref[0], b_im_ref[0])
    ws = [_cplx_mul(q_re, q_im, bb_re[:, c:c + 1], bb_im[:, c:c + 1]) for c in range(nch)]
    wt_re = jnp.concatenate([w[0] for w in ws], axis=1).astype(BF16)
    wt_im = jnp.concatenate([w[1] for w in ws], axis=1).astype(BF16)
    ub = jnp.concatenate([u_tile(c) for c in range(nch)], axis=1).astype(BF16)
    s_re_s[...] = _dot_nt(ub, wt_re)
    s_im_s[...] = _dot_nt(ub, wt_im)

    c_re, c_im = _lam_pow(arr, air, delta, float(lc))
    l1_re, l1_im = _lam_pow(arr, air, delta, 1.0)
    xr = jnp.zeros((nb, S5_STATE), F32)
    xi = jnp.zeros((nb, S5_STATE), F32)
    for k in range(nchunks):
        rows = slice(k * nb, (k + 1) * nb)
        xp_re_s[rows, :] = xr * l1_re - xi * l1_im
        xp_im_s[rows, :] = xr * l1_im + xi * l1_re
        xr, xi = (xr * c_re - xi * c_im + s_re_s[rows, :],
                  xr * c_im + xi * c_re + s_im_s[rows, :])

    y = yacc_s[...] + _dot(xp_re_s[...].astype(BF16), e_re.astype(BF16))
    y = y - _dot(xp_im_s[...].astype(BF16), e_im.astype(BF16))
    for c in range(nch):
        yc = y[:, c * lc:(c + 1) * lc] + d_ref[0, c:c + 1, :] * u_tile(c)
        y_s[c * pitch:c * pitch + nrows, :] = (
            0.5 * yc * (1.0 + jnp.tanh(math.sqrt(2.0 / math.pi) * (yc + 0.044715 * (yc * yc * yc)))))
    for r in range(nrows):
        o_ref[:, r * lc:(r + 1) * lc] = y_s[pl.ds(chunk_row(r), nch, stride=pitch), :]


def _s5(ut, bsz, a_re, a_im, log_dt, b_re, b_im, c_re, c_im, d):
    width, t = ut.shape
    l = t // bsz
    g, p, ch, lc = S5_GROUPS, S5_STATE, S5_GROUP_CH, S5_CHUNK
    nchunks = l // lc
    nrows = nchunks * bsz
    pitch = nrows + 8

    def per_group(shape):
        return pl.BlockSpec((1,) + shape, lambda i: (i, 0, 0))

    return pl.pallas_call(
        functools.partial(_s5_kernel, nb=bsz, nchunks=nchunks),
        grid=(g,),
        in_specs=[pl.BlockSpec((ch, t), lambda i: (i, 0)),
                  per_group((p, 1)), per_group((p, 1)), per_group((1, p)), per_group((1, p)),
                  per_group((1, 1)),
                  per_group((ch, p)), per_group((ch, p)),
                  per_group((p, ch)), per_group((p, ch)),
                  per_group((p, ch)), per_group((p, ch)),
                  per_group((ch, 1))],
        out_specs=pl.BlockSpec((ch, t), lambda i: (i, 0)),
        out_shape=jax.ShapeDtypeStruct((width, t), F32),
        scratch_shapes=[pltpu.VMEM((2 * lc, ch * lc), BF16), pltpu.VMEM((2 * lc, ch * lc), BF16),
                        pltpu.VMEM((nrows, ch * lc), F32), pltpu.VMEM((ch, ch * lc), F32),
                        pltpu.VMEM((nrows, p), F32), pltpu.VMEM((nrows, p), F32),
                        pltpu.VMEM((nrows, p), F32), pltpu.VMEM((nrows, p), F32),
                        pltpu.VMEM((ch * pitch, lc), F32), pltpu.VMEM((ch * pitch, lc), F32)],
        compiler_params=_cparams(("parallel",)),
        name="s5_ssm",
    )(ut,
      a_re.reshape(g, p, 1), a_im.reshape(g, p, 1), a_re.reshape(g, 1, p), a_im.reshape(g, 1, p),
      log_dt.reshape(g, 1, 1),
      b_re.transpose(0, 2, 1), b_im.transpose(0, 2, 1), b_re, b_im,
      c_re.transpose(0, 2, 1), c_im.transpose(0, 2, 1),
      d.reshape(g, ch, 1))


def _glu_kernel(yt_ref, w_ref, b_ref, o_ref):
    y = yt_ref[...].T
    gate = jax.nn.sigmoid(_dot(y.astype(BF16), w_ref[...]) + b_ref[...])
    o_ref[0] = (y * gate).astype(o_ref.dtype)


def _glu(yt, bsz, w, b, *, tm=512):
    n, t = yt.shape
    l = t // bsz
    tm = min(tm, l)
    nt = l // tm
    return pl.pallas_call(
        _glu_kernel,
        grid=(bsz, nt),
        in_specs=[pl.BlockSpec((n, tm), lambda b_, i: (0, b_ * nt + i)),
                  pl.BlockSpec((n, n), lambda b_, i: (0, 0)),
                  pl.BlockSpec((1, n), lambda b_, i: (0, 0))],
        out_specs=pl.BlockSpec((1, tm, n), lambda b_, i: (b_, i, 0)),
        out_shape=jax.ShapeDtypeStruct((bsz, l, n), BF16),
        compiler_params=_cparams(("parallel", "parallel")),
        name="s5_glu",
    )(yt, w, b.reshape(1, n))


def _ret_kernel(q_ref, k_ref, v_ref, g_ref, lg_ref, invf_ref, gn_ref, o_ref,
                r_s, d_s, cb_s, sb_s, *, tb):
    j = pl.program_id(1)
    idx = lax.broadcasted_iota(jnp.int32, (tb, 1), 0).astype(F32)

    @pl.when(j == 0)
    def _():
        r_s[...] = jnp.zeros_like(r_s)
        ri = lax.broadcasted_iota(jnp.int32, (tb, tb), 0)
        ci = lax.broadcasted_iota(jnp.int32, (tb, tb), 1)
        rc, cc = ri // RET_CHUNK, ci // RET_CHUNK
        dist = jnp.where(rc == cc, jnp.abs(ri - ci), jnp.maximum(ri - ci, 0)).astype(F32)
        for h in range(RET_HEADS):
            d_s[h] = jnp.where(cc > rc, 0.0, jnp.exp(lg_ref[h] * dist) * (RET_QK ** -0.5))
        ang = idx * invf_ref[...]
        cb_s[...] = jnp.cos(ang)
        sb_s[...] = jnp.sin(ang)

    base = (j * tb).astype(F32) * invf_ref[...]
    cos_a, sin_a = jnp.cos(base), jnp.sin(base)
    cos = cos_a * cb_s[...] - sin_a * sb_s[...]
    sin = sin_a * cb_s[...] + cos_a * sb_s[...]
    half = RET_QK // 2

    def rot(t):
        t1, t2 = t[:, :half], t[:, half:]
        return jnp.concatenate([t1 * cos - t2 * sin, t1 * sin + t2 * cos], axis=1)

    for h in range(RET_HEADS):
        lg = lg_ref[h]
        qs = slice(h * RET_QK, (h + 1) * RET_QK)
        vs = slice(h * RET_V, (h + 1) * RET_V)
        q = rot(q_ref[0, :, qs].astype(F32))
        k = rot(k_ref[0, :, qs].astype(F32))
        vb = v_ref[0, :, vs]
        xi = jnp.exp(lg * (idx + 1.0))
        zeta = jnp.exp(lg * (tb - 1.0 - idx)) * (RET_QK ** -0.5)
        gblk = jnp.exp(lg * float(tb))

        r = r_s[h]
        s = _dot_nt(q.astype(BF16), k.astype(BF16)) * d_s[h]
        o = _dot(s.astype(BF16), vb) + _dot((q * xi).astype(BF16), r.astype(BF16))
        r_s[h] = r * gblk + _dot_tn((k * zeta).astype(BF16), vb)

        mu = jnp.mean(o, axis=-1, keepdims=True)
        oc = o - mu
        y = oc * lax.rsqrt(jnp.mean(oc * oc, axis=-1, keepdims=True) + EPS)
        o_ref[0, :, vs] = (y * gn_ref[:, vs] * _silu(g_ref[0, :, vs].astype(F32))).astype(o_ref.dtype)


def _retention(proj, gn_g):
    bsz, l, _ = proj.shape
    tb = min(RET_BLOCK, l)
    hq = RET_HEADS
    dq, dv = hq * RET_QK, hq * RET_V
    log_g = jnp.log(1.0 - 2.0 ** (-5.0 - jnp.arange(hq, dtype=F32))).reshape(hq, 1, 1)
    inv_freq = (ROPE_BASE ** (-jnp.arange(0, RET_QK, 2, dtype=F32) / RET_QK)).reshape(1, RET_QK // 2)
    return pl.pallas_call(
        functools.partial(_ret_kernel, tb=tb),
        grid=(bsz, l // tb),
        in_specs=[pl.BlockSpec((1, tb, dq), lambda b, j: (b, j, 0)),
                  pl.BlockSpec((1, tb, dq), lambda b, j: (b, j, 1)),
                  pl.BlockSpec((1, tb, dv), lambda b, j: (b, j, 1)),
                  pl.BlockSpec((1, tb, dv), lambda b, j: (b, j, 2)),
                  pl.BlockSpec((hq, 1, 1), lambda b, j: (0, 0, 0)),
                  pl.BlockSpec((1, RET_QK // 2), lambda b, j: (0, 0)),
                  pl.BlockSpec((1, dv), lambda b, j: (0, 0))],
        out_specs=pl.BlockSpec((1, tb, dv), lambda b, j: (b, j, 0)),
        out_shape=jax.ShapeDtypeStruct((bsz, l, dv), BF16),
        scratch_shapes=[pltpu.VMEM((hq, RET_QK, RET_V), F32), pltpu.VMEM((hq, tb, tb), F32),
                        pltpu.VMEM((tb, RET_QK // 2), F32), pltpu.VMEM((tb, RET_QK // 2), F32)],
        compiler_params=_cparams(("parallel", "arbitrary")),
        name="retention",
    )(proj, proj, proj, proj, log_g, inv_freq, gn_g.reshape(1, -1))


def _split_mod(m, bsz):
    d = m.shape[-1] // 3
    return tuple(m[:, k * d:(k + 1) * d].reshape(bsz, 1, d) for k in range(3))


def kernel(x, c, norm_mix_g, ada_mix_w, ada_mix_b, norm_ffn_g, ada_ffn_w, ada_ffn_b, ffn_w_in, ffn_w_out, ab_w_in, ssd_conv_w, ssd_conv_b, ssd_dt_bias, ssd_a_log, ssd_d, ssd_norm_g, s5_a_re, s5_a_im, s5_log_dt, s5_b_re, s5_b_im, s5_c_re, s5_c_im, s5_d, s5_glu_w, s5_glu_b, ab_w_out, ret_w_in, ret_gn_g, ret_w_out, final_norm_g):
    bsz = x.shape[0]
    mod_mix = _ada_modulation(c, ada_mix_w, ada_mix_b)
    mod_ffn = _ada_modulation(c, ada_ffn_w, ada_ffn_b)

    shift, scale, gate = _split_mod(mod_mix[0], bsz)
    w0 = ab_w_in.astype(BF16)
    n_main = SSD_INNER + SSD_XBC
    w_dt = jnp.pad(w0[0, :, n_main:n_main + SSD_HEADS], ((0, 0), (0, LANES - SSD_HEADS)))
    w_u = w0[0, :, n_main + SSD_HEADS:]
    proj0, dtp, ut = _in_proj0(x, norm_mix_g[0], scale, shift, w0, 0, n_main, w_dt, w_u)
    y_a = _ssd(proj0, dtp, ssd_conv_w[0], ssd_conv_b[0], ssd_dt_bias[0], ssd_a_log[0], ssd_d[0],
               ssd_norm_g[0])
    yt = _s5(ut, bsz, s5_a_re[0], s5_a_im[0], s5_log_dt[0], s5_b_re[0], s5_b_im[0],
             s5_c_re[0], s5_c_im[0], s5_d[0])
    y_b = _glu(yt, bsz, s5_glu_w[0].astype(BF16), s5_glu_b[0])
    x = _matmul_residual([y_a, y_b], ab_w_out[0].astype(BF16), x, gate, name="out_proj0")

    shift, scale, gate = _split_mod(mod_ffn[0], bsz)
    ffn_w_in_b, ffn_w_out_b = ffn_w_in.astype(BF16), ffn_w_out.astype(BF16)
    x = _ffn(x, norm_ffn_g[0], scale, shift, gate, ffn_w_in_b, ffn_w_out_b, 0, final_norm_g,
             final_norm=False, name="ffn0")

    shift, scale, gate = _split_mod(mod_mix[1], bsz)
    proj1 = _norm_matmul(x, norm_mix_g[1], scale, shift, ret_w_in.astype(BF16), 0, name="in_proj1")
    y_r = _retention(proj1, ret_gn_g[0])
    x = _matmul_residual([y_r], ret_w_out[0].astype(BF16), x, gate, name="out_proj1")

    shift, scale, gate = _split_mod(mod_ffn[1], bsz)
    return _ffn(x, norm_ffn_g[1], scale, shift, gate, ffn_w_in_b, ffn_w_out_b, 1, final_norm_g,
                final_norm=True, name="ffn1")
```

```python
import functools
import math

import numpy as np
import jax
import jax.numpy as jnp
from jax import lax
from jax.experimental import pallas as pl
from jax.experimental.pallas import tpu as pltpu

F32 = jnp.float32
BF16 = jnp.bfloat16
EPS = 1e-6

D_MODEL = 2048
MIX_WIDTH = 2 * D_MODEL
SSD_HEADDIM = 64
SSD_INNER = 3 * MIX_WIDTH // 4
SSD_HEADS = SSD_INNER // SSD_HEADDIM
SSD_GROUPS = 8
SSD_HPG = SSD_HEADS // SSD_GROUPS
SSD_STATE = 128
SSD_CONV = 4
SSD_GW = SSD_HPG * SSD_HEADDIM
SSD_BC = SSD_GROUPS * SSD_STATE
SSD_XBC = SSD_INNER + 2 * SSD_BC
S5_WIDTH = MIX_WIDTH - SSD_INNER
S5_GROUP_CH = 16
S5_GROUPS = S5_WIDTH // S5_GROUP_CH
S5_STATE = 64
S5_CHUNK = 128
RET_HEADS = 8
RET_QK = D_MODEL // RET_HEADS
RET_V = MIX_WIDTH // RET_HEADS
RET_CHUNK = 64
RET_BLOCK = 256
ROPE_BASE = 10000.0
FFN_HIDDEN = ((-(-8 * D_MODEL // 3) + 255) // 256) * 256

LANES = 128
SSD_LC = 128
VMEM_LIMIT = 56 * 1024 * 1024


def _cparams(sem):
    return pltpu.CompilerParams(dimension_semantics=sem, vmem_limit_bytes=VMEM_LIMIT)


def _dot(a, b):
    return jnp.dot(a, b, preferred_element_type=F32)


def _dot_nt(a, b):
    return lax.dot_general(a, b, (((1,), (1,)), ((), ())), preferred_element_type=F32)


def _dot_tn(a, b):
    return lax.dot_general(a, b, (((0,), (0,)), ((), ())), preferred_element_type=F32)


def _split3(v):
    hi = v.astype(BF16)
    r = v - hi.astype(F32)
    mid = r.astype(BF16)
    lo = (r - mid.astype(F32)).astype(BF16)
    return hi, mid, lo


def _sel_right(v, e):
    hi, mid, lo = _split3(v)
    return (_dot(lo, e) + _dot(mid, e)) + _dot(hi, e)


def _sel_left(e, v):
    hi, mid, lo = _split3(v)
    return (_dot(e, lo) + _dot(e, mid)) + _dot(e, hi)


def _silu(v):
    h = 0.5 * v
    return h + h * jnp.tanh(h)


NORM_ROWS = 32


def _store_modulated_norm(x_ref, g_ref, sc_ref, sh_ref, hn_ref):
    gm = g_ref[...] * (1.0 + sc_ref[0])
    sh = sh_ref[0]

    def body(i, carry):
        r0 = pl.multiple_of(i * NORM_ROWS, NORM_ROWS)
        x = x_ref[0, pl.ds(r0, NORM_ROWS), :]
        ms = jnp.mean(x * x, axis=-1, keepdims=True)
        hn_ref[pl.ds(r0, NORM_ROWS), :] = ((x * lax.rsqrt(ms + EPS)) * gm + sh).astype(BF16)
        return carry

    lax.fori_loop(0, hn_ref.shape[0] // NORM_ROWS, body, 0, unroll=4)


def _ada_kernel(c_ref, w_ref, b_ref, o_ref):
    sc = _silu(c_ref[...]).astype(BF16)
    o_ref[0] = _dot(sc, w_ref[0].astype(BF16)) + b_ref[0]


def _ada_modulation(c, w, b):
    depth, d, n = w.shape
    bsz = c.shape[0]
    tn = 1024
    return pl.pallas_call(
        _ada_kernel,
        grid=(depth, n // tn),
        in_specs=[pl.BlockSpec((bsz, d), lambda i, j: (0, 0)),
                  pl.BlockSpec((1, d, tn), lambda i, j: (i, 0, j)),
                  pl.BlockSpec((1, 1, tn), lambda i, j: (i, 0, j))],
        out_specs=pl.BlockSpec((1, bsz, tn), lambda i, j: (i, 0, j)),
        out_shape=jax.ShapeDtypeStruct((depth, bsz, n), F32),
        compiler_params=_cparams(("parallel", "parallel")),
        name="ada_modulation",
    )(c, w, b.reshape(depth, 1, n))


def _norm_mm_kernel(x_ref, g_ref, sc_ref, sh_ref, w_ref, o_ref, hn_ref):
    @pl.when(pl.program_id(2) == 0)
    def _():
        _store_modulated_norm(x_ref, g_ref, sc_ref, sh_ref, hn_ref)

    o_ref[0] = _dot(hn_ref[...], w_ref[...]).astype(o_ref.dtype)


def _in_proj0_kernel(x_ref, g_ref, sc_ref, sh_ref, w_ref, wdt_ref, wu_ref, o_ref, dt_ref, ut_ref, hn_ref,
                     *, n_ut):
    n = pl.program_id(2)

    @pl.when(n == 0)
    def _():
        _store_modulated_norm(x_ref, g_ref, sc_ref, sh_ref, hn_ref)
        dt_ref[0] = _dot(hn_ref[...], wdt_ref[...])

    @pl.when(n < n_ut)
    def _():
        ut_ref[...] = _dot(hn_ref[...], wu_ref[...]).T

    @pl.when(n >= n_ut)
    def _():
        o_ref[0] = _dot(hn_ref[...], w_ref[...]).astype(o_ref.dtype)


def _norm_matmul(x, g, scale, shift, w, layer, *, tm=1024, tn=1024, name):
    bsz, l, d = x.shape
    n = w.shape[2]
    tm = min(tm, l)
    nt, nj = l // tm, n // tn

    def col(b, i, j):
        return _snake(b * nt + i, j, nj)

    return pl.pallas_call(
        _norm_mm_kernel,
        grid=(bsz, nt, nj),
        in_specs=[pl.BlockSpec((1, tm, d), lambda b, i, j: (b, i, 0)),
                  pl.BlockSpec((1, d), lambda b, i, j: (0, 0)),
                  pl.BlockSpec((1, 1, d), lambda b, i, j: (b, 0, 0)),
                  pl.BlockSpec((1, 1, d), lambda b, i, j: (b, 0, 0)),
                  pl.BlockSpec((None, d, tn), lambda b, i, j: (layer, 0, col(b, i, j)))],
        out_specs=pl.BlockSpec((1, tm, tn), lambda b, i, j: (b, i, col(b, i, j))),
        out_shape=jax.ShapeDtypeStruct((bsz, l, n), BF16),
        scratch_shapes=[pltpu.VMEM((tm, d), BF16)],
        compiler_params=_cparams(("parallel", "parallel", "arbitrary")),
        name=name,
    )(x, g.reshape(1, d), scale, shift, w)


def _in_proj0(x, g, scale, shift, w, layer, n, w_dt, w_u, *, tm=1024, tn=1024, tc=512):
    bsz, l, d = x.shape
    ndt, nut = w_dt.shape[1], w_u.shape[1]
    tm = min(tm, l)
    n_main, n_ut, nt = n // tn, nut // tc, l // tm

    def main_tile(b, i, j):
        return _snake(b * nt + i, jnp.clip(j - n_ut, 0, n_main - 1), n_main)

    return pl.pallas_call(
        functools.partial(_in_proj0_kernel, n_ut=n_ut),
        grid=(bsz, nt, n_ut + n_main),
        in_specs=[pl.BlockSpec((1, tm, d), lambda b, i, j: (b, i, 0)),
                  pl.BlockSpec((1, d), lambda b, i, j: (0, 0)),
                  pl.BlockSpec((1, 1, d), lambda b, i, j: (b, 0, 0)),
                  pl.BlockSpec((1, 1, d), lambda b, i, j: (b, 0, 0)),
                  pl.BlockSpec((None, d, tn), lambda b, i, j: (layer, 0, main_tile(b, i, j))),
                  pl.BlockSpec((d, ndt), lambda b, i, j: (0, 0)),
                  pl.BlockSpec((d, tc), lambda b, i, j: (0, jnp.minimum(j, n_ut - 1)))],
        out_specs=[pl.BlockSpec((1, tm, tn), lambda b, i, j: (b, i, main_tile(b, i, j))),
                   pl.BlockSpec((1, tm, ndt), lambda b, i, j: (b, i, 0)),
                   pl.BlockSpec((tc, tm), lambda b, i, j: (jnp.minimum(j, n_ut - 1), b * nt + i))],
        out_shape=[jax.ShapeDtypeStruct((bsz, l, n), BF16),
                   jax.ShapeDtypeStruct((bsz, l, ndt), F32),
                   jax.ShapeDtypeStruct((nut, bsz * l), F32)],
        scratch_shapes=[pltpu.VMEM((tm, d), BF16)],
        compiler_params=_cparams(("parallel", "parallel", "arbitrary")),
        name="in_proj0",
    )(x, g.reshape(1, d), scale, shift, w, w_dt, w_u)


def _snake(tile, j, n):
    return jnp.where(tile % 2 == 0, j, n - 1 - j)


def _mm_res_kernel(*refs, bounds, nt):
    na = len(bounds) - 1
    a_refs = refs[:na]
    w_ref, x_ref, gate_ref, o_ref = refs[na:]
    step = pl.program_id(2)
    k = _snake(pl.program_id(0) * nt + pl.program_id(1), step, bounds[-1])

    for i, a_ref in enumerate(a_refs):
        mine = (k >= bounds[i]) & (k < bounds[i + 1])

        @pl.when(mine & (step == 0))
        def _(a_ref=a_ref):
            o_ref[0] = _dot(a_ref[0], w_ref[...])

        @pl.when(mine & (step > 0))
        def _(a_ref=a_ref):
            o_ref[0] += _dot(a_ref[0], w_ref[...])

    @pl.when(step == bounds[-1] - 1)
    def _():
        o_ref[0] = x_ref[0] + gate_ref[0] * o_ref[0]


def _matmul_residual(a_list, w, x, gate, *, tm=1024, tk=1024, name):
    bsz, l, d = x.shape
    kk, n = w.shape
    tm = min(tm, l)
    bounds = [0]
    for a in a_list:
        bounds.append(bounds[-1] + a.shape[2] // tk)
    assert bounds[-1] * tk == kk

    nt, nk = l // tm, bounds[-1]

    def ktile(b, i, k):
        return _snake(b * nt + i, k, nk)

    def a_spec(lo, hi):
        return pl.BlockSpec((1, tm, tk), lambda b, i, k: (b, i, jnp.clip(ktile(b, i, k) - lo, 0, hi - lo - 1)))

    return pl.pallas_call(
        functools.partial(_mm_res_kernel, bounds=tuple(bounds), nt=nt),
        grid=(bsz, nt, nk),
        in_specs=[a_spec(bounds[i], bounds[i + 1]) for i in range(len(a_list))] + [
            pl.BlockSpec((tk, n), lambda b, i, k: (ktile(b, i, k), 0)),
            pl.BlockSpec((1, tm, d), lambda b, i, k: (b, i, 0)),
            pl.BlockSpec((1, 1, d), lambda b, i, k: (b, 0, 0))],
        out_specs=pl.BlockSpec((1, tm, n), lambda b, i, k: (b, i, 0)),
        out_shape=jax.ShapeDtypeStruct((bsz, l, n), F32),
        compiler_params=_cparams(("parallel", "parallel", "arbitrary")),
        name=name,
    )(*a_list, w, x, gate)


def _ffn_kernel(x_ref, g_ref, sc_ref, sh_ref, gate_ref, wg_ref, wu_ref, wo_ref, fg_ref,
                o_ref, hn_ref, *, nh, final_norm):
    h = pl.program_id(2)

    @pl.when(h == 0)
    def _():
        _store_modulated_norm(x_ref, g_ref, sc_ref, sh_ref, hn_ref)
        o_ref[...] = jnp.zeros_like(o_ref)

    hn = hn_ref[...]
    gt = _dot(hn, wg_ref[...])
    up = _dot(hn, wu_ref[...])
    act = (_silu(gt) * up).astype(BF16)
    o_ref[0] += _dot(act, wo_ref[...])

    @pl.when(h == nh - 1)
    def _():
        gate = gate_ref[0]

        def body(i, carry):
            rows = pl.ds(pl.multiple_of(i * NORM_ROWS, NORM_ROWS), NORM_ROWS)
            y = x_ref[0, rows, :] + gate * o_ref[0, rows, :]
            if final_norm:
                ms = jnp.mean(y * y, axis=-1, keepdims=True)
                y = y * lax.rsqrt(ms + EPS) * fg_ref[...]
            o_ref[0, rows, :] = y
            return carry

        lax.fori_loop(0, hn_ref.shape[0] // NORM_ROWS, body, 0, unroll=4)


def _ffn(x, g, scale, shift, gate, w_in, w_out, layer, final_g, *, final_norm, tm=1024, th=512, name):
    bsz, l, d = x.shape
    hid = w_out.shape[1]
    tm = min(tm, l)
    nh = hid // th
    nt = l // tm
    vec = pl.BlockSpec((1, 1, d), lambda b, i, h: (b, 0, 0))

    def hid_tile(b, i, h):
        return _snake(b * nt + i, h, nh)

    return pl.pallas_call(
        functools.partial(_ffn_kernel, nh=nh, final_norm=final_norm),
        grid=(bsz, l // tm, nh),
        in_specs=[pl.BlockSpec((1, tm, d), lambda b, i, h: (b, i, 0)),
                  pl.BlockSpec((1, d), lambda b, i, h: (0, 0)),
                  vec, vec, vec,
                  pl.BlockSpec((None, d, th), lambda b, i, h: (layer, 0, hid_tile(b, i, h))),
                  pl.BlockSpec((None, d, th), lambda b, i, h: (layer, 0, hid_tile(b, i, h) + nh)),
                  pl.BlockSpec((None, th, d), lambda b, i, h: (layer, hid_tile(b, i, h), 0)),
                  pl.BlockSpec((1, d), lambda b, i, h: (0, 0))],
        out_specs=pl.BlockSpec((1, tm, d), lambda b, i, h: (b, i, 0)),
        out_shape=jax.ShapeDtypeStruct((bsz, l, d), F32),
        scratch_shapes=[pltpu.VMEM((tm, d), BF16)],
        compiler_params=_cparams(("parallel", "parallel", "arbitrary")),
        name=name,
    )(x, g.reshape(1, d), scale, shift, gate, w_in, w_in, w_out, final_g.reshape(1, d))


def _conv_silu(raw_ref, pad_ref, w_ref, b_ref, out_ref, first, lc, width, cb=256):
    @pl.when(first)
    def _():
        pad_ref[0:8, :] = jnp.zeros((8, width), F32)

    pad_ref[8:8 + lc, :] = raw_ref[0].astype(F32)
    ntap = SSD_CONV
    for c0 in range(0, width, cb):
        cs = slice(c0, c0 + cb)
        blk = pad_ref[:, cs]
        acc = b_ref[:, cs] + w_ref[ntap - 1:ntap, cs] * blk[8:]
        for j in range(1, ntap):
            acc = acc + w_ref[ntap - 1 - j:ntap - j, cs] * pltpu.roll(blk, j, 0)[8:]
        out_ref[:, cs] = _silu(acc)
    pad_ref[0:8, :] = pad_ref[lc:lc + 8, :]


def _ssd_kernel(z_ref, xs_ref, bm_ref, cm_ref, dt_ref,
                cwx_ref, cwb_ref, cwc_ref, cbx_ref, cbb_ref, cbc_ref,
                dtb_ref, alog_ref, dsk_ref, ng_ref, e_ref, tri_ref,
                o_ref,
                padx_s, padb_s, padc_s, xs_s, bm_s, cm_s, rhs_s, st_s, *, lc):
    first = pl.program_id(1) == 0

    @pl.when(first)
    def _():
        st_s[...] = jnp.zeros_like(st_s)

    _conv_silu(xs_ref, padx_s, cwx_ref, cbx_ref, xs_s, first, lc, SSD_INNER)
    _conv_silu(bm_ref, padb_s, cwb_ref, cbb_ref, bm_s, first, lc, SSD_BC)
    _conv_silu(cm_ref, padc_s, cwc_ref, cbc_ref, cm_s, first, lc, SSD_BC)

    rhs_s[0:lc, :] = xs_s[...].astype(BF16)
    rhs_s[lc:lc + SSD_STATE, :] = st_s[...].astype(BF16)

    v = dt_ref[0] + dtb_ref[...]
    dt = jnp.maximum(v, 0.0) + jnp.log(1.0 + jnp.exp(-jnp.abs(v)))
    a = -jnp.exp(alog_ref[...])
    acum = _sel_left(tri_ref[...], dt * a)
    acum_last = acum[lc - 1:lc, :]
    wdec = jnp.exp(acum_last - acum) * dt
    eacum = jnp.exp(acum)
    acum_t = acum.T
    dt_t = dt.T
    wexp = _sel_right(wdec, e_ref[...])
    eal = _sel_right(jnp.broadcast_to(jnp.exp(acum_last), (8, LANES)), e_ref[...])[0:1, :]

    row = lax.broadcasted_iota(jnp.int32, (lc, lc), 0)
    col = lax.broadcasted_iota(jnp.int32, (lc, lc), 1)
    causal = row >= col
    lane = lax.broadcasted_iota(jnp.int32, (lc, LANES), 1)
    low_half = lane < SSD_HEADDIM

    for g in range(SSD_GROUPS):
        gs = slice(g * SSD_STATE, (g + 1) * SSD_STATE)
        bm_g = bm_s[:, gs]
        cm_g = cm_s[:, gs]
        bm_gb = bm_g.astype(BF16)
        scores = _dot_nt(cm_g.astype(BF16), bm_gb)
        pieces = []
        for j in range(SSD_HPG // 2):
            ps = slice(g * SSD_GW + j * LANES, g * SSD_GW + (j + 1) * LANES)
            rhs = rhs_s[:, ps]
            ys = []
            for hh in range(2):
                h = g * SSD_HPG + 2 * j + hh
                seg = acum[:, h:h + 1] - acum_t[h:h + 1, :]
                lmat = jnp.exp(jnp.where(causal, seg, -1e30))
                amat = scores * lmat * dt_t[h:h + 1, :]
                cs = cm_g * eacum[:, h:h + 1]
                lhs = jnp.concatenate([amat, cs], axis=1).astype(BF16)
                ys.append(_dot(lhs, rhs))
            pieces.append(jnp.where(low_half, ys[0], ys[1]))
        y = jnp.concatenate(pieces, axis=1)

        ws = slice(g * SSD_GW, (g + 1) * SSD_GW)
        xs_g = xs_s[:, ws]
        xw = (xs_g * wexp[:, ws]).astype(BF16)
        st_s[:, ws] = st_s[:, ws] * eal[:, ws] + _dot_tn(bm_gb, xw)

        y = y + dsk_ref[:, ws] * xs_g
        y = y * _silu(z_ref[0, :, ws].astype(F32))
        ms = jnp.mean(y * y, axis=-1, keepdims=True)
        o_ref[0, :, ws] = (y * lax.rsqrt(ms + EPS) * ng_ref[:, ws]).astype(o_ref.dtype)


def _ssd(proj, dtp, conv_w, conv_b, dt_bias, a_log, d_skip, norm_g):
    bsz, l, _ = proj.shape
    lc = min(SSD_LC, l)
    i0 = SSD_INNER
    cwx, cwb, cwc = conv_w[:, :i0], conv_w[:, i0:i0 + SSD_BC], conv_w[:, i0 + SSD_BC:]
    cb = conv_b.reshape(1, -1)
    cbx, cbb, cbc = cb[:, :i0], cb[:, i0:i0 + SSD_BC], cb[:, i0 + SSD_BC:]
    pad = LANES - SSD_HEADS
    dtb = jnp.pad(dt_bias, (0, pad)).reshape(1, LANES)
    alog = jnp.pad(a_log, (0, pad)).reshape(1, LANES)
    dsk = jnp.repeat(d_skip, SSD_HEADDIM).reshape(1, i0)
    head_of_lane = np.arange(i0) // SSD_HEADDIM
    expand = jnp.asarray(np.arange(LANES)[:, None] == head_of_lane[None, :], BF16)
    tri = jnp.asarray(np.tril(np.ones((lc, lc))), BF16)

    def full(shape):
        return pl.BlockSpec(shape, lambda b, k: (0,) * len(shape))

    nb3 = i0 // SSD_BC
    return pl.pallas_call(
        functools.partial(_ssd_kernel, lc=lc),
        grid=(bsz, l // lc),
        in_specs=[pl.BlockSpec((1, lc, i0), lambda b, k: (b, k, 0)),
                  pl.BlockSpec((1, lc, i0), lambda b, k: (b, k, 1)),
                  pl.BlockSpec((1, lc, SSD_BC), lambda b, k: (b, k, 2 * nb3)),
                  pl.BlockSpec((1, lc, SSD_BC), lambda b, k: (b, k, 2 * nb3 + 1)),
                  pl.BlockSpec((1, lc, LANES), lambda b, k: (b, k, 0)),
                  full((SSD_CONV, i0)), full((SSD_CONV, SSD_BC)), full((SSD_CONV, SSD_BC)),
                  full((1, i0)), full((1, SSD_BC)), full((1, SSD_BC)),
                  full((1, LANES)), full((1, LANES)), full((1, i0)), full((1, i0)),
                  full((LANES, i0)), full((lc, lc))],
        out_specs=pl.BlockSpec((1, lc, i0), lambda b, k: (b, k, 0)),
        out_shape=jax.ShapeDtypeStruct((bsz, l, i0), BF16),
        scratch_shapes=[pltpu.VMEM((lc + 8, i0), F32), pltpu.VMEM((lc + 8, SSD_BC), F32),
                        pltpu.VMEM((lc + 8, SSD_BC), F32),
                        pltpu.VMEM((lc, i0), F32), pltpu.VMEM((lc, SSD_BC), F32),
                        pltpu.VMEM((lc, SSD_BC), F32),
                        pltpu.VMEM((lc + SSD_STATE, i0), BF16), pltpu.VMEM((SSD_STATE, i0), F32)],
        compiler_params=_cparams(("parallel", "arbitrary")),
        name="ssd_scan",
    )(proj, proj, proj, proj, dtp, cwx, cwb, cwc, cbx, cbb, cbc, dtb, alog, dsk,
      norm_g.reshape(1, i0), expand, tri)


def _cplx_mul(ar, ai, br, bi):
    return ar * br - ai * bi, ar * bi + ai * br


def _zoh_coef(are, aim, delta):
    er = jnp.exp(are * delta)
    br = er * jnp.cos(aim * delta) - 1.0
    bi = er * jnp.sin(aim * delta)
    den = are * are + aim * aim
    return (br * are + bi * aim) / den, (bi * are - br * aim) / den


def _lam_pow(are, aim, delta, n):
    mag = jnp.exp((are * delta) * n)
    ang = (aim * delta) * n
    return mag * jnp.cos(ang), mag * jnp.sin(ang)


def _s5_kernel(u_ref, arc_ref, aic_ref, arr_ref, air_ref, ldt_ref,
               bT_re_ref, bT_im_ref, b_re_ref, b_im_ref, ct_re_ref, ct_im_ref, d_ref,
               o_ref,
               ma_s, mb_s, yacc_s, r0_s, s_re_s, s_im_s, xp_re_s, xp_im_s, u_s, y_s, *, nb, nchunks):
    nch, lc = S5_GROUP_CH, S5_CHUNK
    nrows = nb * nchunks
    pitch = u_s.shape[0] // nch

    def chunk_row(r):
        b, k = divmod(r, nchunks)
        return k * nb + b

    for r in range(nrows):
        u_s[pl.ds(chunk_row(r), nch, stride=pitch), :] = u_ref[:, r * lc:(r + 1) * lc]

    def u_tile(c):
        return u_s[c * pitch:c * pitch + nrows, :]
    delta = jnp.exp(ldt_ref[0])
    arc, aic = arc_ref[0], aic_ref[0]
    arr, air = arr_ref[0], air_ref[0]
    tau = lax.broadcasted_iota(jnp.int32, (1, lc), 1).astype(F32)
    p_re, p_im = _lam_pow(arc, aic, delta, tau)
    q_re, q_im = _lam_pow(arc, aic, delta, (lc - 1.0) - tau)
    cf_re_c, cf_im_c = _zoh_coef(arc, aic, delta)
    cf_re_r, cf_im_r = _zoh_coef(arr, air, delta)
    ct_re, ct_im = ct_re_ref[0], ct_im_ref[0]

    es = [_cplx_mul(ct_re[:, c:c + 1], ct_im[:, c:c + 1], p_re, p_im) for c in range(nch)]
    e_re = jnp.concatenate([e[0] for e in es], axis=1)
    e_im = jnp.concatenate([e[1] for e in es], axis=1)
    bbT_re, bbT_im = _cplx_mul(cf_re_r, cf_im_r, bT_re_ref[0], bT_im_ref[0])
    r0_s[...] = _dot(bbT_re.astype(BF16), e_re.astype(BF16)) - _dot(bbT_im.astype(BF16), e_im.astype(BF16))

    srow = lax.broadcasted_iota(jnp.int32, (lc, lc), 0)
    tcol = lax.broadcasted_iota(jnp.int32, (lc, lc), 1)
    causal = tcol >= srow

    def fill_rows(q, dst):
        for half in range(2):
            row = r0_s[pl.ds(2 * q + half, 1), :].astype(BF16).astype(F32)
            bits = lax.bitcast_convert_type(row, jnp.int32)
            rows = slice(half * lc, (half + 1) * lc)
            for c in range(0, nch, 2):
                word = (bits[:, c * lc:(c + 1) * lc]
                        | lax.shift_right_logical(bits[:, (c + 1) * lc:(c + 2) * lc], 16))
                blk = pltpu.roll(jnp.broadcast_to(word, (lc, lc)), 0, 1, stride=1, stride_axis=0)
                first = lax.bitcast_convert_type(blk & jnp.int32(-65536), F32)
                second = lax.bitcast_convert_type(lax.shift_left(blk, 16), F32)
                dst[rows, c * lc:(c + 1) * lc] = jnp.where(causal, first, 0.0).astype(BF16)
                dst[rows, (c + 1) * lc:(c + 2) * lc] = jnp.where(causal, second, 0.0).astype(BF16)

    def add_product(q, src):
        r_even = pl.multiple_of(2 * q * pitch, 8)
        r_odd = pl.multiple_of((2 * q + 1) * pitch, 8)
        lhs = jnp.concatenate([u_s[pl.ds(r_even, nrows), :], u_s[pl.ds(r_odd, nrows), :]], axis=1)
        yacc_s[...] += _dot(lhs.astype(BF16), src[...])

    nq = nch // 2
    yacc_s[...] = jnp.zeros_like(yacc_s)
    fill_rows(0, ma_s)

    def step(i, carry):
        q = 2 * i
        fill_rows(q + 1, mb_s)
        add_product(q, ma_s)
        fill_rows(jnp.minimum(q + 2, nq - 1), ma_s)
        add_product(q + 1, mb_s)
        return carry

    lax.fori_loop(0, nq // 2, step, 0)

    bb_re, bb_im = _cplx_mul(cf_re_c, cf_im_c, b_re_ref[0], b_im_ref[0])
    ws = [_cplx_mul(q_re, q_im, bb_re[:, c:c + 1], bb_im[:, c:c + 1]) for c in range(nch)]
    wt_re = jnp.concatenate([w[0] for w in ws], axis=1).astype(BF16)
    wt_im = jnp.concatenate([w[1] for w in ws], axis=1).astype(BF16)
    ub = jnp.concatenate([u_tile(c) for c in range(nch)], axis=1).astype(BF16)
    s_re_s[...] = _dot_nt(ub, wt_re)
    s_im_s[...] = _dot_nt(ub, wt_im)

    c_re, c_im = _lam_pow(arr, air, delta, float(lc))
    l1_re, l1_im = _lam_pow(arr, air, delta, 1.0)
    xr = jnp.zeros((nb, S5_STATE), F32)
    xi = jnp.zeros((nb, S5_STATE), F32)
    for k in range(nchunks):
        rows = slice(k * nb, (k + 1) * nb)
        xp_re_s[rows, :] = xr * l1_re - xi * l1_im
        xp_im_s[rows, :] = xr * l1_im + xi * l1_re
        xr, xi = (xr * c_re - xi * c_im + s_re_s[rows, :],
                  xr * c_im + xi * c_re + s_im_s[rows, :])

    y = yacc_s[...] + _dot(xp_re_s[...].astype(BF16), e_re.astype(BF16))
    y = y - _dot(xp_im_s[...].astype(BF16), e_im.astype(BF16))
    for c in range(nch):
        yc = y[:, c * lc:(c + 1) * lc] + d_ref[0, c:c + 1, :] * u_tile(c)
        y_s[c * pitch:c * pitch + nrows, :] = (
            0.5 * yc * (1.0 + jnp.tanh(math.sqrt(2.0 / math.pi) * (yc + 0.044715 * (yc * yc * yc)))))
    for r in range(nrows):
        o_ref[:, r * lc:(r + 1) * lc] = y_s[pl.ds(chunk_row(r), nch, stride=pitch), :]


def _s5(ut, bsz, a_re, a_im, log_dt, b_re, b_im, c_re, c_im, d):
    width, t = ut.shape
    l = t // bsz
    g, p, ch, lc = S5_GROUPS, S5_STATE, S5_GROUP_CH, S5_CHUNK
    nchunks = l // lc
    nrows = nchunks * bsz
    pitch = nrows + 8

    def per_group(shape):
        return pl.BlockSpec((1,) + shape, lambda i: (i, 0, 0))

    return pl.pallas_call(
        functools.partial(_s5_kernel, nb=bsz, nchunks=nchunks),
        grid=(g,),
        in_specs=[pl.BlockSpec((ch, t), lambda i: (i, 0)),
                  per_group((p, 1)), per_group((p, 1)), per_group((1, p)), per_group((1, p)),
                  per_group((1, 1)),
                  per_group((ch, p)), per_group((ch, p)),
                  per_group((p, ch)), per_group((p, ch)),
                  per_group((p, ch)), per_group((p, ch)),
                  per_group((ch, 1))],
        out_specs=pl.BlockSpec((ch, t), lambda i: (i, 0)),
        out_shape=jax.ShapeDtypeStruct((width, t), F32),
        scratch_shapes=[pltpu.VMEM((2 * lc, ch * lc), BF16), pltpu.VMEM((2 * lc, ch * lc), BF16),
                        pltpu.VMEM((nrows, ch * lc), F32), pltpu.VMEM((ch, ch * lc), F32),
                        pltpu.VMEM((nrows, p), F32), pltpu.VMEM((nrows, p), F32),
                        pltpu.VMEM((nrows, p), F32), pltpu.VMEM((nrows, p), F32),
                        pltpu.VMEM((ch * pitch, lc), F32), pltpu.VMEM((ch * pitch, lc), F32)],
        compiler_params=_cparams(("parallel",)),
        name="s5_ssm",
    )(ut,
      a_re.reshape(g, p, 1), a_im.reshape(g, p, 1), a_re.reshape(g, 1, p), a_im.reshape(g, 1, p),
      log_dt.reshape(g, 1, 1),
      b_re.transpose(0, 2, 1), b_im.transpose(0, 2, 1), b_re, b_im,
      c_re.transpose(0, 2, 1), c_im.transpose(0, 2, 1),
      d.reshape(g, ch, 1))


def _glu_kernel(yt_ref, w_ref, b_ref, o_ref):
    y = yt_ref[...].T
    gate = jax.nn.sigmoid(_dot(y.astype(BF16), w_ref[...]) + b_ref[...])
    o_ref[0] = (y * gate).astype(o_ref.dtype)


def _glu(yt, bsz, w, b, *, tm=512):
    n, t = yt.shape
    l = t // bsz
    tm = min(tm, l)
    nt = l // tm
    return pl.pallas_call(
        _glu_kernel,
        grid=(bsz, nt),
        in_specs=[pl.BlockSpec((n, tm), lambda b_, i: (0, b_ * nt + i)),
                  pl.BlockSpec((n, n), lambda b_, i: (0, 0)),
                  pl.BlockSpec((1, n), lambda b_, i: (0, 0))],
        out_specs=pl.BlockSpec((1, tm, n), lambda b_, i: (b_, i, 0)),
        out_shape=jax.ShapeDtypeStruct((bsz, l, n), BF16),
        compiler_params=_cparams(("parallel", "parallel")),
        name="s5_glu",
    )(yt, w, b.reshape(1, n))


def _ret_kernel(q_ref, k_ref, v_ref, g_ref, lg_ref, invf_ref, gn_ref, o_ref,
                r_s, d_s, cb_s, sb_s, *, tb):
    j = pl.program_id(1)
    idx = lax.broadcasted_iota(jnp.int32, (tb, 1), 0).astype(F32)

    @pl.when(j == 0)
    def _():
        r_s[...] = jnp.zeros_like(r_s)
        ri = lax.broadcasted_iota(jnp.int32, (tb, tb), 0)
        ci = lax.broadcasted_iota(jnp.int32, (tb, tb), 1)
        rc, cc = ri // RET_CHUNK, ci // RET_CHUNK
        dist = jnp.where(rc == cc, jnp.abs(ri - ci), jnp.maximum(ri - ci, 0)).astype(F32)
        for h in range(RET_HEADS):
            d_s[h] = jnp.where(cc > rc, 0.0, jnp.exp(lg_ref[h] * dist) * (RET_QK ** -0.5))
        ang = idx * invf_ref[...]
        cb_s[...] = jnp.cos(ang)
        sb_s[...] = jnp.sin(ang)

    base = (j * tb).astype(F32) * invf_ref[...]
    cos_a, sin_a = jnp.cos(base), jnp.sin(base)
    cos = cos_a * cb_s[...] - sin_a * sb_s[...]
    sin = sin_a * cb_s[...] + cos_a * sb_s[...]
    half = RET_QK // 2

    def rot(t):
        t1, t2 = t[:, :half], t[:, half:]
        return jnp.concatenate([t1 * cos - t2 * sin, t1 * sin + t2 * cos], axis=1)

    for h in range(RET_HEADS):
        lg = lg_ref[h]
        qs = slice(h * RET_QK, (h + 1) * RET_QK)
        vs = slice(h * RET_V, (h + 1) * RET_V)
        q = rot(q_ref[0, :, qs].astype(F32))
        k = rot(k_ref[0, :, qs].astype(F32))
        vb = v_ref[0, :, vs]
        xi = jnp.exp(lg * (idx + 1.0))
        zeta = jnp.exp(lg * (tb - 1.0 - idx)) * (RET_QK ** -0.5)
        gblk = jnp.exp(lg * float(tb))

        r = r_s[h]
        s = _dot_nt(q.astype(BF16), k.astype(BF16)) * d_s[h]
        o = _dot(s.astype(BF16), vb) + _dot((q * xi).astype(BF16), r.astype(BF16))
        r_s[h] = r * gblk + _dot_tn((k * zeta).astype(BF16), vb)

        mu = jnp.mean(o, axis=-1, keepdims=True)
        oc = o - mu
        y = oc * lax.rsqrt(jnp.mean(oc * oc, axis=-1, keepdims=True) + EPS)
        o_ref[0, :, vs] = (y * gn_ref[:, vs] * _silu(g_ref[0, :, vs].astype(F32))).astype(o_ref.dtype)


def _retention(proj, gn_g):
    bsz, l, _ = proj.shape
    tb = min(RET_BLOCK, l)
    hq = RET_HEADS
    dq, dv = hq * RET_QK, hq * RET_V
    log_g = jnp.log(1.0 - 2.0 ** (-5.0 - jnp.arange(hq, dtype=F32))).reshape(hq, 1, 1)
    inv_freq = (ROPE_BASE ** (-jnp.arange(0, RET_QK, 2, dtype=F32) / RET_QK)).reshape(1, RET_QK // 2)
    return pl.pallas_call(
        functools.partial(_ret_kernel, tb=tb),
        grid=(bsz, l // tb),
        in_specs=[pl.BlockSpec((1, tb, dq), lambda b, j: (b, j, 0)),
                  pl.BlockSpec((1, tb, dq), lambda b, j: (b, j, 1)),
                  pl.BlockSpec((1, tb, dv), lambda b, j: (b, j, 1)),
                  pl.BlockSpec((1, tb, dv), lambda b, j: (b, j, 2)),
                  pl.BlockSpec((hq, 1, 1), lambda b, j: (0, 0, 0)),
                  pl.BlockSpec((1, RET_QK // 2), lambda b, j: (0, 0)),
                  pl.BlockSpec((1, dv), lambda b, j: (0, 0))],
        out_specs=pl.BlockSpec((1, tb, dv), lambda b, j: (b, j, 0)),
        out_shape=jax.ShapeDtypeStruct((bsz, l, dv), BF16),
        scratch_shapes=[pltpu.VMEM((hq, RET_QK, RET_V), F32), pltpu.VMEM((hq, tb, tb), F32),
                        pltpu.VMEM((tb, RET_QK // 2), F32), pltpu.VMEM((tb, RET_QK // 2), F32)],
        compiler_params=_cparams(("parallel", "arbitrary")),
        name="retention",
    )(proj, proj, proj, proj, log_g, inv_freq, gn_g.reshape(1, -1))


def _split_mod(m, bsz):
    d = m.shape[-1] // 3
    return tuple(m[:, k * d:(k + 1) * d].reshape(bsz, 1, d) for k in range(3))


def kernel(x, c, norm_mix_g, ada_mix_w, ada_mix_b, norm_ffn_g, ada_ffn_w, ada_ffn_b, ffn_w_in, ffn_w_out, ab_w_in, ssd_conv_w, ssd_conv_b, ssd_dt_bias, ssd_a_log, ssd_d, ssd_norm_g, s5_a_re, s5_a_im, s5_log_dt, s5_b_re, s5_b_im, s5_c_re, s5_c_im, s5_d, s5_glu_w, s5_glu_b, ab_w_out, ret_w_in, ret_gn_g, ret_w_out, final_norm_g):
    bsz = x.shape[0]
    mod_mix = _ada_modulation(c, ada_mix_w, ada_mix_b)
    mod_ffn = _ada_modulation(c, ada_ffn_w, ada_ffn_b)

    shift, scale, gate = _split_mod(mod_mix[0], bsz)
    w0 = ab_w_in.astype(BF16)
    n_main = SSD_INNER + SSD_XBC
    w_dt = jnp.pad(w0[0, :, n_main:n_main + SSD_HEADS], ((0, 0), (0, LANES - SSD_HEADS)))
    w_u = w0[0, :, n_main + SSD_HEADS:]
    proj0, dtp, ut = _in_proj0(x, norm_mix_g[0], scale, shift, w0, 0, n_main, w_dt, w_u)
    y_a = _ssd(proj0, dtp, ssd_conv_w[0], ssd_conv_b[0], ssd_dt_bias[0], ssd_a_log[0], ssd_d[0],
               ssd_norm_g[0])
    yt = _s5(ut, bsz, s5_a_re[0], s5_a_im[0], s5_log_dt[0], s5_b_re[0], s5_b_im[0],
             s5_c_re[0], s5_c_im[0], s5_d[0])
    y_b = _glu(yt, bsz, s5_glu_w[0].astype(BF16), s5_glu_b[0])
    x = _matmul_residual([y_a, y_b], ab_w_out[0].astype(BF16), x, gate, name="out_proj0")

    shift, scale, gate = _split_mod(mod_ffn[0], bsz)
    ffn_w_in_b, ffn_w_out_b = ffn_w_in.astype(BF16), ffn_w_out.astype(BF16)
    x = _ffn(x, norm_ffn_g[0], scale, shift, gate, ffn_w_in_b, ffn_w_out_b, 0, final_norm_g,
             final_norm=False, name="ffn0")

    shift, scale, gate = _split_mod(mod_mix[1], bsz)
    proj1 = _norm_matmul(x, norm_mix_g[1], scale, shift, ret_w_in.astype(BF16), 0, name="in_proj1")
    y_r = _retention(proj1, ret_gn_g[0])
    x = _matmul_residual([y_r], ret_w_out[0].astype(BF16), x, gate, name="out_proj1")

    shift, scale, gate = _split_mod(mod_ffn[1], bsz)
    return _ffn(x, norm_ffn_g[1], scale, shift, gate, ffn_w_in_b, ffn_w_out_b, 1, final_norm_g,
                final_norm=True, name="ffn1")
```

```python
import functools
import math

import numpy as np
import jax
import jax.numpy as jnp
from jax import lax
from jax.experimental import pallas as pl
from jax.experimental.pallas import tpu as pltpu

F32 = jnp.float32
BF16 = jnp.bfloat16
EPS = 1e-6

D_MODEL = 2048
MIX_WIDTH = 2 * D_MODEL
SSD_HEADDIM = 64
SSD_INNER = 3 * MIX_WIDTH // 4
SSD_HEADS = SSD_INNER // SSD_HEADDIM
SSD_GROUPS = 8
SSD_HPG = SSD_HEADS // SSD_GROUPS
SSD_STATE = 128
SSD_CONV = 4
SSD_GW = SSD_HPG * SSD_HEADDIM
SSD_BC = SSD_GROUPS * SSD_STATE
SSD_XBC = SSD_INNER + 2 * SSD_BC
S5_WIDTH = MIX_WIDTH - SSD_INNER
S5_GROUP_CH = 16
S5_GROUPS = S5_WIDTH // S5_GROUP_CH
S5_STATE = 64
S5_CHUNK = 128
RET_HEADS = 8
RET_QK = D_MODEL // RET_HEADS
RET_V = MIX_WIDTH // RET_HEADS
RET_CHUNK = 64
RET_BLOCK = 256
ROPE_BASE = 10000.0
FFN_HIDDEN = ((-(-8 * D_MODEL // 3) + 255) // 256) * 256

LANES = 128
SSD_LC = 128
VMEM_LIMIT = 56 * 1024 * 1024


def _cparams(sem):
    return pltpu.CompilerParams(dimension_semantics=sem, vmem_limit_bytes=VMEM_LIMIT)


def _dot(a, b):
    return jnp.dot(a, b, preferred_element_type=F32)


def _dot_nt(a, b):
    return lax.dot_general(a, b, (((1,), (1,)), ((), ())), preferred_element_type=F32)


def _dot_tn(a, b):
    return lax.dot_general(a, b, (((0,), (0,)), ((), ())), preferred_element_type=F32)


def _split3(v):
    hi = v.astype(BF16)
    r = v - hi.astype(F32)
    mid = r.astype(BF16)
    lo = (r - mid.astype(F32)).astype(BF16)
    return hi, mid, lo


def _sel_right(v, e):
    hi, mid, lo = _split3(v)
    return (_dot(lo, e) + _dot(mid, e)) + _dot(hi, e)


def _sel_left(e, v):
    hi, mid, lo = _split3(v)
    return (_dot(e, lo) + _dot(e, mid)) + _dot(e, hi)


def _silu(v):
    h = 0.5 * v
    return h + h * jnp.tanh(h)


NORM_ROWS = 32


def _store_modulated_norm(x_ref, g_ref, sc_ref, sh_ref, hn_ref):
    gm = g_ref[...] * (1.0 + sc_ref[0])
    sh = sh_ref[0]

    def body(i, carry):
        r0 = pl.multiple_of(i * NORM_ROWS, NORM_ROWS)
        x = x_ref[0, pl.ds(r0, NORM_ROWS), :]
        ms = jnp.mean(x * x, axis=-1, keepdims=True)
        hn_ref[pl.ds(r0, NORM_ROWS), :] = ((x * lax.rsqrt(ms + EPS)) * gm + sh).astype(BF16)
        return carry

    lax.fori_loop(0, hn_ref.shape[0] // NORM_ROWS, body, 0, unroll=4)


def _ada_kernel(c_ref, w_ref, b_ref, o_ref):
    sc = _silu(c_ref[...]).astype(BF16)
    o_ref[0] = _dot(sc, w_ref[0].astype(BF16)) + b_ref[0]


def _ada_modulation(c, w, b):
    depth, d, n = w.shape
    bsz = c.shape[0]
    tn = 1024
    return pl.pallas_call(
        _ada_kernel,
        grid=(depth, n // tn),
        in_specs=[pl.BlockSpec((bsz, d), lambda i, j: (0, 0)),
                  pl.BlockSpec((1, d, tn), lambda i, j: (i, 0, j)),
                  pl.BlockSpec((1, 1, tn), lambda i, j: (i, 0, j))],
        out_specs=pl.BlockSpec((1, bsz, tn), lambda i, j: (i, 0, j)),
        out_shape=jax.ShapeDtypeStruct((depth, bsz, n), F32),
        compiler_params=_cparams(("parallel", "parallel")),
        name="ada_modulation",
    )(c, w, b.reshape(depth, 1, n))


def _norm_mm_kernel(x_ref, g_ref, sc_ref, sh_ref, w_ref, o_ref, hn_ref):
    @pl.when(pl.program_id(2) == 0)
    def _():
        _store_modulated_norm(x_ref, g_ref, sc_ref, sh_ref, hn_ref)

    o_ref[0] = _dot(hn_ref[...], w_ref[...]).astype(o_ref.dtype)


def _in_proj0_kernel(x_ref, g_ref, sc_ref, sh_ref, w_ref, wdt_ref, wu_ref, o_ref, dt_ref, ut_ref, hn_ref,
                     *, n_ut):
    n = pl.program_id(2)

    @pl.when(n == 0)
    def _():
        _store_modulated_norm(x_ref, g_ref, sc_ref, sh_ref, hn_ref)
        dt_ref[0] = _dot(hn_ref[...], wdt_ref[...])

    @pl.when(n < n_ut)
    def _():
        ut_ref[...] = _dot(hn_ref[...], wu_ref[...]).T

    @pl.when(n >= n_ut)
    def _():
        o_ref[0] = _dot(hn_ref[...], w_ref[...]).astype(o_ref.dtype)


def _norm_matmul(x, g, scale, shift, w, layer, *, tm=1024, tn=1024, name):
    bsz, l, d = x.shape
    n = w.shape[2]
    tm = min(tm, l)
    nt, nj = l // tm, n // tn

    def col(b, i, j):
        return _snake(b * nt + i, j, nj)

    return pl.pallas_call(
        _norm_mm_kernel,
        grid=(bsz, nt, nj),
        in_specs=[pl.BlockSpec((1, tm, d), lambda b, i, j: (b, i, 0)),
                  pl.BlockSpec((1, d), lambda b, i, j: (0, 0)),
                  pl.BlockSpec((1, 1, d), lambda b, i, j: (b, 0, 0)),
                  pl.BlockSpec((1, 1, d), lambda b, i, j: (b, 0, 0)),
                  pl.BlockSpec((None, d, tn), lambda b, i, j: (layer, 0, col(b, i, j)))],
        out_specs=pl.BlockSpec((1, tm, tn), lambda b, i, j: (b, i, col(b, i, j))),
        out_shape=jax.ShapeDtypeStruct((bsz, l, n), BF16),
        scratch_shapes=[pltpu.VMEM((tm, d), BF16)],
        compiler_params=_cparams(("parallel", "parallel", "arbitrary")),
        name=name,
    )(x, g.reshape(1, d), scale, shift, w)


def _in_proj0(x, g, scale, shift, w, layer, n, w_dt, w_u, *, tm=1024, tn=1024, tc=512):
    bsz, l, d = x.shape
    ndt, nut = w_dt.shape[1], w_u.shape[1]
    tm = min(tm, l)
    n_main, n_ut, nt = n // tn, nut // tc, l // tm

    def main_tile(b, i, j):
        return _snake(b * nt + i, jnp.clip(j - n_ut, 0, n_main - 1), n_main)

    return pl.pallas_call(
        functools.partial(_in_proj0_kernel, n_ut=n_ut),
        grid=(bsz, nt, n_ut + n_main),
        in_specs=[pl.BlockSpec((1, tm, d), lambda b, i, j: (b, i, 0)),
                  pl.BlockSpec((1, d), lambda b, i, j: (0, 0)),
                  pl.BlockSpec((1, 1, d), lambda b, i, j: (b, 0, 0)),
                  pl.BlockSpec((1, 1, d), lambda b, i, j: (b, 0, 0)),
                  pl.BlockSpec((None, d, tn), lambda b, i, j: (layer, 0, main_tile(b, i, j))),
                  pl.BlockSpec((d, ndt), lambda b, i, j: (0, 0)),
                  pl.BlockSpec((d, tc), lambda b, i, j: (0, jnp.minimum(j, n_ut - 1)))],
        out_specs=[pl.BlockSpec((1, tm, tn), lambda b, i, j: (b, i, main_tile(b, i, j))),
                   pl.BlockSpec((1, tm, ndt), lambda b, i, j: (b, i, 0)),
                   pl.BlockSpec((tc, tm), lambda b, i, j: (jnp.minimum(j, n_ut - 1), b * nt + i))],
        out_shape=[jax.ShapeDtypeStruct((bsz, l, n), BF16),
                   jax.ShapeDtypeStruct((bsz, l, ndt), F32),
                   jax.ShapeDtypeStruct((nut, bsz * l), F32)],
        scratch_shapes=[pltpu.VMEM((tm, d), BF16)],
        compiler_params=_cparams(("parallel", "parallel", "arbitrary")),
        name="in_proj0",
    )(x, g.reshape(1, d), scale, shift, w, w_dt, w_u)


def _snake(tile, j, n):
    return jnp.where(tile % 2 == 0, j, n - 1 - j)


def _mm_res_kernel(*refs, bounds, nt):
    na = len(bounds) - 1
    a_refs = refs[:na]
    w_ref, x_ref, gate_ref, o_ref = refs[na:]
    step = pl.program_id(2)
    k = _snake(pl.program_id(0) * nt + pl.program_id(1), step, bounds[-1])

    for i, a_ref in enumerate(a_refs):
        mine = (k >= bounds[i]) & (k < bounds[i + 1])

        @pl.when(mine & (step == 0))
        def _(a_ref=a_ref):
            o_ref[0] = _dot(a_ref[0], w_ref[...])

        @pl.when(mine & (step > 0))
        def _(a_ref=a_ref):
            o_ref[0] += _dot(a_ref[0], w_ref[...])

    @pl.when(step == bounds[-1] - 1)
    def _():
        o_ref[0] = x_ref[0] + gate_ref[0] * o_ref[0]


def _matmul_residual(a_list, w, x, gate, *, tm=1024, tk=1024, name):
    bsz, l, d = x.shape
    kk, n = w.shape
    tm = min(tm, l)
    bounds = [0]
    for a in a_list:
        bounds.append(bounds[-1] + a.shape[2] // tk)
    assert bounds[-1] * tk == kk

    nt, nk = l // tm, bounds[-1]

    def ktile(b, i, k):
        return _snake(b * nt + i, k, nk)

    def a_spec(lo, hi):
        return pl.BlockSpec((1, tm, tk), lambda b, i, k: (b, i, jnp.clip(ktile(b, i, k) - lo, 0, hi - lo - 1)))

    return pl.pallas_call(
        functools.partial(_mm_res_kernel, bounds=tuple(bounds), nt=nt),
        grid=(bsz, nt, nk),
        in_specs=[a_spec(bounds[i], bounds[i + 1]) for i in range(len(a_list))] + [
            pl.BlockSpec((tk, n), lambda b, i, k: (ktile(b, i, k), 0)),
            pl.BlockSpec((1, tm, d), lambda b, i, k: (b, i, 0)),
            pl.BlockSpec((1, 1, d), lambda b, i, k: (b, 0, 0))],
        out_specs=pl.BlockSpec((1, tm, n), lambda b, i, k: (b, i, 0)),
        out_shape=jax.ShapeDtypeStruct((bsz, l, n), F32),
        compiler_params=_cparams(("parallel", "parallel", "arbitrary")),
        name=name,
    )(*a_list, w, x, gate)


def _ffn_kernel(x_ref, g_ref, sc_ref, sh_ref, gate_ref, wg_ref, wu_ref, wo_ref, fg_ref,
                o_ref, hn_ref, *, nh, final_norm):
    h = pl.program_id(2)

    @pl.when(h == 0)
    def _():
        _store_modulated_norm(x_ref, g_ref, sc_ref, sh_ref, hn_ref)

    hn = hn_ref[...]
    gt = _dot(hn, wg_ref[...])
    up = _dot(hn, wu_ref[...])
    act = (_silu(gt) * up).astype(BF16)

    @pl.when(h == 0)
    def _():
        o_ref[0] = _dot(act, wo_ref[...])

    @pl.when(h > 0)
    def _():
        o_ref[0] += _dot(act, wo_ref[...])

    @pl.when(h == nh - 1)
    def _():
        gate = gate_ref[0]

        def body(i, carry):
            rows = pl.ds(pl.multiple_of(i * NORM_ROWS, NORM_ROWS), NORM_ROWS)
            y = x_ref[0, rows, :] + gate * o_ref[0, rows, :]
            if final_norm:
                ms = jnp.mean(y * y, axis=-1, keepdims=True)
                y = y * lax.rsqrt(ms + EPS) * fg_ref[...]
            o_ref[0, rows, :] = y
            return carry

        lax.fori_loop(0, hn_ref.shape[0] // NORM_ROWS, body, 0, unroll=4)


def _ffn(x, g, scale, shift, gate, w_in, w_out, layer, final_g, *, final_norm, tm=1024, th=512, name):
    bsz, l, d = x.shape
    hid = w_out.shape[1]
    tm = min(tm, l)
    nh = hid // th
    nt = l // tm
    vec = pl.BlockSpec((1, 1, d), lambda b, i, h: (b, 0, 0))

    def hid_tile(b, i, h):
        return _snake(b * nt + i, h, nh)

    return pl.pallas_call(
        functools.partial(_ffn_kernel, nh=nh, final_norm=final_norm),
        grid=(bsz, l // tm, nh),
        in_specs=[pl.BlockSpec((1, tm, d), lambda b, i, h: (b, i, 0)),
                  pl.BlockSpec((1, d), lambda b, i, h: (0, 0)),
                  vec, vec, vec,
                  pl.BlockSpec((None, d, th), lambda b, i, h: (layer, 0, hid_tile(b, i, h))),
                  pl.BlockSpec((None, d, th), lambda b, i, h: (layer, 0, hid_tile(b, i, h) + nh)),
                  pl.BlockSpec((None, th, d), lambda b, i, h: (layer, hid_tile(b, i, h), 0)),
                  pl.BlockSpec((1, d), lambda b, i, h: (0, 0))],
        out_specs=pl.BlockSpec((1, tm, d), lambda b, i, h: (b, i, 0)),
        out_shape=jax.ShapeDtypeStruct((bsz, l, d), F32),
        scratch_shapes=[pltpu.VMEM((tm, d), BF16)],
        compiler_params=_cparams(("parallel", "parallel", "arbitrary")),
        name=name,
    )(x, g.reshape(1, d), scale, shift, gate, w_in, w_in, w_out, final_g.reshape(1, d))


def _conv_silu(raw_ref, pad_ref, w_ref, b_ref, out_ref, first, lc, width, cb=256):
    @pl.when(first)
    def _():
        pad_ref[0:8, :] = jnp.zeros((8, width), F32)

    pad_ref[8:8 + lc, :] = raw_ref[0].astype(F32)
    ntap = SSD_CONV
    for c0 in range(0, width, cb):
        cs = slice(c0, c0 + cb)
        blk = pad_ref[:, cs]
        acc = b_ref[:, cs] + w_ref[ntap - 1:ntap, cs] * blk[8:]
        for j in range(1, ntap):
            acc = acc + w_ref[ntap - 1 - j:ntap - j, cs] * pltpu.roll(blk, j, 0)[8:]
        out_ref[:, cs] = _silu(acc)
    pad_ref[0:8, :] = pad_ref[lc:lc + 8, :]


def _ssd_kernel(z_ref, xs_ref, bm_ref, cm_ref, dt_ref,
                cwx_ref, cwb_ref, cwc_ref, cbx_ref, cbb_ref, cbc_ref,
                dtb_ref, alog_ref, dsk_ref, ng_ref, e_ref, tri_ref,
                o_ref,
                padx_s, padb_s, padc_s, xs_s, bm_s, cm_s, rhs_s, st_s, *, lc):
    first = pl.program_id(1) == 0

    @pl.when(first)
    def _():
        st_s[...] = jnp.zeros_like(st_s)

    _conv_silu(xs_ref, padx_s, cwx_ref, cbx_ref, xs_s, first, lc, SSD_INNER)
    _conv_silu(bm_ref, padb_s, cwb_ref, cbb_ref, bm_s, first, lc, SSD_BC)
    _conv_silu(cm_ref, padc_s, cwc_ref, cbc_ref, cm_s, first, lc, SSD_BC)

    rhs_s[0:lc, :] = xs_s[...].astype(BF16)
    rhs_s[lc:lc + SSD_STATE, :] = st_s[...].astype(BF16)

    v = dt_ref[0] + dtb_ref[...]
    dt = jnp.maximum(v, 0.0) + jnp.log(1.0 + jnp.exp(-jnp.abs(v)))
    a = -jnp.exp(alog_ref[...])
    acum = _sel_left(tri_ref[...], dt * a)
    acum_last = acum[lc - 1:lc, :]
    wdec = jnp.exp(acum_last - acum) * dt
    eacum = jnp.exp(acum)
    acum_t = acum.T
    dt_t = dt.T
    wexp = _sel_right(wdec, e_ref[...])
    eal = _sel_right(jnp.broadcast_to(jnp.exp(acum_last), (8, LANES)), e_ref[...])[0:1, :]

    row = lax.broadcasted_iota(jnp.int32, (lc, lc), 0)
    col = lax.broadcasted_iota(jnp.int32, (lc, lc), 1)
    causal = row >= col
    lane = lax.broadcasted_iota(jnp.int32, (lc, LANES), 1)
    low_half = lane < SSD_HEADDIM

    for g in range(SSD_GROUPS):
        gs = slice(g * SSD_STATE, (g + 1) * SSD_STATE)
        bm_g = bm_s[:, gs]
        cm_g = cm_s[:, gs]
        bm_gb = bm_g.astype(BF16)
        scores = _dot_nt(cm_g.astype(BF16), bm_gb)
        pieces = []
        for j in range(SSD_HPG // 2):
            ps = slice(g * SSD_GW + j * LANES, g * SSD_GW + (j + 1) * LANES)
            rhs = rhs_s[:, ps]
            ys = []
            for hh in range(2):
                h = g * SSD_HPG + 2 * j + hh
                seg = acum[:, h:h + 1] - acum_t[h:h + 1, :]
                lmat = jnp.exp(jnp.where(causal, seg, -1e30))
                amat = scores * lmat * dt_t[h:h + 1, :]
                cs = cm_g * eacum[:, h:h + 1]
                lhs = jnp.concatenate([amat, cs], axis=1).astype(BF16)
                ys.append(_dot(lhs, rhs))
            pieces.append(jnp.where(low_half, ys[0], ys[1]))
        y = jnp.concatenate(pieces, axis=1)

        ws = slice(g * SSD_GW, (g + 1) * SSD_GW)
        xs_g = xs_s[:, ws]
        xw = (xs_g * wexp[:, ws]).astype(BF16)
        st_s[:, ws] = st_s[:, ws] * eal[:, ws] + _dot_tn(bm_gb, xw)

        y = y + dsk_ref[:, ws] * xs_g
        y = y * _silu(z_ref[0, :, ws].astype(F32))
        ms = jnp.mean(y * y, axis=-1, keepdims=True)
        o_ref[0, :, ws] = (y * lax.rsqrt(ms + EPS) * ng_ref[:, ws]).astype(o_ref.dtype)


def _ssd(proj, dtp, conv_w, conv_b, dt_bias, a_log, d_skip, norm_g):
    bsz, l, _ = proj.shape
    lc = min(SSD_LC, l)
    i0 = SSD_INNER
    cwx, cwb, cwc = conv_w[:, :i0], conv_w[:, i0:i0 + SSD_BC], conv_w[:, i0 + SSD_BC:]
    cb = conv_b.reshape(1, -1)
    cbx, cbb, cbc = cb[:, :i0], cb[:, i0:i0 + SSD_BC], cb[:, i0 + SSD_BC:]
    pad = LANES - SSD_HEADS
    dtb = jnp.pad(dt_bias, (0, pad)).reshape(1, LANES)
    alog = jnp.pad(a_log, (0, pad)).reshape(1, LANES)
    dsk = jnp.repeat(d_skip, SSD_HEADDIM).reshape(1, i0)
    head_of_lane = np.arange(i0) // SSD_HEADDIM
    expand = jnp.asarray(np.arange(LANES)[:, None] == head_of_lane[None, :], BF16)
    tri = jnp.asarray(np.tril(np.ones((lc, lc))), BF16)

    def full(shape):
        return pl.BlockSpec(shape, lambda b, k: (0,) * len(shape))

    nb3 = i0 // SSD_BC
    return pl.pallas_call(
        functools.partial(_ssd_kernel, lc=lc),
        grid=(bsz, l // lc),
        in_specs=[pl.BlockSpec((1, lc, i0), lambda b, k: (b, k, 0)),
                  pl.BlockSpec((1, lc, i0), lambda b, k: (b, k, 1)),
                  pl.BlockSpec((1, lc, SSD_BC), lambda b, k: (b, k, 2 * nb3)),
                  pl.BlockSpec((1, lc, SSD_BC), lambda b, k: (b, k, 2 * nb3 + 1)),
                  pl.BlockSpec((1, lc, LANES), lambda b, k: (b, k, 0)),
                  full((SSD_CONV, i0)), full((SSD_CONV, SSD_BC)), full((SSD_CONV, SSD_BC)),
                  full((1, i0)), full((1, SSD_BC)), full((1, SSD_BC)),
                  full((1, LANES)), full((1, LANES)), full((1, i0)), full((1, i0)),
                  full((LANES, i0)), full((lc, lc))],
        out_specs=pl.BlockSpec((1, lc, i0), lambda b, k: (b, k, 0)),
        out_shape=jax.ShapeDtypeStruct((bsz, l, i0), BF16),
        scratch_shapes=[pltpu.VMEM((lc + 8, i0), F32), pltpu.VMEM((lc + 8, SSD_BC), F32),
                        pltpu.VMEM((lc + 8, SSD_BC), F32),
                        pltpu.VMEM((lc, i0), F32), pltpu.VMEM((lc, SSD_BC), F32),
                        pltpu.VMEM((lc, SSD_BC), F32),
                        pltpu.VMEM((lc + SSD_STATE, i0), BF16), pltpu.VMEM((SSD_STATE, i0), F32)],
        compiler_params=_cparams(("parallel", "arbitrary")),
        name="ssd_scan",
    )(proj, proj, proj, proj, dtp, cwx, cwb, cwc, cbx, cbb, cbc, dtb, alog, dsk,
      norm_g.reshape(1, i0), expand, tri)


def _cplx_mul(ar, ai, br, bi):
    return ar * br - ai * bi, ar * bi + ai * br


def _zoh_coef(are, aim, delta):
    er = jnp.exp(are * delta)
    br = er * jnp.cos(aim * delta) - 1.0
    bi = er * jnp.sin(aim * delta)
    den = are * are + aim * aim
    return (br * are + bi * aim) / den, (bi * are - br * aim) / den


def _lam_pow(are, aim, delta, n):
    mag = jnp.exp((are * delta) * n)
    ang = (aim * delta) * n
    return mag * jnp.cos(ang), mag * jnp.sin(ang)


def _s5_kernel(u_ref, arc_ref, aic_ref, arr_ref, air_ref, ldt_ref,
               bT_re_ref, bT_im_ref, b_re_ref, b_im_ref, ct_re_ref, ct_im_ref, d_ref,
               o_ref,
               ma_s, mb_s, yacc_s, r0_s, s_re_s, s_im_s, xp_re_s, xp_im_s, u_s, y_s, *, nb, nchunks):
    nch, lc = S5_GROUP_CH, S5_CHUNK
    nrows = nb * nchunks
    pitch = u_s.shape[0] // nch

    def chunk_row(r):
        b, k = divmod(r, nchunks)
        return k * nb + b

    for r in range(nrows):
        u_s[pl.ds(chunk_row(r), nch, stride=pitch), :] = u_ref[:, r * lc:(r + 1) * lc]

    def u_tile(c):
        return u_s[c * pitch:c * pitch + nrows, :]
    delta = jnp.exp(ldt_ref[0])
    arc, aic = arc_ref[0], aic_ref[0]
    arr, air = arr_ref[0], air_ref[0]
    tau = lax.broadcasted_iota(jnp.int32, (1, lc), 1).astype(F32)
    p_re, p_im = _lam_pow(arc, aic, delta, tau)
    q_re, q_im = _lam_pow(arc, aic, delta, (lc - 1.0) - tau)
    cf_re_c, cf_im_c = _zoh_coef(arc, aic, delta)
    cf_re_r, cf_im_r = _zoh_coef(arr, air, delta)
    ct_re, ct_im = ct_re_ref[0], ct_im_ref[0]

    es = [_cplx_mul(ct_re[:, c:c + 1], ct_im[:, c:c + 1], p_re, p_im) for c in range(nch)]
    e_re = jnp.concatenate([e[0] for e in es], axis=1)
    e_im = jnp.concatenate([e[1] for e in es], axis=1)
    bbT_re, bbT_im = _cplx_mul(cf_re_r, cf_im_r, bT_re_ref[0], bT_im_ref[0])
    r0_s[...] = _dot(bbT_re.astype(BF16), e_re.astype(BF16)) - _dot(bbT_im.astype(BF16), e_im.astype(BF16))

    srow = lax.broadcasted_iota(jnp.int32, (lc, lc), 0)
    tcol = lax.broadcasted_iota(jnp.int32, (lc, lc), 1)
    causal = tcol >= srow

    def fill_rows(q, dst):
        for half in range(2):
            row = r0_s[pl.ds(2 * q + half, 1), :].astype(BF16).astype(F32)
            bits = lax.bitcast_convert_type(row, jnp.int32)
            rows = slice(half * lc, (half + 1) * lc)
            for c in range(0, nch, 2):
                word = (bits[:, c * lc:(c + 1) * lc]
                        | lax.shift_right_logical(bits[:, (c + 1) * lc:(c + 2) * lc], 16))
                blk = pltpu.roll(jnp.broadcast_to(word, (lc, lc)), 0, 1, stride=1, stride_axis=0)
                first = lax.bitcast_convert_type(blk & jnp.int32(-65536), F32)
                second = lax.bitcast_convert_type(lax.shift_left(blk, 16), F32)
                dst[rows, c * lc:(c + 1) * lc] = jnp.where(causal, first, 0.0).astype(BF16)
                dst[rows, (c + 1) * lc:(c + 2) * lc] = jnp.where(causal, second, 0.0).astype(BF16)

    def add_product(q, src):
        r_even = pl.multiple_of(2 * q * pitch, 8)
        r_odd = pl.multiple_of((2 * q + 1) * pitch, 8)
        lhs = jnp.concatenate([u_s[pl.ds(r_even, nrows), :], u_s[pl.ds(r_odd, nrows), :]], axis=1)
        yacc_s[...] += _dot(lhs.astype(BF16), src[...])

    nq = nch // 2
    yacc_s[...] = jnp.zeros_like(yacc_s)
    fill_rows(0, ma_s)

    def step(i, carry):
        q = 2 * i
        fill_rows(q + 1, mb_s)
        add_product(q, ma_s)
        fill_rows(jnp.minimum(q + 2, nq - 1), ma_s)
        add_product(q + 1, mb_s)
        return carry

    lax.fori_loop(0, nq // 2, step, 0)

    bb_re, bb_im = _cplx_mul(cf_re_c, cf_im_c, b_re_ref[0], b_im_ref[0])
    ws = [_cplx_mul(q_re, q_im, bb_re[:, c:c + 1], bb_im[:, c:c + 1]) for c in range(nch)]
    wt_re = jnp.concatenate([w[0] for w in ws], axis=1).astype(BF16)
    wt_im = jnp.concatenate([w[1] for w in ws], axis=1).astype(BF16)
    ub = jnp.concatenate([u_tile(c) for c in range(nch)], axis=1).astype(BF16)
    s_re_s[...] = _dot_nt(ub, wt_re)
    s_im_s[...] = _dot_nt(ub, wt_im)

    c_re, c_im = _lam_pow(arr, air, delta, float(lc))
    l1_re, l1_im = _lam_pow(arr, air, delta, 1.0)
    xr = jnp.zeros((nb, S5_STATE), F32)
    xi = jnp.zeros((nb, S5_STATE), F32)
    for k in range(nchunks):
        rows = slice(k * nb, (k + 1) * nb)
        xp_re_s[rows, :] = xr * l1_re - xi * l1_im
        xp_im_s[rows, :] = xr * l1_im + xi * l1_re
        xr, xi = (xr * c_re - xi * c_im + s_re_s[rows, :],
                  xr * c_im + xi * c_re + s_im_s[rows, :])

    y = yacc_s[...] + _dot(xp_re_s[...].astype(BF16), e_re.astype(BF16))
    y = y - _dot(xp_im_s[...].astype(BF16), e_im.astype(BF16))
    for c in range(nch):
        yc = y[:, c * lc:(c + 1) * lc] + d_ref[0, c:c + 1, :] * u_tile(c)
        y_s[c * pitch:c * pitch + nrows, :] = (
            0.5 * yc * (1.0 + jnp.tanh(math.sqrt(2.0 / math.pi) * (yc + 0.044715 * (yc * yc * yc)))))
    for r in range(nrows):
        o_ref[:, r * lc:(r + 1) * lc] = y_s[pl.ds(chunk_row(r), nch, stride=pitch), :]


def _s5(ut, bsz, a_re, a_im, log_dt, b_re, b_im, c_re, c_im, d):
    width, t = ut.shape
    l = t // bsz
    g, p, ch, lc = S5_GROUPS, S5_STATE, S5_GROUP_CH, S5_CHUNK
    nchunks = l // lc
    nrows = nchunks * bsz
    pitch = nrows + 8

    def per_group(shape):
        return pl.BlockSpec((1,) + shape, lambda i: (i, 0, 0))

    return pl.pallas_call(
        functools.partial(_s5_kernel, nb=bsz, nchunks=nchunks),
        grid=(g,),
        in_specs=[pl.BlockSpec((ch, t), lambda i: (i, 0)),
                  per_group((p, 1)), per_group((p, 1)), per_group((1, p)), per_group((1, p)),
                  per_group((1, 1)),
                  per_group((ch, p)), per_group((ch, p)),
                  per_group((p, ch)), per_group((p, ch)),
                  per_group((p, ch)), per_group((p, ch)),
                  per_group((ch, 1))],
        out_specs=pl.BlockSpec((ch, t), lambda i: (i, 0)),
        out_shape=jax.ShapeDtypeStruct((width, t), F32),
        scratch_shapes=[pltpu.VMEM((2 * lc, ch * lc), BF16), pltpu.VMEM((2 * lc, ch * lc), BF16),
                        pltpu.VMEM((nrows, ch * lc), F32), pltpu.VMEM((ch, ch * lc), F32),
                        pltpu.VMEM((nrows, p), F32), pltpu.VMEM((nrows, p), F32),
                        pltpu.VMEM((nrows, p), F32), pltpu.VMEM((nrows, p), F32),
                        pltpu.VMEM((ch * pitch, lc), F32), pltpu.VMEM((ch * pitch, lc), F32)],
        compiler_params=_cparams(("parallel",)),
        name="s5_ssm",
    )(ut,
      a_re.reshape(g, p, 1), a_im.reshape(g, p, 1), a_re.reshape(g, 1, p), a_im.reshape(g, 1, p),
      log_dt.reshape(g, 1, 1),
      b_re.transpose(0, 2, 1), b_im.transpose(0, 2, 1), b_re, b_im,
      c_re.transpose(0, 2, 1), c_im.transpose(0, 2, 1),
      d.reshape(g, ch, 1))


def _glu_kernel(yt_ref, w_ref, b_ref, o_ref):
    y = yt_ref[...].T
    gate = jax.nn.sigmoid(_dot(y.astype(BF16), w_ref[...]) + b_ref[...])
    o_ref[0] = (y * gate).astype(o_ref.dtype)


def _glu(yt, bsz, w, b, *, tm=512):
    n, t = yt.shape
    l = t // bsz
    tm = min(tm, l)
    nt = l // tm
    return pl.pallas_call(
        _glu_kernel,
        grid=(bsz, nt),
        in_specs=[pl.BlockSpec((n, tm), lambda b_, i: (0, b_ * nt + i)),
                  pl.BlockSpec((n, n), lambda b_, i: (0, 0)),
                  pl.BlockSpec((1, n), lambda b_, i: (0, 0))],
        out_specs=pl.BlockSpec((1, tm, n), lambda b_, i: (b_, i, 0)),
        out_shape=jax.ShapeDtypeStruct((bsz, l, n), BF16),
        compiler_params=_cparams(("parallel", "parallel")),
        name="s5_glu",
    )(yt, w, b.reshape(1, n))


def _ret_kernel(q_ref, k_ref, v_ref, g_ref, lg_ref, invf_ref, gn_ref, o_ref,
                r_s, d_s, cb_s, sb_s, *, tb):
    j = pl.program_id(1)
    idx = lax.broadcasted_iota(jnp.int32, (tb, 1), 0).astype(F32)

    @pl.when(j == 0)
    def _():
        r_s[...] = jnp.zeros_like(r_s)
        ri = lax.broadcasted_iota(jnp.int32, (tb, tb), 0)
        ci = lax.broadcasted_iota(jnp.int32, (tb, tb), 1)
        rc, cc = ri // RET_CHUNK, ci // RET_CHUNK
        dist = jnp.where(rc == cc, jnp.abs(ri - ci), jnp.maximum(ri - ci, 0)).astype(F32)
        for h in range(RET_HEADS):
            d_s[h] = jnp.where(cc > rc, 0.0, jnp.exp(lg_ref[h] * dist) * (RET_QK ** -0.5))
        ang = idx * invf_ref[...]
        cb_s[...] = jnp.cos(ang)
        sb_s[...] = jnp.sin(ang)

    base = (j * tb).astype(F32) * invf_ref[...]
    cos_a, sin_a = jnp.cos(base), jnp.sin(base)
    cos = cos_a * cb_s[...] - sin_a * sb_s[...]
    sin = sin_a * cb_s[...] + cos_a * sb_s[...]
    half = RET_QK // 2

    def rot(t):
        t1, t2 = t[:, :half], t[:, half:]
        return jnp.concatenate([t1 * cos - t2 * sin, t1 * sin + t2 * cos], axis=1)

    for h in range(RET_HEADS):
        lg = lg_ref[h]
        qs = slice(h * RET_QK, (h + 1) * RET_QK)
        vs = slice(h * RET_V, (h + 1) * RET_V)
        q = rot(q_ref[0, :, qs].astype(F32))
        k = rot(k_ref[0, :, qs].astype(F32))
        vb = v_ref[0, :, vs]
        xi = jnp.exp(lg * (idx + 1.0))
        zeta = jnp.exp(lg * (tb - 1.0 - idx)) * (RET_QK ** -0.5)
        gblk = jnp.exp(lg * float(tb))

        r = r_s[h]
        s = _dot_nt(q.astype(BF16), k.astype(BF16)) * d_s[h]
        o = _dot(s.astype(BF16), vb) + _dot((q * xi).astype(BF16), r.astype(BF16))
        r_s[h] = r * gblk + _dot_tn((k * zeta).astype(BF16), vb)

        mu = jnp.mean(o, axis=-1, keepdims=True)
        oc = o - mu
        y = oc * lax.rsqrt(jnp.mean(oc * oc, axis=-1, keepdims=True) + EPS)
        o_ref[0, :, vs] = (y * gn_ref[:, vs] * _silu(g_ref[0, :, vs].astype(F32))).astype(o_ref.dtype)


def _retention(proj, gn_g):
    bsz, l, _ = proj.shape
    tb = min(RET_BLOCK, l)
    hq = RET_HEADS
    dq, dv = hq * RET_QK, hq * RET_V
    log_g = jnp.log(1.0 - 2.0 ** (-5.0 - jnp.arange(hq, dtype=F32))).reshape(hq, 1, 1)
    inv_freq = (ROPE_BASE ** (-jnp.arange(0, RET_QK, 2, dtype=F32) / RET_QK)).reshape(1, RET_QK // 2)
    return pl.pallas_call(
        functools.partial(_ret_kernel, tb=tb),
        grid=(bsz, l // tb),
        in_specs=[pl.BlockSpec((1, tb, dq), lambda b, j: (b, j, 0)),
                  pl.BlockSpec((1, tb, dq), lambda b, j: (b, j, 1)),
                  pl.BlockSpec((1, tb, dv), lambda b, j: (b, j, 1)),
                  pl.BlockSpec((1, tb, dv), lambda b, j: (b, j, 2)),
                  pl.BlockSpec((hq, 1, 1), lambda b, j: (0, 0, 0)),
                  pl.BlockSpec((1, RET_QK // 2), lambda b, j: (0, 0)),
                  pl.BlockSpec((1, dv), lambda b, j: (0, 0))],
        out_specs=pl.BlockSpec((1, tb, dv), lambda b, j: (b, j, 0)),
        out_shape=jax.ShapeDtypeStruct((bsz, l, dv), BF16),
        scratch_shapes=[pltpu.VMEM((hq, RET_QK, RET_V), F32), pltpu.VMEM((hq, tb, tb), F32),
                        pltpu.VMEM((tb, RET_QK // 2), F32), pltpu.VMEM((tb, RET_QK // 2), F32)],
        compiler_params=_cparams(("parallel", "arbitrary")),
        name="retention",
    )(proj, proj, proj, proj, log_g, inv_freq, gn_g.reshape(1, -1))


def _split_mod(m, bsz):
    d = m.shape[-1] // 3
    return tuple(m[:, k * d:(k + 1) * d].reshape(bsz, 1, d) for k in range(3))


def kernel(x, c, norm_mix_g, ada_mix_w, ada_mix_b, norm_ffn_g, ada_ffn_w, ada_ffn_b, ffn_w_in, ffn_w_out, ab_w_in, ssd_conv_w, ssd_conv_b, ssd_dt_bias, ssd_a_log, ssd_d, ssd_norm_g, s5_a_re, s5_a_im, s5_log_dt, s5_b_re, s5_b_im, s5_c_re, s5_c_im, s5_d, s5_glu_w, s5_glu_b, ab_w_out, ret_w_in, ret_gn_g, ret_w_out, final_norm_g):
    bsz = x.shape[0]
    mod_mix = _ada_modulation(c, ada_mix_w, ada_mix_b)
    mod_ffn = _ada_modulation(c, ada_ffn_w, ada_ffn_b)

    shift, scale, gate = _split_mod(mod_mix[0], bsz)
    w0 = ab_w_in.astype(BF16)
    n_main = SSD_INNER + SSD_XBC
    w_dt = jnp.pad(w0[0, :, n_main:n_main + SSD_HEADS], ((0, 0), (0, LANES - SSD_HEADS)))
    w_u = w0[0, :, n_main + SSD_HEADS:]
    proj0, dtp, ut = _in_proj0(x, norm_mix_g[0], scale, shift, w0, 0, n_main, w_dt, w_u)
    y_a = _ssd(proj0, dtp, ssd_conv_w[0], ssd_conv_b[0], ssd_dt_bias[0], ssd_a_log[0], ssd_d[0],
               ssd_norm_g[0])
    yt = _s5(ut, bsz, s5_a_re[0], s5_a_im[0], s5_log_dt[0], s5_b_re[0], s5_b_im[0],
             s5_c_re[0], s5_c_im[0], s5_d[0])
    y_b = _glu(yt, bsz, s5_glu_w[0].astype(BF16), s5_glu_b[0])
    x = _matmul_residual([y_a, y_b], ab_w_out[0].astype(BF16), x, gate, name="out_proj0")

    shift, scale, gate = _split_mod(mod_ffn[0], bsz)
    ffn_w_in_b, ffn_w_out_b = ffn_w_in.astype(BF16), ffn_w_out.astype(BF16)
    x = _ffn(x, norm_ffn_g[0], scale, shift, gate, ffn_w_in_b, ffn_w_out_b, 0, final_norm_g,
             final_norm=False, name="ffn0")

    shift, scale, gate = _split_mod(mod_mix[1], bsz)
    proj1 = _norm_matmul(x, norm_mix_g[1], scale, shift, ret_w_in.astype(BF16), 0, name="in_proj1")
    y_r = _retention(proj1, ret_gn_g[0])
    x = _matmul_residual([y_r], ret_w_out[0].astype(BF16), x, gate, name="out_proj1")

    shift, scale, gate = _split_mod(mod_ffn[1], bsz)
    return _ffn(x, norm_ffn_g[1], scale, shift, gate, ffn_w_in_b, ffn_w_out_b, 1, final_norm_g,
                final_norm=True, name="ffn1")
```
